```python
import numpy as np
import jax
import jax.numpy as jnp
from jax import lax

D_MODEL = 1024
BATCH = 8
SEQ = 4096
DEPTH = 2

BRANCH_WIDTH = D_MODEL // 2
N_BRANCHES = 3
A_HEAD_DIM = 64
A_HEADS = BRANCH_WIDTH // A_HEAD_DIM
IDX_HEADS = 4
IDX_DIM = 64
INDEX_TOPK = 256
B_HEAD_DIM = 64
B_HEADS = BRANCH_WIDTH // B_HEAD_DIM
C_HEAD_DIM = 128
C_HEADS = BRANCH_WIDTH // C_HEAD_DIM
CONV_WIDTH = 4
DELTA_CHUNK = 64
Q_BLOCK = 128
ROPE_THETA = 500000.0
ROT_DIM = A_HEAD_DIM // 4
FFN_DIM = 2 * D_MODEL
NORM_EPS = 1e-6
L2_EPS = 1e-6
IN_SIZES = (
    A_HEADS * A_HEAD_DIM,
    A_HEAD_DIM,
    A_HEAD_DIM,
    IDX_HEADS * IDX_DIM,
    IDX_DIM,
    IDX_HEADS,
    3 * BRANCH_WIDTH,
    B_HEADS,
    3 * BRANCH_WIDTH,
    BRANCH_WIDTH,
    C_HEADS,
    C_HEADS,
    N_BRANCHES * D_MODEL,
)
IN_DIM = 7636

kernel_name = 'hybrid_dsa_fox_gdn_macaron'


def rms_norm(x, gain):
    xf = x.astype(jnp.float32)
    y = xf * lax.rsqrt(jnp.mean(xf * xf, axis=-1, keepdims=True) + NORM_EPS)
    return (y * gain.astype(jnp.float32)).astype(x.dtype)


def l2_normalize(x):
    xf = x.astype(jnp.float32)
    return (xf * lax.rsqrt(jnp.sum(xf * xf, axis=-1, keepdims=True) + L2_EPS)).astype(x.dtype)


def rotary_tables(seq_len):
    pos = jnp.arange(seq_len, dtype=jnp.float32)
    inv_freq = jnp.power(ROPE_THETA, -jnp.arange(0, ROT_DIM, 2, dtype=jnp.float32) / ROT_DIM)
    ang = pos[:, None] * inv_freq[None, :]
    return jnp.cos(ang), jnp.sin(ang)


def apply_partial_rotary(t, cos, sin):
    half = ROT_DIM // 2
    c = cos[None, :, None, :].astype(t.dtype)
    s = sin[None, :, None, :].astype(t.dtype)
    x1 = t[..., :half]
    x2 = t[..., half:ROT_DIM]
    return jnp.concatenate([x1 * c - x2 * s, x2 * c + x1 * s, t[..., ROT_DIM:]], axis=-1)


def swiglu_ffn(h, w_in, w_out):
    gate, up = jnp.split(h @ w_in, 2, axis=-1)
    return (jax.nn.silu(gate) * up) @ w_out


def causal_depthwise_conv(x, w):
    k_width, ch = w.shape
    return lax.conv_general_dilated(
        x, w[:, None, :].astype(x.dtype), window_strides=(1,), padding=[(k_width - 1, 0)],
        dimension_numbers=('NWC', 'WIO', 'NWC'), feature_group_count=ch)


def dsa_sparse_attention(q, k, v, q_idx, k_idx, w_idx, top_k):
    b, s, h, d = q.shape
    nb = s // Q_BLOCK
    pos = jnp.arange(s)
    kv = jnp.concatenate([k, v], axis=-1)

    def to_blocks(t):
        return jnp.swapaxes(t.reshape((b, nb, Q_BLOCK) + t.shape[2:]), 0, 1)

    def block(args):
        qb, qib, wb, pb = args
        rel = jax.nn.relu(jnp.einsum('bqhd,bsd->bqhs', qib, k_idx).astype(jnp.float32) * IDX_DIM ** -0.5)
        score = jnp.einsum('bqhs,bqh->bqs', rel, wb.astype(jnp.float32))
        causal = pos[None, :] <= pb[:, None]
        score = jnp.where(causal[None], score, -jnp.inf)
        _, sel = lax.top_k(score, top_k)
        kv_sel = jax.vmap(lambda kv_b, sel_b: kv_b[sel_b])(kv, sel)
        k_sel = kv_sel[..., :d]
        v_sel = kv_sel[..., d:]
        logits = jnp.einsum('bqhd,bqkd->bhqk', qb, k_sel).astype(jnp.float32) * d ** -0.5
        valid = (sel <= pb[None, :, None])[:, None]
        p = jax.nn.softmax(jnp.where(valid, logits, -jnp.inf), axis=-1).astype(v.dtype)
        return jnp.einsum('bhqk,bqkd->bqhd', p, v_sel)

    out = lax.map(block, (to_blocks(q), to_blocks(q_idx), to_blocks(w_idx), pos.reshape(nb, Q_BLOCK)))
    return jnp.swapaxes(out, 0, 1).reshape(b, s, h, d)


def forgetting_attention(q, k, v, log_f):
    b, s, h, d = q.shape
    nb = s // Q_BLOCK
    pos = jnp.arange(s)
    c = jnp.transpose(jnp.cumsum(log_f, axis=1), (0, 2, 1))

    def block(args):
        qb, cb, pb = args
        logits = (jnp.einsum('bqhd,bshd->bhqs', qb, k).astype(jnp.float32) * d ** -0.5
                  + (cb[..., :, None] - c[..., None, :]))
        causal = pos[None, :] <= pb[:, None]
        p = jax.nn.softmax(jnp.where(causal, logits, -jnp.inf), axis=-1).astype(v.dtype)
        return jnp.einsum('bhqs,bshd->bqhd', p, v)

    q_blocks = jnp.swapaxes(q.reshape(b, nb, Q_BLOCK, h, d), 0, 1)
    c_blocks = jnp.transpose(c.reshape(b, h, nb, Q_BLOCK), (2, 0, 1, 3))
    out = lax.map(block, (q_blocks, c_blocks, pos.reshape(nb, Q_BLOCK)))
    return jnp.swapaxes(out, 0, 1).reshape(b, s, h, d)


def chunked_gated_delta_rule(q, k, v, g, beta):
    b, s, h, dk = q.shape
    dv = v.shape[-1]
    cs = DELTA_CHUNK
    n = s // cs

    def chunks(t):
        t = t.astype(jnp.float32).reshape((b, n, cs, h) + t.shape[3:])
        return jnp.moveaxis(t, (1, 3), (0, 2))

    qc, kc, vc, gc, bc = chunks(q), chunks(k), chunks(v), chunks(g), chunks(beta)
    gc = jnp.cumsum(gc, axis=-1)
    idx = jnp.arange(cs)
    incl = idx[:, None] >= idx[None, :]
    strict = idx[:, None] > idx[None, :]
    decay = jnp.exp(jnp.where(incl, gc[..., :, None] - gc[..., None, :], -jnp.inf))
    kb = kc * bc[..., None]
    a_mat = jnp.where(strict, jnp.einsum('nbhid,nbhjd->nbhij', kb, kc) * decay, 0.0)
    m = a_mat + jnp.eye(cs, dtype=jnp.float32)
    rhs = jnp.concatenate([vc * bc[..., None], kb * jnp.exp(gc)[..., None]], axis=-1)
    sol = lax.linalg.triangular_solve(m, rhs, left_side=True, lower=True, unit_diagonal=True)
    u = sol[..., :dv]
    w = sol[..., dv:]
    qk = jnp.where(incl, jnp.einsum('nbhid,nbhjd->nbhij', qc, kc) * decay, 0.0)

    def step(state, xs):
        q_i, k_i, u_i, w_i, qk_i, g_i = xs
        v_new = u_i - jnp.einsum('bhcd,bhde->bhce', w_i, state)
        o_i = (jnp.einsum('bhcd,bhde->bhce', q_i * jnp.exp(g_i)[..., None], state)
               + jnp.einsum('bhcj,bhje->bhce', qk_i, v_new))
        g_last = g_i[..., -1]
        state = (state * jnp.exp(g_last)[..., None, None]
                 + jnp.einsum('bhcd,bhce->bhde', k_i * jnp.exp(g_last[..., None] - g_i)[..., None], v_new))
        return state, o_i

    state0 = jnp.zeros((b, h, dk, dv), jnp.float32)
    _, o = lax.scan(step, state0, (qc, kc, u, w, qk, gc))
    o = jnp.moveaxis(o, (0, 2), (1, 3)).reshape(b, s, h, dv)
    return o.astype(v.dtype)


def hybrid_mixer(h, cos, sin, top_k, w_in, b_gate, b_forget, conv_w, a_log, dt_bias,
                 delta_norm, w_branch_a, w_branch_b, w_branch_c, w_out):
    b, s, _ = h.shape
    proj = h @ w_in
    (a_q, a_k, a_v, i_q, i_k, i_w, b_qkv, b_f, c_qkv, c_z, c_beta, c_a, gate_logits) = jnp.split(
        proj, np.cumsum(IN_SIZES)[:-1].tolist(), axis=-1)

    q_a = apply_partial_rotary(a_q.reshape(b, s, A_HEADS, A_HEAD_DIM), cos, sin)
    k_a = apply_partial_rotary(a_k[:, :, None, :], cos, sin)[:, :, 0]
    q_i = apply_partial_rotary(i_q.reshape(b, s, IDX_HEADS, IDX_DIM), cos, sin)
    k_i = apply_partial_rotary(i_k[:, :, None, :], cos, sin)[:, :, 0]
    w_i = i_w * IDX_HEADS ** -0.5
    y_a = dsa_sparse_attention(q_a, k_a, a_v, q_i, k_i, w_i, top_k)

    qkv_b = b_qkv.reshape(b, s, 3, B_HEADS, B_HEAD_DIM)
    log_f = jax.nn.log_sigmoid((b_f + b_forget).astype(jnp.float32))
    y_b = forgetting_attention(qkv_b[:, :, 0], qkv_b[:, :, 1], qkv_b[:, :, 2], log_f)

    qkv_c = jax.nn.silu(causal_depthwise_conv(c_qkv, conv_w)).reshape(b, s, 3, C_HEADS, C_HEAD_DIM)
    q_c = l2_normalize(qkv_c[:, :, 0]) * C_HEAD_DIM ** -0.5
    k_c = l2_normalize(qkv_c[:, :, 1])
    v_c = qkv_c[:, :, 2]
    beta = jax.nn.sigmoid(c_beta.astype(jnp.float32))
    g = -jnp.exp(a_log.astype(jnp.float32)) * jax.nn.softplus(c_a.astype(jnp.float32) + dt_bias.astype(jnp.float32))
    o_c = chunked_gated_delta_rule(q_c, k_c, v_c, g, beta)
    y_c = rms_norm(o_c, delta_norm) * jax.nn.silu(c_z.reshape(b, s, C_HEADS, C_HEAD_DIM))

    y_a = y_a.reshape(b, s, BRANCH_WIDTH) @ w_branch_a
    y_b = y_b.reshape(b, s, BRANCH_WIDTH) @ w_branch_b
    y_c = y_c.reshape(b, s, BRANCH_WIDTH) @ w_branch_c
    gates = jax.nn.sigmoid((gate_logits + b_gate).astype(jnp.float32)).astype(h.dtype)
    gates = gates.reshape(b, s, N_BRANCHES, D_MODEL)
    merged = gates[:, :, 0] * y_a + gates[:, :, 1] * y_b + gates[:, :, 2] * y_c
    return merged @ w_out


def setup_inputs(seed: int = 0) -> dict:
    key = jax.random.key(seed)
    ks = jax.random.split(key, 24)
    f32 = jnp.float32

    def nrm(k, shape, scale):
        return jax.random.normal(k, shape, f32) * scale

    def gain(k, shape):
        return 1.0 + 0.02 * jax.random.normal(k, shape, f32)

    dt = jnp.exp(jax.random.uniform(ks[10], (DEPTH, C_HEADS), f32, np.log(0.001), np.log(0.1)))
    return {
        'x': nrm(ks[0], (BATCH, SEQ, D_MODEL), 1.0),
        'ffn1_norm': gain(ks[1], (DEPTH, D_MODEL)),
        'ffn1_w_in': nrm(ks[2], (DEPTH, D_MODEL, 2 * FFN_DIM), D_MODEL ** -0.5),
        'ffn1_w_out': nrm(ks[3], (DEPTH, FFN_DIM, D_MODEL), FFN_DIM ** -0.5),
        'mix_norm': gain(ks[4], (DEPTH, D_MODEL)),
        'w_in': nrm(ks[5], (DEPTH, D_MODEL, IN_DIM), D_MODEL ** -0.5),
        'b_gate': nrm(ks[6], (DEPTH, N_BRANCHES * D_MODEL), 0.1),
        'b_forget': 3.0 + nrm(ks[7], (DEPTH, B_HEADS), 0.5),
        'conv_w': nrm(ks[8], (DEPTH, CONV_WIDTH, 3 * BRANCH_WIDTH), CONV_WIDTH ** -0.5),
        'a_log': jnp.log(jax.random.uniform(ks[9], (DEPTH, C_HEADS), f32, 1.0, 16.0)),
        'dt_bias': dt + jnp.log(-jnp.expm1(-dt)),
        'delta_norm': gain(ks[11], (DEPTH, C_HEAD_DIM)),
        'w_branch_a': nrm(ks[12], (DEPTH, BRANCH_WIDTH, D_MODEL), BRANCH_WIDTH ** -0.5),
        'w_branch_b': nrm(ks[13], (DEPTH, BRANCH_WIDTH, D_MODEL), BRANCH_WIDTH ** -0.5),
        'w_branch_c': nrm(ks[14], (DEPTH, BRANCH_WIDTH, D_MODEL), BRANCH_WIDTH ** -0.5),
        'w_out': nrm(ks[15], (DEPTH, D_MODEL, D_MODEL), D_MODEL ** -0.5),
        'ffn2_norm': gain(ks[16], (DEPTH, D_MODEL)),
        'ffn2_w_in': nrm(ks[17], (DEPTH, D_MODEL, 2 * FFN_DIM), D_MODEL ** -0.5),
        'ffn2_w_out': nrm(ks[18], (DEPTH, FFN_DIM, D_MODEL), FFN_DIM ** -0.5),
        'final_norm': gain(ks[19], (D_MODEL,)),
    }


def reference(x, ffn1_norm, ffn1_w_in, ffn1_w_out, mix_norm, w_in, b_gate, b_forget, conv_w,
              a_log, dt_bias, delta_norm, w_branch_a, w_branch_b, w_branch_c, w_out,
              ffn2_norm, ffn2_w_in, ffn2_w_out, final_norm):
    seq_len = x.shape[1]
    top_k = min(INDEX_TOPK, seq_len // 4)
    cos, sin = rotary_tables(seq_len)
    for l in range(DEPTH):
        x = x + 0.5 * swiglu_ffn(rms_norm(x, ffn1_norm[l]), ffn1_w_in[l], ffn1_w_out[l])
        x = x + hybrid_mixer(rms_norm(x, mix_norm[l]), cos, sin, top_k, w_in[l], b_gate[l],
                             b_forget[l], conv_w[l], a_log[l], dt_bias[l], delta_norm[l],
                             w_branch_a[l], w_branch_b[l], w_branch_c[l], w_out[l])
        x = x + 0.5 * swiglu_ffn(rms_norm(x, ffn2_norm[l]), ffn2_w_in[l], ffn2_w_out[l])
    return rms_norm(x, final_norm)
```

```python
import functools

import numpy as np
import jax
import jax.numpy as jnp
from jax import lax
from jax.experimental import pallas as pl
from jax.experimental.pallas import tpu as pltpu

F32 = jnp.float32
BF16 = jnp.bfloat16

D_MODEL = 1024
BRANCH = 512
A_HEADS = 8
HEAD_DIM = 64
IDX_HEADS = 4
INDEX_TOPK = 256
B_HEADS = 8
C_HEADS = 4
C_DIM = 128
CONV_WIDTH = 4
ROPE_THETA = 500000.0
ROT_DIM = 16
FFN_DIM = 2048
NORM_EPS = 1e-6
L2_EPS = 1e-6
IN_SIZES = (512, 64, 64, 256, 64, 4, 1536, 8, 1536, 512, 4, 4, 3072)

LANES = 128
SC_BF = 0
SC_IW = 8
SC_BETA = 12
SC_DECAY = 16
W_A, W_SC, W_B, W_CQKV, W_CZ, W_G = 1024, 128, 1536, 1536, 512, 3072
W_TOTAL = W_A + W_SC + W_B + W_CQKV + W_CZ + W_G
ROT_TILES = 7

NEG_BIG = -1e30
NEG_MASK = -2e30
VMEM_LIMIT = 56 * 1024 * 1024


def _cparams(sem):
    return pltpu.CompilerParams(dimension_semantics=sem, vmem_limit_bytes=VMEM_LIMIT)


def _dot(a, b):
    return jnp.dot(a, b, preferred_element_type=F32)


def _dot_hi(a, b):
    return jnp.dot(a, b, preferred_element_type=F32, precision=lax.Precision.HIGHEST)


def _dot_nt(a, b):
    return lax.dot_general(a, b, (((1,), (1,)), ((), ())), preferred_element_type=F32)


def _rms(x, gain):
    return x * lax.rsqrt(jnp.mean(x * x, axis=-1, keepdims=True) + NORM_EPS) * gain


def _silu(x):
    return x * jax.nn.sigmoid(x)


def _resident(shape):
    nd = len(shape)
    return pl.BlockSpec(shape, lambda *_: (0,) * nd, pipeline_mode=pl.Buffered(1))


FFN_TM = 512
FFN_CHUNK = 512


def _ffn_kernel(x_ref, g_ref, win_ref, wout_ref, fg_ref, o_ref, *, final):
    x = x_ref[...]
    h = _rms(x, g_ref[...]).astype(BF16)
    acc = jnp.zeros(x.shape, F32)
    for c in range(FFN_DIM // FFN_CHUNK):
        lo = c * FFN_CHUNK
        gate = _dot(h, win_ref[:, lo:lo + FFN_CHUNK])
        up = _dot(h, win_ref[:, FFN_DIM + lo:FFN_DIM + lo + FFN_CHUNK])
        act = (_silu(gate) * up).astype(BF16)
        acc = acc + _dot(act, wout_ref[lo:lo + FFN_CHUNK, :])
    y = x + 0.5 * acc
    if final:
        y = _rms(y, fg_ref[...])
    o_ref[...] = y


def _ffn(x, gain, w_in, w_out, final_gain, final):
    n = x.shape[0]
    tm = min(FFN_TM, n)
    return pl.pallas_call(
        functools.partial(_ffn_kernel, final=final),
        grid=(n // tm,),
        in_specs=[
            pl.BlockSpec((tm, D_MODEL), lambda i: (i, 0)),
            _resident((1, D_MODEL)),
            _resident((D_MODEL, 2 * FFN_DIM)),
            _resident((FFN_DIM, D_MODEL)),
            _resident((1, D_MODEL)),
        ],
        out_specs=pl.BlockSpec((tm, D_MODEL), lambda i: (i, 0)),
        out_shape=jax.ShapeDtypeStruct((n, D_MODEL), F32),
        compiler_params=_cparams(("parallel",)),
        name="ffn_half",
    )(x, gain, w_in, w_out, final_gain)


PROJ_TM = 256
PROJ_CHUNK = 512


def _proj_kernel(x_ref, g_ref, w_ref, cos_ref, s1_ref, s2_ref,
                 a_ref, sc_ref, b_ref, cqkv_ref, cz_ref, gl_ref):
    h = _rms(x_ref[...], g_ref[...]).astype(BF16)
    cos, s1, s2 = cos_ref[...], s1_ref[...], s2_ref[...]

    off = 0
    for j in range(W_A // LANES):
        t = _dot(h, w_ref[:, off:off + LANES])
        if j < ROT_TILES:
            t = t * cos + pltpu.roll(t, 8, 1) * s1 + pltpu.roll(t, LANES - 8, 1) * s2
        a_ref[:, off:off + LANES] = t.astype(BF16)
        off += LANES
    sc_ref[...] = _dot(h, w_ref[:, off:off + W_SC])
    off += W_SC
    for ref, width in ((b_ref, W_B), (cqkv_ref, W_CQKV), (cz_ref, W_CZ), (gl_ref, W_G)):
        for lo in range(0, width, PROJ_CHUNK):
            ref[:, lo:lo + PROJ_CHUNK] = _dot(
                h, w_ref[:, off + lo:off + lo + PROJ_CHUNK]).astype(ref.dtype)
        off += width


def _proj(x, gain, w, cos, s1, s2, seq):
    n = x.shape[0]
    tm = min(PROJ_TM, seq)
    per_seq = seq // tm
    row = lambda i: (i, 0)
    tab = lambda i: (i % per_seq, 0)
    widths = (W_A, W_SC, W_B, W_CQKV, W_CZ, W_G)
    dtypes = (BF16, F32, BF16, F32, F32, F32)
    return pl.pallas_call(
        _proj_kernel,
        grid=(n // tm,),
        in_specs=[
            pl.BlockSpec((tm, D_MODEL), row),
            _resident((1, D_MODEL)),
            _resident((D_MODEL, W_TOTAL)),
            pl.BlockSpec((tm, LANES), tab),
            pl.BlockSpec((tm, LANES), tab),
            pl.BlockSpec((tm, LANES), tab),
        ],
        out_specs=[pl.BlockSpec((tm, wd), row) for wd in widths],
        out_shape=[jax.ShapeDtypeStruct((n, wd), dt) for wd, dt in zip(widths, dtypes)],
        compiler_params=_cparams(("parallel",)),
        name="mixer_proj",
    )(x, gain, w, cos, s1, s2)


DSA_TQ = 128
DSA_CK = 512
INT_MIN = -2 ** 31
KEY_NEG_INF = int(np.array(-np.inf, np.float32).view(np.int32)) ^ 0x7FFFFFFF


def _dsa_kernel(qa_ref, qi_ref, kk_ref, vv_ref, sc_ref, o_ref,
                key_ref, bias_ref, qs_ref, *, top_k, ck):
    i = pl.program_id(1)
    tq = qa_ref.shape[1]
    nck = (i * tq + tq + ck - 1) // ck
    lane = lax.broadcasted_iota(jnp.int32, (tq, LANES), 1)
    lo_half = lane < HEAD_DIM
    zero = jnp.zeros((tq, LANES), BF16)

    for j in range(A_HEADS // 2):
        t = qa_ref[0, :, j * LANES:(j + 1) * LANES].astype(F32)
        qs_ref[2 * j] = jnp.where(lo_half, t, 0.0).astype(BF16)
        qs_ref[2 * j + 1] = jnp.where(lo_half, pltpu.roll(t, HEAD_DIM, 1), 0.0).astype(BF16)
    for j in range(IDX_HEADS // 2):
        t = qi_ref[0, :, j * LANES:(j + 1) * LANES].astype(F32)
        qs_ref[A_HEADS + 2 * j] = jnp.where(lo_half, 0.0, pltpu.roll(t, HEAD_DIM, 1)).astype(BF16)
        qs_ref[A_HEADS + 2 * j + 1] = jnp.where(lo_half, 0.0, t).astype(BF16)
    del zero

    w_idx = sc_ref[0, :, SC_IW:SC_IW + IDX_HEADS] * (IDX_HEADS ** -0.5)
    row_pos = i * tq + lax.broadcasted_iota(jnp.int32, (tq, ck), 0)
    col_iota = lax.broadcasted_iota(jnp.int32, (tq, ck), 1)

    def score_chunk(c, _):
        kc = kk_ref[0, pl.ds(pl.multiple_of(c * ck, ck), ck), :]
        score = jnp.zeros((tq, ck), F32)
        for h in range(IDX_HEADS):
            rel = jnp.maximum(_dot_nt(qs_ref[A_HEADS + h], kc) * (HEAD_DIM ** -0.5), 0.0)
            score = score + rel * w_idx[:, h:h + 1]
        score = jnp.where(score == 0.0, 0.0, score)
        score = jnp.where(col_iota + c * ck <= row_pos, score, -jnp.inf)
        bits = lax.bitcast_convert_type(score, jnp.int32)
        key_ref[c] = jnp.where(bits < 0, bits ^ 0x7FFFFFFF, bits)
        return 0

    lax.fori_loop(0, nck, score_chunk, 0)

    def count(pred):
        def body(c, acc):
            kc = key_ref[c]
            for j in range(ck // LANES):
                acc = acc + jnp.where(pred(kc[:, j * LANES:(j + 1) * LANES]), 1.0, 0.0)
            return acc
        acc = lax.fori_loop(0, nck, body, jnp.zeros((tq, LANES), F32))
        return jnp.sum(acc, axis=-1, keepdims=True)

    kf = float(top_k)

    def bit_step(b, thr):
        cand = thr + lax.shift_left(jnp.int32(1), 31 - b)
        cnt = count(lambda kc: kc >= cand)
        return jnp.where(cnt >= kf, cand, thr)

    thr = lax.fori_loop(0, 32, bit_step, jnp.full((tq, 1), INT_MIN, jnp.int32))
    thr = jnp.maximum(thr, KEY_NEG_INF + 1)
    cnt_ge = count(lambda kc: kc >= thr)

    def write_bias(c, _):
        bias_ref[c] = jnp.where(key_ref[c] >= thr, 0.0, NEG_MASK)
        return 0

    lax.fori_loop(0, nck, write_bias, 0)

    @pl.when(jnp.max(cnt_ge) > kf)
    def _():
        need = kf - count(lambda kc: kc > thr)
        r = lax.broadcasted_iota(jnp.int32, (ck, ck), 0)
        cidx = lax.broadcasted_iota(jnp.int32, (ck, ck), 1)
        upper = jnp.where(r <= cidx, 1.0, 0.0).astype(BF16)

        def tie_chunk(c, before):
            kc = key_ref[c]
            tie = kc == thr
            rank = before + _dot(jnp.where(tie, 1.0, 0.0).astype(BF16), upper)
            keep = (kc > thr) | (tie & (rank <= need))
            bias_ref[c] = jnp.where(keep, 0.0, NEG_MASK)
            return rank[:, ck - 1:ck]

        lax.fori_loop(0, nck, tie_chunk, jnp.zeros((tq, 1), F32))

    scale = HEAD_DIM ** -0.5
    for j in range(A_HEADS // 2):
        outs = []
        for h in (2 * j, 2 * j + 1):
            qh = qs_ref[h]

            def attend(c, carry, qh=qh):
                m, l, acc = carry
                rows = pl.ds(pl.multiple_of(c * ck, ck), ck)
                s = _dot_nt(qh, kk_ref[0, rows, :]) * scale + bias_ref[c]
                m_new = jnp.maximum(m, jnp.max(s, axis=-1, keepdims=True))
                alpha = jnp.exp(m - m_new)
                p = jnp.exp(s - m_new)
                l = alpha * l + jnp.sum(p, axis=-1, keepdims=True)
                acc = alpha * acc + _dot(p.astype(BF16), vv_ref[0, rows, :])
                return m_new, l, acc

            m, l, acc = lax.fori_loop(
                0, nck, attend,
                (jnp.full((tq, 1), NEG_BIG, F32), jnp.zeros((tq, 1), F32),
                 jnp.zeros((tq, LANES), F32)))
            outs.append(acc / l)
        o_ref[0, :, j * LANES:(j + 1) * LANES] = (
            outs[0] + pltpu.roll(outs[1], HEAD_DIM, 1)).astype(o_ref.dtype)


def _dsa(a_grp, sc, top_k):
    b, s, _ = a_grp.shape
    tq = min(DSA_TQ, s)
    ck = min(DSA_CK, s)
    return pl.pallas_call(
        functools.partial(_dsa_kernel, top_k=top_k, ck=ck),
        grid=(b, s // tq),
        in_specs=[
            pl.BlockSpec((1, tq, 512), lambda bi, i: (bi, i, 0)),
            pl.BlockSpec((1, tq, 256), lambda bi, i: (bi, i, 2)),
            pl.BlockSpec((1, s, LANES), lambda bi, i: (bi, 0, 6)),
            pl.BlockSpec((1, s, LANES), lambda bi, i: (bi, 0, 7)),
            pl.BlockSpec((1, tq, LANES), lambda bi, i: (bi, i, 0)),
        ],
        out_specs=pl.BlockSpec((1, tq, BRANCH), lambda bi, i: (bi, i, 0)),
        out_shape=jax.ShapeDtypeStruct((b, s, BRANCH), BF16),
        scratch_shapes=[
            pltpu.VMEM((s // ck, tq, ck), jnp.int32),
            pltpu.VMEM((s // ck, tq, ck), F32),
            pltpu.VMEM((A_HEADS + IDX_HEADS, tq, LANES), BF16),
        ],
        compiler_params=_cparams(("parallel", "arbitrary")),
        name="dsa_attention",
    )(a_grp, a_grp, a_grp, a_grp, sc)


CUM_CHUNK = 256


def _fox_gate_kernel(sc_ref, bf_ref, o_ref, *, chunk):
    s = sc_ref.shape[1]
    r = lax.broadcasted_iota(jnp.int32, (chunk, chunk), 0)
    c = lax.broadcasted_iota(jnp.int32, (chunk, chunk), 1)
    lower = jnp.where(c <= r, 1.0, 0.0)
    carry = jnp.zeros((1, LANES), F32)
    for n in range(s // chunk):
        x = sc_ref[0, n * chunk:(n + 1) * chunk, :] + bf_ref[...]
        log_f = jnp.minimum(x, 0.0) - jnp.log1p(jnp.exp(-jnp.abs(x)))
        cum = _dot_hi(lower, log_f) + carry
        carry = cum[chunk - 1:chunk, :]
        o_ref[0, :, n * chunk:(n + 1) * chunk] = -(cum.T[:B_HEADS, :])


def _fox_gate(sc, bf_row):
    b, s, _ = sc.shape
    chunk = min(CUM_CHUNK, s)
    return pl.pallas_call(
        functools.partial(_fox_gate_kernel, chunk=chunk),
        grid=(b,),
        in_specs=[pl.BlockSpec((1, s, LANES), lambda bi: (bi, 0, 0)), _resident((1, LANES))],
        out_specs=pl.BlockSpec((1, B_HEADS, s), lambda bi: (bi, 0, 0)),
        out_shape=jax.ShapeDtypeStruct((b, B_HEADS, s), F32),
        compiler_params=_cparams(("parallel",)),
        name="fox_gate_cumsum",
    )(sc, bf_row)


FOX_T = 256


def _fox_kernel(q_ref, k_ref, v_ref, nc_ref, o_ref, *, t):
    hp = pl.program_id(1)
    i = pl.program_id(2)
    lane = lax.broadcasted_iota(jnp.int32, (t, LANES), 1)
    lo_half = lane < HEAD_DIM
    q = q_ref[0]
    scale = HEAD_DIM ** -0.5
    causal = (lax.broadcasted_iota(jnp.int32, (t, t), 1)
              <= lax.broadcasted_iota(jnp.int32, (t, t), 0))
    outs = []
    for j in range(2):
        keep = lo_half if j == 0 else jnp.logical_not(lo_half)
        qh = jnp.where(keep, q, jnp.zeros_like(q))
        head = 2 * hp + j

        def attend(c, carry, masked, qh=qh, head=head):
            m, l, acc = carry
            rows = pl.ds(pl.multiple_of(c * t, t), t)
            s = _dot_nt(qh, k_ref[0, rows, :]) * scale + nc_ref[0, head, pl.ds(c, 1), :]
            if masked:
                s = jnp.where(causal, s, -jnp.inf)
            m_new = jnp.maximum(m, jnp.max(s, axis=-1, keepdims=True))
            alpha = jnp.exp(m - m_new)
            p = jnp.exp(s - m_new)
            l = alpha * l + jnp.sum(p, axis=-1, keepdims=True)
            acc = alpha * acc + _dot(p.astype(BF16), v_ref[0, rows, :])
            return m_new, l, acc

        carry = (jnp.full((t, 1), NEG_BIG, F32), jnp.zeros((t, 1), F32),
                 jnp.zeros((t, LANES), F32))
        carry = lax.fori_loop(0, i, functools.partial(attend, masked=False), carry)
        m, l, acc = attend(i, carry, masked=True)
        outs.append(acc / l)
    o_ref[0] = jnp.where(lo_half, outs[0], outs[1]).astype(o_ref.dtype)


def _fox(b_grp, neg_c):
    b, s, _ = b_grp.shape
    t = min(FOX_T, s)
    pairs = B_HEADS // 2
    neg_c = neg_c.reshape(b, B_HEADS, s // t, t)
    return pl.pallas_call(
        functools.partial(_fox_kernel, t=t),
        grid=(b, pairs, s // t),
        in_specs=[
            pl.BlockSpec((1, t, LANES), lambda bi, hp, i: (bi, i, hp)),
            pl.BlockSpec((1, s, LANES), lambda bi, hp, i: (bi, 0, pairs + hp)),
            pl.BlockSpec((1, s, LANES), lambda bi, hp, i: (bi, 0, 2 * pairs + hp)),
            pl.BlockSpec((1, B_HEADS, s // t, t), lambda bi, hp, i: (bi, 0, 0, 0)),
        ],
        out_specs=pl.BlockSpec((1, t, LANES), lambda bi, hp, i: (bi, i, hp)),
        out_shape=jax.ShapeDtypeStruct((b, s, BRANCH), BF16),
        compiler_params=_cparams(("parallel", "parallel", "arbitrary")),
        name="fox_attention",
    )(b_grp, b_grp, b_grp, neg_c)


GDN_TS = 256
GDN_C = 128
TAIL = 8


def _neumann_inverse(a):
    n = a.shape[0]
    eye = jnp.where(lax.broadcasted_iota(jnp.int32, (n, n), 0)
                    == lax.broadcasted_iota(jnp.int32, (n, n), 1), 1.0, 0.0)
    inv = eye - a
    p = a
    steps = int(np.log2(n)) - 1
    for _ in range(steps):
        p = _dot_hi(p, p)
        inv = inv + _dot_hi(inv, p)
    return inv


def _gdn_kernel(x_ref, z_ref, sc_ref, cw_ref, alog_ref, dtb_ref, dn_ref, o_ref,
                state_ref, tail_ref, ext_ref, *, ts):
    j = pl.program_id(1)
    width = x_ref.shape[2]

    @pl.when(j == 0)
    def _():
        state_ref[...] = jnp.zeros_like(state_ref)
        tail_ref[...] = jnp.zeros_like(tail_ref)

    x = x_ref[0]
    ext_ref[0:TAIL, :] = tail_ref[...]
    ext_ref[TAIL:TAIL + ts, :] = x
    tail_ref[...] = x[ts - TAIL:ts, :]
    conv = jnp.zeros((ts, width), F32)
    for tap in range(CONV_WIDTH):
        start = TAIL - (CONV_WIDTH - 1) + tap
        conv = conv + cw_ref[tap:tap + 1, :] * ext_ref[start:start + ts, :]
    act = _silu(conv)

    sc = sc_ref[0]
    beta_all = jax.nn.sigmoid(sc)
    xg = sc + dtb_ref[...]
    softplus = jnp.maximum(xg, 0.0) + jnp.log1p(jnp.exp(-jnp.abs(xg)))
    g_all = -jnp.exp(alog_ref[...]) * softplus
    r = lax.broadcasted_iota(jnp.int32, (ts, ts), 0)
    c = lax.broadcasted_iota(jnp.int32, (ts, ts), 1)
    same_chunk_lower = jnp.where((c <= r) & (r // GDN_C == c // GDN_C), 1.0, 0.0)
    gc_all = _dot_hi(same_chunk_lower, g_all)
    gc_t = gc_all.T

    ri = lax.broadcasted_iota(jnp.int32, (GDN_C, GDN_C), 0)
    ci = lax.broadcasted_iota(jnp.int32, (GDN_C, GDN_C), 1)
    incl = ri >= ci
    strict = ri > ci

    def l2n(t):
        return t * lax.rsqrt(jnp.sum(t * t, axis=-1, keepdims=True) + L2_EPS)

    for h in range(C_HEADS):
        for n in range(ts // GDN_C):
            r0 = n * GDN_C
            rows = slice(r0, r0 + GDN_C)
            q = l2n(act[rows, h * C_DIM:(h + 1) * C_DIM]) * (C_DIM ** -0.5)
            k = l2n(act[rows, BRANCH + h * C_DIM:BRANCH + (h + 1) * C_DIM])
            v = act[rows, 2 * BRANCH + h * C_DIM:2 * BRANCH + (h + 1) * C_DIM]
            beta = beta_all[rows, SC_BETA + h:SC_BETA + h + 1]
            gcol = gc_all[rows, SC_DECAY + h:SC_DECAY + h + 1]
            grow = gc_t[SC_DECAY + h:SC_DECAY + h + 1, r0:r0 + GDN_C]
            decay = jnp.exp(jnp.where(incl, gcol - grow, -jnp.inf))
            kb = k * beta
            k16 = k.astype(BF16)
            a_mat = jnp.where(strict, _dot_nt(kb.astype(BF16), k16) * decay, 0.0)
            t_inv = _neumann_inverse(a_mat)
            u = _dot_hi(t_inv, v * beta)
            w = _dot_hi(t_inv, kb * jnp.exp(gcol))
            qk = jnp.where(incl, _dot_nt(q.astype(BF16), k16) * decay, 0.0)

            state = state_ref[h]
            s16 = state.astype(BF16)
            v_new = u - _dot(w.astype(BF16), s16)
            vn16 = v_new.astype(BF16)
            o = _dot((q * jnp.exp(gcol)).astype(BF16), s16) + _dot(qk.astype(BF16), vn16)
            g_last = gcol[GDN_C - 1:GDN_C, :]
            kd = k * jnp.exp(g_last - gcol)
            state_ref[h] = state * jnp.exp(g_last) + _dot(kd.T.astype(BF16), vn16)

            z = z_ref[0, rows, h * C_DIM:(h + 1) * C_DIM]
            y = _rms(o, dn_ref[...]) * _silu(z)
            o_ref[0, rows, h * C_DIM:(h + 1) * C_DIM] = y.astype(o_ref.dtype)


def _gdn(cqkv, cz, sc, conv_w, alog_row, dtb_row, dn_row):
    b, s, width = cqkv.shape
    ts = min(GDN_TS, s)
    blk = lambda bi, j: (bi, j, 0)
    return pl.pallas_call(
        functools.partial(_gdn_kernel, ts=ts),
        grid=(b, s // ts),
        in_specs=[
            pl.BlockSpec((1, ts, width), blk),
            pl.BlockSpec((1, ts, BRANCH), blk),
            pl.BlockSpec((1, ts, LANES), blk),
            _resident((CONV_WIDTH, width)),
            _resident((1, LANES)),
            _resident((1, LANES)),
            _resident((1, C_DIM)),
        ],
        out_specs=pl.BlockSpec((1, ts, BRANCH), blk),
        out_shape=jax.ShapeDtypeStruct((b, s, BRANCH), BF16),
        scratch_shapes=[
            pltpu.VMEM((C_HEADS, C_DIM, C_DIM), F32),
            pltpu.VMEM((TAIL, width), F32),
            pltpu.VMEM((TAIL + ts, width), F32),
        ],
        compiler_params=_cparams(("parallel", "arbitrary")),
        name="gated_deltanet",
    )(cqkv, cz, sc, conv_w, alog_row, dtb_row, dn_row)


MERGE_TM = 512


def _merge_kernel(x_ref, ya_ref, yb_ref, yc_ref, gl_ref, bg_ref,
                  wa_ref, wb_ref, wc_ref, wo_ref, o_ref):
    merged = None
    for n, (y_ref, w_ref) in enumerate(((ya_ref, wa_ref), (yb_ref, wb_ref), (yc_ref, wc_ref))):
        cols = slice(n * D_MODEL, (n + 1) * D_MODEL)
        gate = jax.nn.sigmoid(gl_ref[:, cols] + bg_ref[:, cols])
        term = gate * _dot(y_ref[...], w_ref[...])
        merged = term if merged is None else merged + term
    o_ref[...] = x_ref[...] + _dot(merged.astype(BF16), wo_ref[...])


def _merge(x, ya, yb, yc, gl, b_gate, wa, wb, wc, wo):
    n = x.shape[0]
    tm = min(MERGE_TM, n)
    row = lambda i: (i, 0)
    return pl.pallas_call(
        _merge_kernel,
        grid=(n // tm,),
        in_specs=[
            pl.BlockSpec((tm, D_MODEL), row),
            pl.BlockSpec((tm, BRANCH), row),
            pl.BlockSpec((tm, BRANCH), row),
            pl.BlockSpec((tm, BRANCH), row),
            pl.BlockSpec((tm, 3 * D_MODEL), row),
            _resident((1, 3 * D_MODEL)),
            _resident((BRANCH, D_MODEL)),
            _resident((BRANCH, D_MODEL)),
            _resident((BRANCH, D_MODEL)),
            _resident((D_MODEL, D_MODEL)),
        ],
        out_specs=pl.BlockSpec((tm, D_MODEL), row),
        out_shape=jax.ShapeDtypeStruct((n, D_MODEL), F32),
        compiler_params=_cparams(("parallel",)),
        name="gated_merge",
    )(x, ya, yb, yc, gl, b_gate, wa, wb, wc, wo)


def _layout_w_in(w_in):
    offs = np.concatenate([[0], np.cumsum(IN_SIZES)])
    seg = lambda n: w_in[:, offs[n]:offs[n + 1]]
    (a_q, a_k, a_v, i_q, i_k, i_w, b_qkv, b_f, c_qkv, c_z, c_beta, c_a, gates) = (
        seg(n) for n in range(len(IN_SIZES)))
    zeros = lambda n: jnp.zeros((w_in.shape[0], n), w_in.dtype)
    a_grp = jnp.concatenate([a_q, i_q, a_k, i_k, a_v, zeros(HEAD_DIM)], axis=1)
    sc_grp = jnp.concatenate([b_f, i_w, c_beta, c_a, zeros(W_SC - 20)], axis=1)
    return jnp.concatenate([a_grp, sc_grp, b_qkv, c_qkv, c_z, gates], axis=1).astype(BF16)


def _rotary_tables(seq):
    pos = jnp.arange(seq, dtype=F32)
    inv_freq = jnp.power(ROPE_THETA, -jnp.arange(0, ROT_DIM, 2, dtype=F32) / ROT_DIM)
    ang = pos[:, None] * inv_freq[None, :]
    cos, sin = jnp.cos(ang), jnp.sin(ang)
    half = ROT_DIM // 2
    ones = jnp.ones((seq, HEAD_DIM - ROT_DIM), F32)
    zeros_h = jnp.zeros((seq, half), F32)
    zeros_r = jnp.zeros((seq, HEAD_DIM - ROT_DIM), F32)
    c64 = jnp.concatenate([cos, cos, ones], axis=1)
    s1_64 = jnp.concatenate([zeros_h, sin, zeros_r], axis=1)
    s2_64 = jnp.concatenate([-sin, zeros_h, zeros_r], axis=1)
    twice = lambda t: jnp.concatenate([t, t], axis=1)
    return twice(c64), twice(s1_64), twice(s2_64)


def _lane_row(values, start):
    return jnp.zeros((1, LANES), F32).at[0, start:start + values.shape[0]].set(values.astype(F32))


def kernel(x, ffn1_norm, ffn1_w_in, ffn1_w_out, mix_norm, w_in, b_gate, b_forget, conv_w, a_log, dt_bias, delta_norm, w_branch_a, w_branch_b, w_branch_c, w_out, ffn2_norm, ffn2_w_in, ffn2_w_out, final_norm):
    batch, seq, _ = x.shape
    depth = w_in.shape[0]
    top_k = min(INDEX_TOPK, seq // 4)
    cos, s1, s2 = _rotary_tables(seq)
    final_row = final_norm.reshape(1, D_MODEL)
    xt = x.reshape(batch * seq, D_MODEL)
    for l in range(depth):
        xt = _ffn(xt, ffn1_norm[l].reshape(1, D_MODEL), ffn1_w_in[l].astype(BF16),
                  ffn1_w_out[l].astype(BF16), final_row, False)
        a_grp, sc, b_grp, cqkv, cz, gl = _proj(
            xt, mix_norm[l].reshape(1, D_MODEL), _layout_w_in(w_in[l]), cos, s1, s2, seq)
        a_grp = a_grp.reshape(batch, seq, W_A)
        sc = sc.reshape(batch, seq, W_SC)
        y_a = _dsa(a_grp, sc, top_k)
        neg_c = _fox_gate(sc, _lane_row(b_forget[l], SC_BF))
        y_b = _fox(b_grp.reshape(batch, seq, W_B), neg_c)
        y_c = _gdn(cqkv.reshape(batch, seq, W_CQKV), cz.reshape(batch, seq, W_CZ), sc,
                   conv_w[l], _lane_row(a_log[l], SC_DECAY), _lane_row(dt_bias[l], SC_DECAY),
                   delta_norm[l].reshape(1, C_DIM))
        flat = lambda t: t.reshape(batch * seq, BRANCH)
        xt = _merge(xt, flat(y_a), flat(y_b), flat(y_c), gl, b_gate[l].reshape(1, 3 * D_MODEL),
                    w_branch_a[l].astype(BF16), w_branch_b[l].astype(BF16),
                    w_branch_c[l].astype(BF16), w_out[l].astype(BF16))
        xt = _ffn(xt, ffn2_norm[l].reshape(1, D_MODEL), ffn2_w_in[l].astype(BF16),
                  ffn2_w_out[l].astype(BF16), final_row, l == depth - 1)
    return xt.reshape(batch, seq, D_MODEL)
```

```python
import functools

import numpy as np
import jax
import jax.numpy as jnp
from jax import lax
from jax.experimental import pallas as pl
from jax.experimental.pallas import tpu as pltpu

F32 = jnp.float32
BF16 = jnp.bfloat16

D_MODEL = 1024
BRANCH = 512
A_HEADS = 8
HEAD_DIM = 64
IDX_HEADS = 4
INDEX_TOPK = 256
B_HEADS = 8
C_HEADS = 4
C_DIM = 128
CONV_WIDTH = 4
ROPE_THETA = 500000.0
ROT_DIM = 16
FFN_DIM = 2048
NORM_EPS = 1e-6
L2_EPS = 1e-6
IN_SIZES = (512, 64, 64, 256, 64, 4, 1536, 8, 1536, 512, 4, 4, 3072)

LANES = 128
SC_BF = 0
SC_IW = 8
SC_BETA = 12
SC_DECAY = 16
W_A, W_SC, W_B, W_CQKV, W_CZ, W_G = 1024, 128, 1536, 1536, 512, 3072
W_TOTAL = W_A + W_SC + W_B + W_CQKV + W_CZ + W_G
ROT_TILES = 7

NEG_BIG = -1e30
NEG_MASK = -2e30
VMEM_LIMIT = 56 * 1024 * 1024


def _cparams(sem):
    return pltpu.CompilerParams(dimension_semantics=sem, vmem_limit_bytes=VMEM_LIMIT)


def _dot(a, b):
    return jnp.dot(a, b, preferred_element_type=F32)


def _dot_hi(a, b):
    return jnp.dot(a, b, preferred_element_type=F32, precision=lax.Precision.HIGHEST)


def _dot_nt(a, b):
    return lax.dot_general(a, b, (((1,), (1,)), ((), ())), preferred_element_type=F32)


def _rms(x, gain):
    return x * lax.rsqrt(jnp.mean(x * x, axis=-1, keepdims=True) + NORM_EPS) * gain


def _silu(x):
    return x * jax.nn.sigmoid(x)


def _resident(shape):
    nd = len(shape)
    return pl.BlockSpec(shape, lambda *_: (0,) * nd, pipeline_mode=pl.Buffered(1))


FFN_TM = 512
FFN_CHUNK = 512


def _ffn_kernel(x_ref, g_ref, win_ref, wout_ref, fg_ref, o_ref, *, final):
    x = x_ref[...]
    h = _rms(x, g_ref[...]).astype(BF16)
    acc = jnp.zeros(x.shape, F32)
    for c in range(FFN_DIM // FFN_CHUNK):
        lo = c * FFN_CHUNK
        gate = _dot(h, win_ref[:, lo:lo + FFN_CHUNK])
        up = _dot(h, win_ref[:, FFN_DIM + lo:FFN_DIM + lo + FFN_CHUNK])
        act = (_silu(gate) * up).astype(BF16)
        acc = acc + _dot(act, wout_ref[lo:lo + FFN_CHUNK, :])
    y = x + 0.5 * acc
    if final:
        y = _rms(y, fg_ref[...])
    o_ref[...] = y


def _ffn(x, gain, w_in, w_out, final_gain, final):
    n = x.shape[0]
    tm = min(FFN_TM, n)
    return pl.pallas_call(
        functools.partial(_ffn_kernel, final=final),
        grid=(n // tm,),
        in_specs=[
            pl.BlockSpec((tm, D_MODEL), lambda i: (i, 0)),
            _resident((1, D_MODEL)),
            _resident((D_MODEL, 2 * FFN_DIM)),
            _resident((FFN_DIM, D_MODEL)),
            _resident((1, D_MODEL)),
        ],
        out_specs=pl.BlockSpec((tm, D_MODEL), lambda i: (i, 0)),
        out_shape=jax.ShapeDtypeStruct((n, D_MODEL), F32),
        compiler_params=_cparams(("parallel",)),
        name="ffn_half",
    )(x, gain, w_in, w_out, final_gain)


PROJ_TM = 256
PROJ_CHUNK = 512


def _proj_kernel(x_ref, g_ref, w_ref, cos_ref, s1_ref, s2_ref,
                 a_ref, sc_ref, b_ref, cqkv_ref, cz_ref, gl_ref):
    h = _rms(x_ref[...], g_ref[...]).astype(BF16)
    cos, s1, s2 = cos_ref[...], s1_ref[...], s2_ref[...]

    off = 0
    for j in range(W_A // LANES):
        t = _dot(h, w_ref[:, off:off + LANES])
        if j < ROT_TILES:
            t = t * cos + pltpu.roll(t, 8, 1) * s1 + pltpu.roll(t, LANES - 8, 1) * s2
        a_ref[:, off:off + LANES] = t.astype(BF16)
        off += LANES
    sc_ref[...] = _dot(h, w_ref[:, off:off + W_SC])
    off += W_SC
    for ref, width in ((b_ref, W_B), (cqkv_ref, W_CQKV), (cz_ref, W_CZ), (gl_ref, W_G)):
        for lo in range(0, width, PROJ_CHUNK):
            ref[:, lo:lo + PROJ_CHUNK] = _dot(
                h, w_ref[:, off + lo:off + lo + PROJ_CHUNK]).astype(ref.dtype)
        off += width


def _proj(x, gain, w, cos, s1, s2, seq):
    n = x.shape[0]
    tm = min(PROJ_TM, seq)
    per_seq = seq // tm
    row = lambda i: (i, 0)
    tab = lambda i: (i % per_seq, 0)
    widths = (W_A, W_SC, W_B, W_CQKV, W_CZ, W_G)
    dtypes = (BF16, F32, BF16, F32, F32, F32)
    return pl.pallas_call(
        _proj_kernel,
        grid=(n // tm,),
        in_specs=[
            pl.BlockSpec((tm, D_MODEL), row),
            _resident((1, D_MODEL)),
            _resident((D_MODEL, W_TOTAL)),
            pl.BlockSpec((tm, LANES), tab),
            pl.BlockSpec((tm, LANES), tab),
            pl.BlockSpec((tm, LANES), tab),
        ],
        out_specs=[pl.BlockSpec((tm, wd), row) for wd in widths],
        out_shape=[jax.ShapeDtypeStruct((n, wd), dt) for wd, dt in zip(widths, dtypes)],
        compiler_params=_cparams(("parallel",)),
        name="mixer_proj",
    )(x, gain, w, cos, s1, s2)


DSA_TQ = 128
DSA_CK = 512
INT_MIN = -2 ** 31
KEY_NEG_INF = int(np.array(-np.inf, np.float32).view(np.int32)) ^ 0x7FFFFFFF


def _dsa_kernel(qa_ref, qi_ref, kk_ref, vv_ref, sc_ref, o_ref,
                key_ref, bias_ref, qs_ref, m_ref, l_ref, alpha_ref, acc_ref, p_ref, *, top_k, ck):
    i = pl.program_id(1)
    tq = qa_ref.shape[1]
    nck = (i * tq + tq + ck - 1) // ck
    lane = lax.broadcasted_iota(jnp.int32, (tq, LANES), 1)
    lo_half = lane < HEAD_DIM
    scale = HEAD_DIM ** -0.5
    n_att = A_HEADS * tq

    def put(h, t):
        qs_ref[h * tq:(h + 1) * tq, :] = (t * scale).astype(BF16)

    for j in range(A_HEADS // 2):
        t = qa_ref[0, :, j * LANES:(j + 1) * LANES].astype(F32)
        put(2 * j, jnp.where(lo_half, t, 0.0))
        put(2 * j + 1, jnp.where(lo_half, pltpu.roll(t, HEAD_DIM, 1), 0.0))
    for j in range(IDX_HEADS // 2):
        t = qi_ref[0, :, j * LANES:(j + 1) * LANES].astype(F32)
        put(A_HEADS + 2 * j, jnp.where(lo_half, 0.0, pltpu.roll(t, HEAD_DIM, 1)))
        put(A_HEADS + 2 * j + 1, jnp.where(lo_half, 0.0, t))

    w_idx = sc_ref[0, :, SC_IW:SC_IW + IDX_HEADS] * (IDX_HEADS ** -0.5)
    row_pos = i * tq + lax.broadcasted_iota(jnp.int32, (tq, ck), 0)
    col_iota = lax.broadcasted_iota(jnp.int32, (tq, ck), 1)

    def key_rows(c):
        return pl.ds(pl.multiple_of(c * ck, ck), ck)

    def score_chunk(c, _):
        rel = jnp.maximum(_dot_nt(qs_ref[n_att:, :], kk_ref[0, key_rows(c), :]), 0.0)
        score = rel[0:tq] * w_idx[:, 0:1]
        for h in range(1, IDX_HEADS):
            score = score + rel[h * tq:(h + 1) * tq] * w_idx[:, h:h + 1]
        score = jnp.where(score == 0.0, 0.0, score)
        score = jnp.where(col_iota + c * ck <= row_pos, score, -jnp.inf)
        bits = lax.bitcast_convert_type(score, jnp.int32)
        key_ref[c] = jnp.where(bits < 0, bits ^ 0x7FFFFFFF, bits)
        return 0

    lax.fori_loop(0, nck, score_chunk, 0)

    def count(pred):
        def body(c, acc):
            kc = key_ref[c]
            for j in range(ck // LANES):
                acc = acc + jnp.where(pred(kc[:, j * LANES:(j + 1) * LANES]), 1.0, 0.0)
            return acc
        acc = lax.fori_loop(0, nck, body, jnp.zeros((tq, LANES), F32))
        return jnp.sum(acc, axis=-1, keepdims=True)

    kf = float(top_k)

    def bit_step(b, thr):
        cand = thr + lax.shift_left(jnp.int32(1), 31 - b)
        cnt = count(lambda kc: kc >= cand)
        return jnp.where(cnt >= kf, cand, thr)

    thr = lax.fori_loop(0, 32, bit_step, jnp.full((tq, 1), INT_MIN, jnp.int32))
    thr = jnp.maximum(thr, KEY_NEG_INF + 1)
    cnt_ge = count(lambda kc: kc >= thr)

    def write_bias(c, _):
        bias_ref[c] = jnp.where(key_ref[c] >= thr, 0.0, NEG_MASK)
        return 0

    lax.fori_loop(0, nck, write_bias, 0)

    @pl.when(jnp.max(cnt_ge) > kf)
    def _():
        need = kf - count(lambda kc: kc > thr)
        r = lax.broadcasted_iota(jnp.int32, (ck, ck), 0)
        cidx = lax.broadcasted_iota(jnp.int32, (ck, ck), 1)
        upper = jnp.where(r <= cidx, 1.0, 0.0).astype(BF16)

        def tie_chunk(c, before):
            kc = key_ref[c]
            tie = kc == thr
            rank = before + _dot(jnp.where(tie, 1.0, 0.0).astype(BF16), upper)
            keep = (kc > thr) | (tie & (rank <= need))
            bias_ref[c] = jnp.where(keep, 0.0, NEG_MASK)
            return rank[:, ck - 1:ck]

        lax.fori_loop(0, nck, tie_chunk, jnp.zeros((tq, 1), F32))

    m_ref[...] = jnp.full(m_ref.shape, NEG_BIG, F32)
    l_ref[...] = jnp.zeros(l_ref.shape, F32)
    acc_ref[...] = jnp.zeros(acc_ref.shape, F32)

    def attend(c, _):
        s_all = _dot_nt(qs_ref[0:n_att, :], kk_ref[0, key_rows(c), :])
        bias = bias_ref[c]
        for h in range(A_HEADS):
            r = slice(h * tq, (h + 1) * tq)
            s = s_all[r] + bias
            m_old = m_ref[r]
            m_new = jnp.maximum(m_old, jnp.max(s, axis=-1, keepdims=True))
            alpha = jnp.exp(m_old - m_new)
            p = jnp.exp(s - jnp.tile(m_new, (1, ck // LANES)))
            l_ref[r] = alpha * l_ref[r] + jnp.sum(p, axis=-1, keepdims=True)
            m_ref[r] = m_new
            alpha_ref[r] = alpha
            p_ref[r] = p.astype(BF16)
        acc_ref[...] = alpha_ref[...] * acc_ref[...] + _dot(p_ref[...], vv_ref[0, key_rows(c), :])
        return 0

    lax.fori_loop(0, nck, attend, 0)

    out = acc_ref[...] / l_ref[...]
    for j in range(A_HEADS // 2):
        even = out[(2 * j) * tq:(2 * j + 1) * tq]
        odd = out[(2 * j + 1) * tq:(2 * j + 2) * tq]
        o_ref[0, :, j * LANES:(j + 1) * LANES] = (
            even + pltpu.roll(odd, HEAD_DIM, 1)).astype(o_ref.dtype)


def _dsa(a_grp, sc, top_k):
    b, s, _ = a_grp.shape
    tq = min(DSA_TQ, s)
    ck = min(DSA_CK, s)
    return pl.pallas_call(
        functools.partial(_dsa_kernel, top_k=top_k, ck=ck),
        grid=(b, s // tq),
        in_specs=[
            pl.BlockSpec((1, tq, 512), lambda bi, i: (bi, i, 0)),
            pl.BlockSpec((1, tq, 256), lambda bi, i: (bi, i, 2)),
            pl.BlockSpec((1, s, LANES), lambda bi, i: (bi, 0, 6)),
            pl.BlockSpec((1, s, LANES), lambda bi, i: (bi, 0, 7)),
            pl.BlockSpec((1, tq, LANES), lambda bi, i: (bi, i, 0)),
        ],
        out_specs=pl.BlockSpec((1, tq, BRANCH), lambda bi, i: (bi, i, 0)),
        out_shape=jax.ShapeDtypeStruct((b, s, BRANCH), BF16),
        scratch_shapes=[
            pltpu.VMEM((s // ck, tq, ck), jnp.int32),
            pltpu.VMEM((s // ck, tq, ck), F32),
            pltpu.VMEM(((A_HEADS + IDX_HEADS) * tq, LANES), BF16),
            pltpu.VMEM((A_HEADS * tq, LANES), F32),
            pltpu.VMEM((A_HEADS * tq, LANES), F32),
            pltpu.VMEM((A_HEADS * tq, LANES), F32),
            pltpu.VMEM((A_HEADS * tq, LANES), F32),
            pltpu.VMEM((A_HEADS * tq, ck), BF16),
        ],
        compiler_params=_cparams(("parallel", "arbitrary")),
        name="dsa_attention",
    )(a_grp, a_grp, a_grp, a_grp, sc)


CUM_CHUNK = 256


def _fox_gate_kernel(sc_ref, bf_ref, o_ref, *, chunk):
    s = sc_ref.shape[1]
    r = lax.broadcasted_iota(jnp.int32, (chunk, chunk), 0)
    c = lax.broadcasted_iota(jnp.int32, (chunk, chunk), 1)
    lower = jnp.where(c <= r, 1.0, 0.0)
    carry = jnp.zeros((1, LANES), F32)
    for n in range(s // chunk):
        x = sc_ref[0, n * chunk:(n + 1) * chunk, :] + bf_ref[...]
        log_f = jnp.minimum(x, 0.0) - jnp.log1p(jnp.exp(-jnp.abs(x)))
        cum = _dot_hi(lower, log_f) + carry
        carry = cum[chunk - 1:chunk, :]
        o_ref[0, :, n * chunk:(n + 1) * chunk] = -(cum.T[:B_HEADS, :])


def _fox_gate(sc, bf_row):
    b, s, _ = sc.shape
    chunk = min(CUM_CHUNK, s)
    return pl.pallas_call(
        functools.partial(_fox_gate_kernel, chunk=chunk),
        grid=(b,),
        in_specs=[pl.BlockSpec((1, s, LANES), lambda bi: (bi, 0, 0)), _resident((1, LANES))],
        out_specs=pl.BlockSpec((1, B_HEADS, s), lambda bi: (bi, 0, 0)),
        out_shape=jax.ShapeDtypeStruct((b, B_HEADS, s), F32),
        compiler_params=_cparams(("parallel",)),
        name="fox_gate_cumsum",
    )(sc, bf_row)


FOX_T = 256


def _fox_kernel(q_ref, k_ref, v_ref, nc_ref, o_ref,
                qz_ref, m_ref, l_ref, alpha_ref, acc_ref, p_ref, *, t):
    hp = pl.program_id(1)
    i = pl.program_id(2)
    lane = lax.broadcasted_iota(jnp.int32, (t, LANES), 1)
    lo_half = lane < HEAD_DIM
    q = q_ref[0].astype(F32) * (HEAD_DIM ** -0.5)
    qz_ref[0:t, :] = jnp.where(lo_half, q, 0.0).astype(BF16)
    qz_ref[t:2 * t, :] = jnp.where(lo_half, 0.0, q).astype(BF16)
    causal = (lax.broadcasted_iota(jnp.int32, (t, t), 1)
              <= lax.broadcasted_iota(jnp.int32, (t, t), 0))
    m_ref[...] = jnp.full(m_ref.shape, NEG_BIG, F32)
    l_ref[...] = jnp.zeros(l_ref.shape, F32)
    acc_ref[...] = jnp.zeros(acc_ref.shape, F32)

    def attend(c, _, masked):
        rows = pl.ds(pl.multiple_of(c * t, t), t)
        s_all = _dot_nt(qz_ref[...], k_ref[0, rows, :])
        for j in range(2):
            r = slice(j * t, (j + 1) * t)
            s = s_all[r] + nc_ref[0, 2 * hp + j, pl.ds(c, 1), :]
            if masked:
                s = jnp.where(causal, s, -jnp.inf)
            m_old = m_ref[r]
            m_new = jnp.maximum(m_old, jnp.max(s, axis=-1, keepdims=True))
            alpha = jnp.exp(m_old - m_new)
            p = jnp.exp(s - jnp.tile(m_new, (1, t // LANES)))
            l_ref[r] = alpha * l_ref[r] + jnp.sum(p, axis=-1, keepdims=True)
            m_ref[r] = m_new
            alpha_ref[r] = alpha
            p_ref[r] = p.astype(BF16)
        acc_ref[...] = alpha_ref[...] * acc_ref[...] + _dot(p_ref[...], v_ref[0, rows, :])
        return 0

    lax.fori_loop(0, i, functools.partial(attend, masked=False), 0)
    attend(i, 0, masked=True)
    out = acc_ref[...] / l_ref[...]
    o_ref[0] = jnp.where(lo_half, out[0:t], out[t:2 * t]).astype(o_ref.dtype)


def _fox(b_grp, neg_c):
    b, s, _ = b_grp.shape
    t = min(FOX_T, s)
    pairs = B_HEADS // 2
    neg_c = neg_c.reshape(b, B_HEADS, s // t, t)
    return pl.pallas_call(
        functools.partial(_fox_kernel, t=t),
        grid=(b, pairs, s // t),
        in_specs=[
            pl.BlockSpec((1, t, LANES), lambda bi, hp, i: (bi, i, hp)),
            pl.BlockSpec((1, s, LANES), lambda bi, hp, i: (bi, 0, pairs + hp)),
            pl.BlockSpec((1, s, LANES), lambda bi, hp, i: (bi, 0, 2 * pairs + hp)),
            pl.BlockSpec((1, B_HEADS, s // t, t), lambda bi, hp, i: (bi, 0, 0, 0)),
        ],
        out_specs=pl.BlockSpec((1, t, LANES), lambda bi, hp, i: (bi, i, hp)),
        out_shape=jax.ShapeDtypeStruct((b, s, BRANCH), BF16),
        scratch_shapes=[
            pltpu.VMEM((2 * t, LANES), BF16),
            pltpu.VMEM((2 * t, LANES), F32),
            pltpu.VMEM((2 * t, LANES), F32),
            pltpu.VMEM((2 * t, LANES), F32),
            pltpu.VMEM((2 * t, LANES), F32),
            pltpu.VMEM((2 * t, t), BF16),
        ],
        compiler_params=_cparams(("parallel", "parallel", "arbitrary")),
        name="fox_attention",
    )(b_grp, b_grp, b_grp, neg_c)


GDN_TS = 256
GDN_C = 128
TAIL = 8


def _neumann_inverse(a):
    n = a.shape[0]
    eye = jnp.where(lax.broadcasted_iota(jnp.int32, (n, n), 0)
                    == lax.broadcasted_iota(jnp.int32, (n, n), 1), 1.0, 0.0)
    inv = eye - a
    p = a
    steps = int(np.log2(n)) - 1
    for _ in range(steps):
        p = _dot_hi(p, p)
        inv = inv + _dot_hi(inv, p)
    return inv


def _gdn_kernel(x_ref, z_ref, sc_ref, cw_ref, alog_ref, dtb_ref, dn_ref, o_ref,
                state_ref, tail_ref, ext_ref, *, ts):
    j = pl.program_id(1)
    width = x_ref.shape[2]

    @pl.when(j == 0)
    def _():
        state_ref[...] = jnp.zeros_like(state_ref)
        tail_ref[...] = jnp.zeros_like(tail_ref)

    x = x_ref[0]
    ext_ref[0:TAIL, :] = tail_ref[...]
    ext_ref[TAIL:TAIL + ts, :] = x
    tail_ref[...] = x[ts - TAIL:ts, :]
    conv = jnp.zeros((ts, width), F32)
    for tap in range(CONV_WIDTH):
        start = TAIL - (CONV_WIDTH - 1) + tap
        conv = conv + cw_ref[tap:tap + 1, :] * ext_ref[start:start + ts, :]
    act = _silu(conv)

    sc = sc_ref[0]
    beta_all = jax.nn.sigmoid(sc)
    xg = sc + dtb_ref[...]
    softplus = jnp.maximum(xg, 0.0) + jnp.log1p(jnp.exp(-jnp.abs(xg)))
    g_all = -jnp.exp(alog_ref[...]) * softplus
    r = lax.broadcasted_iota(jnp.int32, (ts, ts), 0)
    c = lax.broadcasted_iota(jnp.int32, (ts, ts), 1)
    same_chunk_lower = jnp.where((c <= r) & (r // GDN_C == c // GDN_C), 1.0, 0.0)
    gc_all = _dot_hi(same_chunk_lower, g_all)
    gc_t = gc_all.T

    ri = lax.broadcasted_iota(jnp.int32, (GDN_C, GDN_C), 0)
    ci = lax.broadcasted_iota(jnp.int32, (GDN_C, GDN_C), 1)
    incl = ri >= ci
    strict = ri > ci

    def l2n(t):
        return t * lax.rsqrt(jnp.sum(t * t, axis=-1, keepdims=True) + L2_EPS)

    for h in range(C_HEADS):
        for n in range(ts // GDN_C):
            r0 = n * GDN_C
            rows = slice(r0, r0 + GDN_C)
            q = l2n(act[rows, h * C_DIM:(h + 1) * C_DIM]) * (C_DIM ** -0.5)
            k = l2n(act[rows, BRANCH + h * C_DIM:BRANCH + (h + 1) * C_DIM])
            v = act[rows, 2 * BRANCH + h * C_DIM:2 * BRANCH + (h + 1) * C_DIM]
            beta = beta_all[rows, SC_BETA + h:SC_BETA + h + 1]
            gcol = gc_all[rows, SC_DECAY + h:SC_DECAY + h + 1]
            grow = gc_t[SC_DECAY + h:SC_DECAY + h + 1, r0:r0 + GDN_C]
            decay = jnp.exp(jnp.where(incl, gcol - grow, -jnp.inf))
            kb = k * beta
            k16 = k.astype(BF16)
            a_mat = jnp.where(strict, _dot_nt(kb.astype(BF16), k16) * decay, 0.0)
            t_inv = _neumann_inverse(a_mat)
            u = _dot_hi(t_inv, v * beta)
            w = _dot_hi(t_inv, kb * jnp.exp(gcol))
            qk = jnp.where(incl, _dot_nt(q.astype(BF16), k16) * decay, 0.0)

            state = state_ref[h]
            s16 = state.astype(BF16)
            v_new = u - _dot(w.astype(BF16), s16)
            vn16 = v_new.astype(BF16)
            o = _dot((q * jnp.exp(gcol)).astype(BF16), s16) + _dot(qk.astype(BF16), vn16)
            g_last = gcol[GDN_C - 1:GDN_C, :]
            kd = k * jnp.exp(g_last - gcol)
            state_ref[h] = state * jnp.exp(g_last) + _dot(kd.T.astype(BF16), vn16)

            z = z_ref[0, rows, h * C_DIM:(h + 1) * C_DIM]
            y = _rms(o, dn_ref[...]) * _silu(z)
            o_ref[0, rows, h * C_DIM:(h + 1) * C_DIM] = y.astype(o_ref.dtype)


def _gdn(cqkv, cz, sc, conv_w, alog_row, dtb_row, dn_row):
    b, s, width = cqkv.shape
    ts = min(GDN_TS, s)
    blk = lambda bi, j: (bi, j, 0)
    return pl.pallas_call(
        functools.partial(_gdn_kernel, ts=ts),
        grid=(b, s // ts),
        in_specs=[
            pl.BlockSpec((1, ts, width), blk),
            pl.BlockSpec((1, ts, BRANCH), blk),
            pl.BlockSpec((1, ts, LANES), blk),
            _resident((CONV_WIDTH, width)),
            _resident((1, LANES)),
            _resident((1, LANES)),
            _resident((1, C_DIM)),
        ],
        out_specs=pl.BlockSpec((1, ts, BRANCH), blk),
        out_shape=jax.ShapeDtypeStruct((b, s, BRANCH), BF16),
        scratch_shapes=[
            pltpu.VMEM((C_HEADS, C_DIM, C_DIM), F32),
            pltpu.VMEM((TAIL, width), F32),
            pltpu.VMEM((TAIL + ts, width), F32),
        ],
        compiler_params=_cparams(("parallel", "arbitrary")),
        name="gated_deltanet",
    )(cqkv, cz, sc, conv_w, alog_row, dtb_row, dn_row)


MERGE_TM = 512


def _merge_kernel(x_ref, ya_ref, yb_ref, yc_ref, gl_ref, bg_ref,
                  wa_ref, wb_ref, wc_ref, wo_ref, o_ref):
    merged = None
    for n, (y_ref, w_ref) in enumerate(((ya_ref, wa_ref), (yb_ref, wb_ref), (yc_ref, wc_ref))):
        cols = slice(n * D_MODEL, (n + 1) * D_MODEL)
        gate = jax.nn.sigmoid(gl_ref[:, cols] + bg_ref[:, cols])
        term = gate * _dot(y_ref[...], w_ref[...])
        merged = term if merged is None else merged + term
    o_ref[...] = x_ref[...] + _dot(merged.astype(BF16), wo_ref[...])


def _merge(x, ya, yb, yc, gl, b_gate, wa, wb, wc, wo):
    n = x.shape[0]
    tm = min(MERGE_TM, n)
    row = lambda i: (i, 0)
    return pl.pallas_call(
        _merge_kernel,
        grid=(n // tm,),
        in_specs=[
            pl.BlockSpec((tm, D_MODEL), row),
            pl.BlockSpec((tm, BRANCH), row),
            pl.BlockSpec((tm, BRANCH), row),
            pl.BlockSpec((tm, BRANCH), row),
            pl.BlockSpec((tm, 3 * D_MODEL), row),
            _resident((1, 3 * D_MODEL)),
            _resident((BRANCH, D_MODEL)),
            _resident((BRANCH, D_MODEL)),
            _resident((BRANCH, D_MODEL)),
            _resident((D_MODEL, D_MODEL)),
        ],
        out_specs=pl.BlockSpec((tm, D_MODEL), row),
        out_shape=jax.ShapeDtypeStruct((n, D_MODEL), F32),
        compiler_params=_cparams(("parallel",)),
        name="gated_merge",
    )(x, ya, yb, yc, gl, b_gate, wa, wb, wc, wo)


def _layout_w_in(w_in):
    offs = np.concatenate([[0], np.cumsum(IN_SIZES)])
    seg = lambda n: w_in[:, offs[n]:offs[n + 1]]
    (a_q, a_k, a_v, i_q, i_k, i_w, b_qkv, b_f, c_qkv, c_z, c_beta, c_a, gates) = (
        seg(n) for n in range(len(IN_SIZES)))
    zeros = lambda n: jnp.zeros((w_in.shape[0], n), w_in.dtype)
    a_grp = jnp.concatenate([a_q, i_q, a_k, i_k, a_v, zeros(HEAD_DIM)], axis=1)
    sc_grp = jnp.concatenate([b_f, i_w, c_beta, c_a, zeros(W_SC - 20)], axis=1)
    return jnp.concatenate([a_grp, sc_grp, b_qkv, c_qkv, c_z, gates], axis=1).astype(BF16)


def _rotary_tables(seq):
    pos = jnp.arange(seq, dtype=F32)
    inv_freq = jnp.power(ROPE_THETA, -jnp.arange(0, ROT_DIM, 2, dtype=F32) / ROT_DIM)
    ang = pos[:, None] * inv_freq[None, :]
    cos, sin = jnp.cos(ang), jnp.sin(ang)
    half = ROT_DIM // 2
    ones = jnp.ones((seq, HEAD_DIM - ROT_DIM), F32)
    zeros_h = jnp.zeros((seq, half), F32)
    zeros_r = jnp.zeros((seq, HEAD_DIM - ROT_DIM), F32)
    c64 = jnp.concatenate([cos, cos, ones], axis=1)
    s1_64 = jnp.concatenate([zeros_h, sin, zeros_r], axis=1)
    s2_64 = jnp.concatenate([-sin, zeros_h, zeros_r], axis=1)
    twice = lambda t: jnp.concatenate([t, t], axis=1)
    return twice(c64), twice(s1_64), twice(s2_64)


def _lane_row(values, start):
    return jnp.zeros((1, LANES), F32).at[0, start:start + values.shape[0]].set(values.astype(F32))


def kernel(x, ffn1_norm, ffn1_w_in, ffn1_w_out, mix_norm, w_in, b_gate, b_forget, conv_w, a_log, dt_bias, delta_norm, w_branch_a, w_branch_b, w_branch_c, w_out, ffn2_norm, ffn2_w_in, ffn2_w_out, final_norm):
    batch, seq, _ = x.shape
    depth = w_in.shape[0]
    top_k = min(INDEX_TOPK, seq // 4)
    cos, s1, s2 = _rotary_tables(seq)
    final_row = final_norm.reshape(1, D_MODEL)
    xt = x.reshape(batch * seq, D_MODEL)
    for l in range(depth):
        xt = _ffn(xt, ffn1_norm[l].reshape(1, D_MODEL), ffn1_w_in[l].astype(BF16),
                  ffn1_w_out[l].astype(BF16), final_row, False)
        a_grp, sc, b_grp, cqkv, cz, gl = _proj(
            xt, mix_norm[l].reshape(1, D_MODEL), _layout_w_in(w_in[l]), cos, s1, s2, seq)
        a_grp = a_grp.reshape(batch, seq, W_A)
        sc = sc.reshape(batch, seq, W_SC)
        y_a = _dsa(a_grp, sc, top_k)
        neg_c = _fox_gate(sc, _lane_row(b_forget[l], SC_BF))
        y_b = _fox(b_grp.reshape(batch, seq, W_B), neg_c)
        y_c = _gdn(cqkv.reshape(batch, seq, W_CQKV), cz.reshape(batch, seq, W_CZ), sc,
                   conv_w[l], _lane_row(a_log[l], SC_DECAY), _lane_row(dt_bias[l], SC_DECAY),
                   delta_norm[l].reshape(1, C_DIM))
        flat = lambda t: t.reshape(batch * seq, BRANCH)
        xt = _merge(xt, flat(y_a), flat(y_b), flat(y_c), gl, b_gate[l].reshape(1, 3 * D_MODEL),
                    w_branch_a[l].astype(BF16), w_branch_b[l].astype(BF16),
                    w_branch_c[l].astype(BF16), w_out[l].astype(BF16))
        xt = _ffn(xt, ffn2_norm[l].reshape(1, D_MODEL), ffn2_w_in[l].astype(BF16),
                  ffn2_w_out[l].astype(BF16), final_row, l == depth - 1)
    return xt.reshape(batch, seq, D_MODEL)
```

```python
import functools

import numpy as np
import jax
import jax.numpy as jnp
from jax import lax
from jax.experimental import pallas as pl
from jax.experimental.pallas import tpu as pltpu

F32 = jnp.float32
BF16 = jnp.bfloat16

D_MODEL = 1024
BRANCH = 512
A_HEADS = 8
HEAD_DIM = 64
IDX_HEADS = 4
INDEX_TOPK = 256
B_HEADS = 8
C_HEADS = 4
C_DIM = 128
CONV_WIDTH = 4
ROPE_THETA = 500000.0
ROT_DIM = 16
FFN_DIM = 2048
NORM_EPS = 1e-6
L2_EPS = 1e-6
IN_SIZES = (512, 64, 64, 256, 64, 4, 1536, 8, 1536, 512, 4, 4, 3072)

LANES = 128
SC_BF = 0
SC_IW = 8
SC_BETA = 12
SC_DECAY = 16
W_A, W_SC, W_B, W_CQKV, W_CZ, W_G = 1024, 128, 1536, 1536, 512, 3072
W_TOTAL = W_A + W_SC + W_B + W_CQKV + W_CZ + W_G
ROT_TILES = 7

NEG_BIG = -1e30
NEG_MASK = -2e30
VMEM_LIMIT = 56 * 1024 * 1024


def _cparams(sem):
    return pltpu.CompilerParams(dimension_semantics=sem, vmem_limit_bytes=VMEM_LIMIT)


def _dot(a, b):
    return jnp.dot(a, b, preferred_element_type=F32)


def _dot_hi(a, b):
    return jnp.dot(a, b, preferred_element_type=F32, precision=lax.Precision.HIGHEST)


def _dot_solve(a, b):
    a_hi = a.astype(BF16)
    b_hi = b.astype(BF16)
    a_lo = (a - a_hi.astype(F32)).astype(BF16)
    b_lo = (b - b_hi.astype(F32)).astype(BF16)
    m = a.shape[0]
    top = _dot(jnp.concatenate([a_hi, a_lo], axis=0), b_hi)
    return top[0:m] + top[m:2 * m] + _dot(a_hi, b_lo)


def _dot_nt(a, b):
    return lax.dot_general(a, b, (((1,), (1,)), ((), ())), preferred_element_type=F32)


def _rms(x, gain):
    return x * lax.rsqrt(jnp.mean(x * x, axis=-1, keepdims=True) + NORM_EPS) * gain


def _silu(x):
    return x * jax.nn.sigmoid(x)


def _resident(shape):
    nd = len(shape)
    return pl.BlockSpec(shape, lambda *_: (0,) * nd, pipeline_mode=pl.Buffered(1))


FFN_TM = 512
FFN_CHUNK = 512


def _ffn_kernel(x_ref, g_ref, win_ref, wout_ref, fg_ref, o_ref, *, final):
    x = x_ref[...]
    h = _rms(x, g_ref[...]).astype(BF16)
    acc = jnp.zeros(x.shape, F32)
    for c in range(FFN_DIM // FFN_CHUNK):
        lo = c * FFN_CHUNK
        gate = _dot(h, win_ref[:, lo:lo + FFN_CHUNK])
        up = _dot(h, win_ref[:, FFN_DIM + lo:FFN_DIM + lo + FFN_CHUNK])
        act = (_silu(gate) * up).astype(BF16)
        acc = acc + _dot(act, wout_ref[lo:lo + FFN_CHUNK, :])
    y = x + 0.5 * acc
    if final:
        y = _rms(y, fg_ref[...])
    o_ref[...] = y


def _ffn(x, gain, w_in, w_out, final_gain, final):
    n = x.shape[0]
    tm = min(FFN_TM, n)
    return pl.pallas_call(
        functools.partial(_ffn_kernel, final=final),
        grid=(n // tm,),
        in_specs=[
            pl.BlockSpec((tm, D_MODEL), lambda i: (i, 0)),
            _resident((1, D_MODEL)),
            _resident((D_MODEL, 2 * FFN_DIM)),
            _resident((FFN_DIM, D_MODEL)),
            _resident((1, D_MODEL)),
        ],
        out_specs=pl.BlockSpec((tm, D_MODEL), lambda i: (i, 0)),
        out_shape=jax.ShapeDtypeStruct((n, D_MODEL), F32),
        compiler_params=_cparams(("parallel",)),
        name="ffn_half",
    )(x, gain, w_in, w_out, final_gain)


PROJ_TM = 256
PROJ_CHUNK = 512


def _proj_kernel(x_ref, g_ref, w_ref, cos_ref, s1_ref, s2_ref,
                 a_ref, sc_ref, b_ref, cqkv_ref, cz_ref, gl_ref):
    h = _rms(x_ref[...], g_ref[...]).astype(BF16)
    cos, s1, s2 = cos_ref[...], s1_ref[...], s2_ref[...]

    off = 0
    for j in range(W_A // LANES):
        t = _dot(h, w_ref[:, off:off + LANES])
        if j < ROT_TILES:
            t = t * cos + pltpu.roll(t, 8, 1) * s1 + pltpu.roll(t, LANES - 8, 1) * s2
        a_ref[:, off:off + LANES] = t.astype(BF16)
        off += LANES
    sc_ref[...] = _dot(h, w_ref[:, off:off + W_SC])
    off += W_SC
    for ref, width in ((b_ref, W_B), (cqkv_ref, W_CQKV), (cz_ref, W_CZ), (gl_ref, W_G)):
        for lo in range(0, width, PROJ_CHUNK):
            ref[:, lo:lo + PROJ_CHUNK] = _dot(
                h, w_ref[:, off + lo:off + lo + PROJ_CHUNK]).astype(ref.dtype)
        off += width


def _proj(x, gain, w, cos, s1, s2, seq):
    n = x.shape[0]
    tm = min(PROJ_TM, seq)
    per_seq = seq // tm
    row = lambda i: (i, 0)
    tab = lambda i: (i % per_seq, 0)
    widths = (W_A, W_SC, W_B, W_CQKV, W_CZ, W_G)
    dtypes = (BF16, F32, BF16, F32, F32, F32)
    return pl.pallas_call(
        _proj_kernel,
        grid=(n // tm,),
        in_specs=[
            pl.BlockSpec((tm, D_MODEL), row),
            _resident((1, D_MODEL)),
            _resident((D_MODEL, W_TOTAL)),
            pl.BlockSpec((tm, LANES), tab),
            pl.BlockSpec((tm, LANES), tab),
            pl.BlockSpec((tm, LANES), tab),
        ],
        out_specs=[pl.BlockSpec((tm, wd), row) for wd in widths],
        out_shape=[jax.ShapeDtypeStruct((n, wd), dt) for wd, dt in zip(widths, dtypes)],
        compiler_params=_cparams(("parallel",)),
        name="mixer_proj",
    )(x, gain, w, cos, s1, s2)


DSA_TQ = 256
DSA_CK = 512
SEL_ROWS = 128
INT_MIN = -2 ** 31
KEY_NEG_INF = int(np.array(-np.inf, np.float32).view(np.int32)) ^ 0x7FFFFFFF


def _dsa_kernel(qa_ref, qi_ref, kk_ref, vv_ref, sc_ref, o_ref,
                key_ref, bias_ref, qs_ref, m_ref, l_ref, alpha_ref, acc_ref, p_ref, *, top_k, ck):
    i = pl.program_id(1)
    tq = qa_ref.shape[1]
    nck = (i * tq + tq + ck - 1) // ck
    lane = lax.broadcasted_iota(jnp.int32, (tq, LANES), 1)
    lo_half = lane < HEAD_DIM
    scale = HEAD_DIM ** -0.5
    n_att = A_HEADS * tq

    def put(h, t):
        qs_ref[h * tq:(h + 1) * tq, :] = (t * scale).astype(BF16)

    for j in range(A_HEADS // 2):
        t = qa_ref[0, :, j * LANES:(j + 1) * LANES].astype(F32)
        put(2 * j, jnp.where(lo_half, t, 0.0))
        put(2 * j + 1, jnp.where(lo_half, pltpu.roll(t, HEAD_DIM, 1), 0.0))
    for j in range(IDX_HEADS // 2):
        t = qi_ref[0, :, j * LANES:(j + 1) * LANES].astype(F32)
        put(A_HEADS + 2 * j, jnp.where(lo_half, 0.0, pltpu.roll(t, HEAD_DIM, 1)))
        put(A_HEADS + 2 * j + 1, jnp.where(lo_half, 0.0, t))

    w_idx = sc_ref[0, :, SC_IW:SC_IW + IDX_HEADS] * (IDX_HEADS ** -0.5)
    row_pos = i * tq + lax.broadcasted_iota(jnp.int32, (tq, ck), 0)
    col_iota = lax.broadcasted_iota(jnp.int32, (tq, ck), 1)

    def key_rows(c):
        return pl.ds(pl.multiple_of(c * ck, ck), ck)

    def score_chunk(c, _):
        rel = jnp.maximum(_dot_nt(qs_ref[n_att:, :], kk_ref[0, key_rows(c), :]), 0.0)
        score = rel[0:tq] * w_idx[:, 0:1]
        for h in range(1, IDX_HEADS):
            score = score + rel[h * tq:(h + 1) * tq] * w_idx[:, h:h + 1]
        score = jnp.where(score == 0.0, 0.0, score)
        score = jnp.where(col_iota + c * ck <= row_pos, score, -jnp.inf)
        bits = lax.bitcast_convert_type(score, jnp.int32)
        key_ref[c] = jnp.where(bits < 0, bits ^ 0x7FFFFFFF, bits)
        return 0

    lax.fori_loop(0, nck, score_chunk, 0)

    groups = tq // SEL_ROWS

    def count(preds):
        accs = []
        for g, pred in enumerate(preds):
            def body(c, acc, g=g, pred=pred):
                kc = key_ref[c, g * SEL_ROWS:(g + 1) * SEL_ROWS, :]
                for j in range(ck // LANES):
                    acc = acc + jnp.where(pred(kc[:, j * LANES:(j + 1) * LANES]), 1.0, 0.0)
                return acc
            accs.append(lax.fori_loop(0, nck, body, jnp.zeros((SEL_ROWS, LANES), F32)))
        return [jnp.sum(acc, axis=-1, keepdims=True) for acc in accs]

    kf = float(top_k)

    def bit_step(b, thrs):
        cands = [t + lax.shift_left(jnp.int32(1), 31 - b) for t in thrs]
        cnts = count([lambda kc, cand=cand: kc >= cand for cand in cands])
        return tuple(jnp.where(cnt >= kf, cand, t) for cnt, cand, t in zip(cnts, cands, thrs))

    thrs = lax.fori_loop(0, 32, bit_step,
                         tuple(jnp.full((SEL_ROWS, 1), INT_MIN, jnp.int32) for _ in range(groups)))
    thrs = [jnp.maximum(t, KEY_NEG_INF + 1) for t in thrs]
    cnt_ge = count([lambda kc, t=t: kc >= t for t in thrs])
    thr = jnp.concatenate(thrs, axis=0) if groups > 1 else thrs[0]

    def write_bias(c, _):
        bias_ref[c] = jnp.where(key_ref[c] >= thr, 0.0, NEG_MASK)
        return 0

    lax.fori_loop(0, nck, write_bias, 0)

    excess = cnt_ge[0]
    for extra in cnt_ge[1:]:
        excess = jnp.maximum(excess, extra)

    @pl.when(jnp.max(excess) > kf)
    def _():
        cnt_gt = count([lambda kc, t=t: kc > t for t in thrs])
        need = kf - (jnp.concatenate(cnt_gt, axis=0) if groups > 1 else cnt_gt[0])
        r = lax.broadcasted_iota(jnp.int32, (ck, ck), 0)
        cidx = lax.broadcasted_iota(jnp.int32, (ck, ck), 1)
        upper = jnp.where(r <= cidx, 1.0, 0.0).astype(BF16)

        def tie_chunk(c, before):
            kc = key_ref[c]
            tie = kc == thr
            rank = before + _dot(jnp.where(tie, 1.0, 0.0).astype(BF16), upper)
            keep = (kc > thr) | (tie & (rank <= need))
            bias_ref[c] = jnp.where(keep, 0.0, NEG_MASK)
            return rank[:, ck - 1:ck]

        lax.fori_loop(0, nck, tie_chunk, jnp.zeros((tq, 1), F32))

    m_ref[...] = jnp.full(m_ref.shape, NEG_BIG, F32)
    l_ref[...] = jnp.zeros(l_ref.shape, F32)
    acc_ref[...] = jnp.zeros(acc_ref.shape, F32)

    def attend(c, _):
        s_all = _dot_nt(qs_ref[0:n_att, :], kk_ref[0, key_rows(c), :])
        bias = bias_ref[c]
        for h in range(A_HEADS):
            r = slice(h * tq, (h + 1) * tq)
            s = s_all[r] + bias
            m_old = m_ref[r]
            m_new = jnp.maximum(m_old, jnp.max(s, axis=-1, keepdims=True))
            alpha = jnp.exp(m_old - m_new)
            p = jnp.exp(s - jnp.tile(m_new, (1, ck // LANES)))
            l_ref[r] = alpha * l_ref[r] + jnp.sum(p, axis=-1, keepdims=True)
            m_ref[r] = m_new
            alpha_ref[r] = alpha
            p_ref[r] = p.astype(BF16)
        acc_ref[...] = alpha_ref[...] * acc_ref[...] + _dot(p_ref[...], vv_ref[0, key_rows(c), :])
        return 0

    lax.fori_loop(0, nck, attend, 0)

    out = acc_ref[...] / l_ref[...]
    for j in range(A_HEADS // 2):
        even = out[(2 * j) * tq:(2 * j + 1) * tq]
        odd = out[(2 * j + 1) * tq:(2 * j + 2) * tq]
        o_ref[0, :, j * LANES:(j + 1) * LANES] = (
            even + pltpu.roll(odd, HEAD_DIM, 1)).astype(o_ref.dtype)


def _dsa(a_grp, sc, top_k):
    b, s, _ = a_grp.shape
    tq = min(DSA_TQ, s)
    ck = min(DSA_CK, s)
    return pl.pallas_call(
        functools.partial(_dsa_kernel, top_k=top_k, ck=ck),
        grid=(b, s // tq),
        in_specs=[
            pl.BlockSpec((1, tq, 512), lambda bi, i: (bi, i, 0)),
            pl.BlockSpec((1, tq, 256), lambda bi, i: (bi, i, 2)),
            pl.BlockSpec((1, s, LANES), lambda bi, i: (bi, 0, 6)),
            pl.BlockSpec((1, s, LANES), lambda bi, i: (bi, 0, 7)),
            pl.BlockSpec((1, tq, LANES), lambda bi, i: (bi, i, 0)),
        ],
        out_specs=pl.BlockSpec((1, tq, BRANCH), lambda bi, i: (bi, i, 0)),
        out_shape=jax.ShapeDtypeStruct((b, s, BRANCH), BF16),
        scratch_shapes=[
            pltpu.VMEM((s // ck, tq, ck), jnp.int32),
            pltpu.VMEM((s // ck, tq, ck), F32),
            pltpu.VMEM(((A_HEADS + IDX_HEADS) * tq, LANES), BF16),
            pltpu.VMEM((A_HEADS * tq, LANES), F32),
            pltpu.VMEM((A_HEADS * tq, LANES), F32),
            pltpu.VMEM((A_HEADS * tq, LANES), F32),
            pltpu.VMEM((A_HEADS * tq, LANES), F32),
            pltpu.VMEM((A_HEADS * tq, ck), BF16),
        ],
        compiler_params=_cparams(("parallel", "arbitrary")),
        name="dsa_attention",
    )(a_grp, a_grp, a_grp, a_grp, sc)


CUM_CHUNK = 256


def _fox_gate_kernel(sc_ref, bf_ref, o_ref, *, chunk):
    s = sc_ref.shape[1]
    r = lax.broadcasted_iota(jnp.int32, (chunk, chunk), 0)
    c = lax.broadcasted_iota(jnp.int32, (chunk, chunk), 1)
    lower = jnp.where(c <= r, 1.0, 0.0)
    carry = jnp.zeros((1, LANES), F32)
    for n in range(s // chunk):
        x = sc_ref[0, n * chunk:(n + 1) * chunk, :] + bf_ref[...]
        log_f = jnp.minimum(x, 0.0) - jnp.log1p(jnp.exp(-jnp.abs(x)))
        cum = _dot_hi(lower, log_f) + carry
        carry = cum[chunk - 1:chunk, :]
        o_ref[0, :, n * chunk:(n + 1) * chunk] = -(cum.T[:B_HEADS, :])


def _fox_gate(sc, bf_row):
    b, s, _ = sc.shape
    chunk = min(CUM_CHUNK, s)
    return pl.pallas_call(
        functools.partial(_fox_gate_kernel, chunk=chunk),
        grid=(b,),
        in_specs=[pl.BlockSpec((1, s, LANES), lambda bi: (bi, 0, 0)), _resident((1, LANES))],
        out_specs=pl.BlockSpec((1, B_HEADS, s), lambda bi: (bi, 0, 0)),
        out_shape=jax.ShapeDtypeStruct((b, B_HEADS, s), F32),
        compiler_params=_cparams(("parallel",)),
        name="fox_gate_cumsum",
    )(sc, bf_row)


FOX_T = 256


def _fox_kernel(q_ref, k_ref, v_ref, nc_ref, o_ref,
                qz_ref, m_ref, l_ref, alpha_ref, acc_ref, p_ref, *, t):
    hp = pl.program_id(1)
    i = pl.program_id(2)
    lane = lax.broadcasted_iota(jnp.int32, (t, LANES), 1)
    lo_half = lane < HEAD_DIM
    q = q_ref[0].astype(F32) * (HEAD_DIM ** -0.5)
    qz_ref[0:t, :] = jnp.where(lo_half, q, 0.0).astype(BF16)
    qz_ref[t:2 * t, :] = jnp.where(lo_half, 0.0, q).astype(BF16)
    causal = (lax.broadcasted_iota(jnp.int32, (t, t), 1)
              <= lax.broadcasted_iota(jnp.int32, (t, t), 0))
    m_ref[...] = jnp.full(m_ref.shape, NEG_BIG, F32)
    l_ref[...] = jnp.zeros(l_ref.shape, F32)
    acc_ref[...] = jnp.zeros(acc_ref.shape, F32)

    def attend(c, _, masked):
        rows = pl.ds(pl.multiple_of(c * t, t), t)
        s_all = _dot_nt(qz_ref[...], k_ref[0, rows, :])
        for j in range(2):
            r = slice(j * t, (j + 1) * t)
            s = s_all[r] + nc_ref[0, 2 * hp + j, pl.ds(c, 1), :]
            if masked:
                s = jnp.where(causal, s, -jnp.inf)
            m_old = m_ref[r]
            m_new = jnp.maximum(m_old, jnp.max(s, axis=-1, keepdims=True))
            alpha = jnp.exp(m_old - m_new)
            p = jnp.exp(s - jnp.tile(m_new, (1, t // LANES)))
            l_ref[r] = alpha * l_ref[r] + jnp.sum(p, axis=-1, keepdims=True)
            m_ref[r] = m_new
            alpha_ref[r] = alpha
            p_ref[r] = p.astype(BF16)
        acc_ref[...] = alpha_ref[...] * acc_ref[...] + _dot(p_ref[...], v_ref[0, rows, :])
        return 0

    lax.fori_loop(0, i, functools.partial(attend, masked=False), 0)
    attend(i, 0, masked=True)
    out = acc_ref[...] / l_ref[...]
    o_ref[0] = jnp.where(lo_half, out[0:t], out[t:2 * t]).astype(o_ref.dtype)


def _fox(b_grp, neg_c):
    b, s, _ = b_grp.shape
    t = min(FOX_T, s)
    pairs = B_HEADS // 2
    neg_c = neg_c.reshape(b, B_HEADS, s // t, t)
    return pl.pallas_call(
        functools.partial(_fox_kernel, t=t),
        grid=(b, pairs, s // t),
        in_specs=[
            pl.BlockSpec((1, t, LANES), lambda bi, hp, i: (bi, i, hp)),
            pl.BlockSpec((1, s, LANES), lambda bi, hp, i: (bi, 0, pairs + hp)),
            pl.BlockSpec((1, s, LANES), lambda bi, hp, i: (bi, 0, 2 * pairs + hp)),
            pl.BlockSpec((1, B_HEADS, s // t, t), lambda bi, hp, i: (bi, 0, 0, 0)),
        ],
        out_specs=pl.BlockSpec((1, t, LANES), lambda bi, hp, i: (bi, i, hp)),
        out_shape=jax.ShapeDtypeStruct((b, s, BRANCH), BF16),
        scratch_shapes=[
            pltpu.VMEM((2 * t, LANES), BF16),
            pltpu.VMEM((2 * t, LANES), F32),
            pltpu.VMEM((2 * t, LANES), F32),
            pltpu.VMEM((2 * t, LANES), F32),
            pltpu.VMEM((2 * t, LANES), F32),
            pltpu.VMEM((2 * t, t), BF16),
        ],
        compiler_params=_cparams(("parallel", "parallel", "arbitrary")),
        name="fox_attention",
    )(b_grp, b_grp, b_grp, neg_c)


GDN_TS = 512
GDN_C = 128
TAIL = 8
NEUMANN_ROUNDS = 6


def _gdn_kernel(x_ref, z_ref, sc_ref, cw_ref, alog_ref, dtb_ref, dn_ref, o_ref,
                state_ref, tail_ref, ext_ref, act_ref, np_ref, rhs_ref,
                qk_ref, qg_ref, kdt_ref, dec_ref, *, ts):
    j = pl.program_id(1)
    width = x_ref.shape[2]
    n_chunks = ts // GDN_C
    chains = [(h, n) for n in range(n_chunks) for h in range(C_HEADS)]

    @pl.when(j == 0)
    def _():
        state_ref[...] = jnp.zeros_like(state_ref)
        tail_ref[...] = jnp.zeros_like(tail_ref)

    x = x_ref[0]
    ext_ref[0:TAIL, :] = tail_ref[...]
    ext_ref[TAIL:TAIL + ts, :] = x
    tail_ref[...] = x[ts - TAIL:ts, :]
    conv = jnp.zeros((ts, width), F32)
    for tap in range(CONV_WIDTH):
        start = TAIL - (CONV_WIDTH - 1) + tap
        conv = conv + cw_ref[tap:tap + 1, :] * ext_ref[start:start + ts, :]
    act_ref[...] = _silu(conv)

    sc = sc_ref[0]
    beta_all = jax.nn.sigmoid(sc)
    xg = sc + dtb_ref[...]
    softplus = jnp.maximum(xg, 0.0) + jnp.log1p(jnp.exp(-jnp.abs(xg)))
    g_all = -jnp.exp(alog_ref[...]) * softplus
    r = lax.broadcasted_iota(jnp.int32, (ts, ts), 0)
    c = lax.broadcasted_iota(jnp.int32, (ts, ts), 1)
    same_chunk_lower = jnp.where((c <= r) & (r // GDN_C == c // GDN_C), 1.0, 0.0)
    gc_all = _dot_hi(same_chunk_lower, g_all)
    gc_t = gc_all.T

    ri = lax.broadcasted_iota(jnp.int32, (GDN_C, GDN_C), 0)
    ci = lax.broadcasted_iota(jnp.int32, (GDN_C, GDN_C), 1)
    incl = ri >= ci
    strict = ri > ci

    def l2n(t):
        return t * lax.rsqrt(jnp.sum(t * t, axis=-1, keepdims=True) + L2_EPS)

    for idx, (h, n) in enumerate(chains):
        rows = slice(n * GDN_C, (n + 1) * GDN_C)
        q = l2n(act_ref[rows, h * C_DIM:(h + 1) * C_DIM]) * (C_DIM ** -0.5)
        k = l2n(act_ref[rows, BRANCH + h * C_DIM:BRANCH + (h + 1) * C_DIM])
        v = act_ref[rows, 2 * BRANCH + h * C_DIM:2 * BRANCH + (h + 1) * C_DIM]
        beta = beta_all[rows, SC_BETA + h:SC_BETA + h + 1]
        gcol = gc_all[rows, SC_DECAY + h:SC_DECAY + h + 1]
        grow = gc_t[SC_DECAY + h:SC_DECAY + h + 1, n * GDN_C:(n + 1) * GDN_C]
        g_last = gcol[GDN_C - 1:GDN_C, :]
        decay = jnp.exp(jnp.where(incl, gcol - grow, -jnp.inf))
        kb = k * beta
        k16 = k.astype(BF16)
        q16 = q.astype(BF16)
        a_mat = jnp.where(strict, _dot_nt(kb.astype(BF16), k16) * decay, 0.0)
        np_ref[idx, :, 0:GDN_C] = -a_mat
        np_ref[idx, :, GDN_C:2 * GDN_C] = _dot_solve(a_mat, a_mat)
        rhs_ref[idx, :, 0:C_DIM] = v * beta
        rhs_ref[idx, :, C_DIM:2 * C_DIM] = kb * jnp.exp(gcol)
        qk_ref[idx] = jnp.where(incl, _dot_nt(q16, k16) * decay, 0.0).astype(BF16)
        qg_ref[idx] = (q * jnp.exp(gcol)).astype(BF16)
        kdt_ref[idx] = (k * jnp.exp(g_last - gcol)).T.astype(BF16)
        dec_ref[idx] = jnp.broadcast_to(jnp.exp(g_last), (1, C_DIM))

    for rnd in range(1, NEUMANN_ROUNDS + 1):
        last = rnd == NEUMANN_ROUNDS
        for idx in range(len(chains)):
            n_old = np_ref[idx, :, 0:GDN_C]
            p_old = np_ref[idx, :, GDN_C:2 * GDN_C]
            if last:
                np_ref[idx, :, 0:GDN_C] = n_old + p_old + _dot_solve(p_old, n_old)
            else:
                prod = _dot_solve(p_old, np_ref[idx])
                np_ref[idx, :, 0:GDN_C] = n_old + p_old + prod[:, 0:GDN_C]
                np_ref[idx, :, GDN_C:2 * GDN_C] = prod[:, GDN_C:2 * GDN_C]

    for idx in range(len(chains)):
        rhs = rhs_ref[idx]
        rhs_ref[idx] = rhs + _dot_solve(np_ref[idx, :, 0:GDN_C], rhs)

    for idx, (h, n) in enumerate(chains):
        rows = slice(n * GDN_C, (n + 1) * GDN_C)
        state = state_ref[h]
        s16 = state.astype(BF16)
        v_new = rhs_ref[idx, :, 0:C_DIM] - _dot(rhs_ref[idx, :, C_DIM:2 * C_DIM].astype(BF16), s16)
        vn16 = v_new.astype(BF16)
        o = _dot(qg_ref[idx], s16) + _dot(qk_ref[idx], vn16)
        state_ref[h] = state * dec_ref[idx] + _dot(kdt_ref[idx], vn16)
        z = z_ref[0, rows, h * C_DIM:(h + 1) * C_DIM]
        y = _rms(o, dn_ref[...]) * _silu(z)
        o_ref[0, rows, h * C_DIM:(h + 1) * C_DIM] = y.astype(o_ref.dtype)


def _gdn(cqkv, cz, sc, conv_w, alog_row, dtb_row, dn_row):
    b, s, width = cqkv.shape
    ts = min(GDN_TS, s)
    nch = C_HEADS * (ts // GDN_C)
    blk = lambda bi, j: (bi, j, 0)
    return pl.pallas_call(
        functools.partial(_gdn_kernel, ts=ts),
        grid=(b, s // ts),
        in_specs=[
            pl.BlockSpec((1, ts, width), blk),
            pl.BlockSpec((1, ts, BRANCH), blk),
            pl.BlockSpec((1, ts, LANES), blk),
            _resident((CONV_WIDTH, width)),
            _resident((1, LANES)),
            _resident((1, LANES)),
            _resident((1, C_DIM)),
        ],
        out_specs=pl.BlockSpec((1, ts, BRANCH), blk),
        out_shape=jax.ShapeDtypeStruct((b, s, BRANCH), BF16),
        scratch_shapes=[
            pltpu.VMEM((C_HEADS, C_DIM, C_DIM), F32),
            pltpu.VMEM((TAIL, width), F32),
            pltpu.VMEM((TAIL + ts, width), F32),
            pltpu.VMEM((ts, width), F32),
            pltpu.VMEM((nch, GDN_C, 2 * GDN_C), F32),
            pltpu.VMEM((nch, GDN_C, 2 * C_DIM), F32),
            pltpu.VMEM((nch, GDN_C, GDN_C), BF16),
            pltpu.VMEM((nch, GDN_C, C_DIM), BF16),
            pltpu.VMEM((nch, C_DIM, GDN_C), BF16),
            pltpu.VMEM((nch, 1, C_DIM), F32),
        ],
        compiler_params=_cparams(("parallel", "arbitrary")),
        name="gated_deltanet",
    )(cqkv, cz, sc, conv_w, alog_row, dtb_row, dn_row)


MERGE_TM = 512


def _merge_kernel(x_ref, ya_ref, yb_ref, yc_ref, gl_ref, bg_ref,
                  wa_ref, wb_ref, wc_ref, wo_ref, o_ref):
    merged = None
    for n, (y_ref, w_ref) in enumerate(((ya_ref, wa_ref), (yb_ref, wb_ref), (yc_ref, wc_ref))):
        cols = slice(n * D_MODEL, (n + 1) * D_MODEL)
        gate = jax.nn.sigmoid(gl_ref[:, cols] + bg_ref[:, cols])
        term = gate * _dot(y_ref[...], w_ref[...])
        merged = term if merged is None else merged + term
    o_ref[...] = x_ref[...] + _dot(merged.astype(BF16), wo_ref[...])


def _merge(x, ya, yb, yc, gl, b_gate, wa, wb, wc, wo):
    n = x.shape[0]
    tm = min(MERGE_TM, n)
    row = lambda i: (i, 0)
    return pl.pallas_call(
        _merge_kernel,
        grid=(n // tm,),
        in_specs=[
            pl.BlockSpec((tm, D_MODEL), row),
            pl.BlockSpec((tm, BRANCH), row),
            pl.BlockSpec((tm, BRANCH), row),
            pl.BlockSpec((tm, BRANCH), row),
            pl.BlockSpec((tm, 3 * D_MODEL), row),
            _resident((1, 3 * D_MODEL)),
            _resident((BRANCH, D_MODEL)),
            _resident((BRANCH, D_MODEL)),
            _resident((BRANCH, D_MODEL)),
            _resident((D_MODEL, D_MODEL)),
        ],
        out_specs=pl.BlockSpec((tm, D_MODEL), row),
        out_shape=jax.ShapeDtypeStruct((n, D_MODEL), F32),
        compiler_params=_cparams(("parallel",)),
        name="gated_merge",
    )(x, ya, yb, yc, gl, b_gate, wa, wb, wc, wo)


def _layout_w_in(w_in):
    offs = np.concatenate([[0], np.cumsum(IN_SIZES)])
    seg = lambda n: w_in[:, offs[n]:offs[n + 1]]
    (a_q, a_k, a_v, i_q, i_k, i_w, b_qkv, b_f, c_qkv, c_z, c_beta, c_a, gates) = (
        seg(n) for n in range(len(IN_SIZES)))
    zeros = lambda n: jnp.zeros((w_in.shape[0], n), w_in.dtype)
    a_grp = jnp.concatenate([a_q, i_q, a_k, i_k, a_v, zeros(HEAD_DIM)], axis=1)
    sc_grp = jnp.concatenate([b_f, i_w, c_beta, c_a, zeros(W_SC - 20)], axis=1)
    return jnp.concatenate([a_grp, sc_grp, b_qkv, c_qkv, c_z, gates], axis=1).astype(BF16)


def _rotary_tables(seq):
    pos = jnp.arange(seq, dtype=F32)
    inv_freq = jnp.power(ROPE_THETA, -jnp.arange(0, ROT_DIM, 2, dtype=F32) / ROT_DIM)
    ang = pos[:, None] * inv_freq[None, :]
    cos, sin = jnp.cos(ang), jnp.sin(ang)
    half = ROT_DIM // 2
    ones = jnp.ones((seq, HEAD_DIM - ROT_DIM), F32)
    zeros_h = jnp.zeros((seq, half), F32)
    zeros_r = jnp.zeros((seq, HEAD_DIM - ROT_DIM), F32)
    c64 = jnp.concatenate([cos, cos, ones], axis=1)
    s1_64 = jnp.concatenate([zeros_h, sin, zeros_r], axis=1)
    s2_64 = jnp.concatenate([-sin, zeros_h, zeros_r], axis=1)
    twice = lambda t: jnp.concatenate([t, t], axis=1)
    return twice(c64), twice(s1_64), twice(s2_64)


def _lane_row(values, start):
    return jnp.zeros((1, LANES), F32).at[0, start:start + values.shape[0]].set(values.astype(F32))


def kernel(x, ffn1_norm, ffn1_w_in, ffn1_w_out, mix_norm, w_in, b_gate, b_forget, conv_w, a_log, dt_bias, delta_norm, w_branch_a, w_branch_b, w_branch_c, w_out, ffn2_norm, ffn2_w_in, ffn2_w_out, final_norm):
    batch, seq, _ = x.shape
    depth = w_in.shape[0]
    top_k = min(INDEX_TOPK, seq // 4)
    cos, s1, s2 = _rotary_tables(seq)
    final_row = final_norm.reshape(1, D_MODEL)
    xt = x.reshape(batch * seq, D_MODEL)
    for l in range(depth):
        xt = _ffn(xt, ffn1_norm[l].reshape(1, D_MODEL), ffn1_w_in[l].astype(BF16),
                  ffn1_w_out[l].astype(BF16), final_row, False)
        a_grp, sc, b_grp, cqkv, cz, gl = _proj(
            xt, mix_norm[l].reshape(1, D_MODEL), _layout_w_in(w_in[l]), cos, s1, s2, seq)
        a_grp = a_grp.reshape(batch, seq, W_A)
        sc = sc.reshape(batch, seq, W_SC)
        y_a = _dsa(a_grp, sc, top_k)
        neg_c = _fox_gate(sc, _lane_row(b_forget[l], SC_BF))
        y_b = _fox(b_grp.reshape(batch, seq, W_B), neg_c)
        y_c = _gdn(cqkv.reshape(batch, seq, W_CQKV), cz.reshape(batch, seq, W_CZ), sc,
                   conv_w[l], _lane_row(a_log[l], SC_DECAY), _lane_row(dt_bias[l], SC_DECAY),
                   delta_norm[l].reshape(1, C_DIM))
        flat = lambda t: t.reshape(batch * seq, BRANCH)
        xt = _merge(xt, flat(y_a), flat(y_b), flat(y_c), gl, b_gate[l].reshape(1, 3 * D_MODEL),
                    w_branch_a[l].astype(BF16), w_branch_b[l].astype(BF16),
                    w_branch_c[l].astype(BF16), w_out[l].astype(BF16))
        xt = _ffn(xt, ffn2_norm[l].reshape(1, D_MODEL), ffn2_w_in[l].astype(BF16),
                  ffn2_w_out[l].astype(BF16), final_row, l == depth - 1)
    return xt.reshape(batch, seq, D_MODEL)
```

```python
import functools

import numpy as np
import jax
import jax.numpy as jnp
from jax import lax
from jax.experimental import pallas as pl
from jax.experimental.pallas import tpu as pltpu

F32 = jnp.float32
BF16 = jnp.bfloat16

D_MODEL = 1024
BRANCH = 512
A_HEADS = 8
HEAD_DIM = 64
IDX_HEADS = 4
INDEX_TOPK = 256
B_HEADS = 8
C_HEADS = 4
C_DIM = 128
CONV_WIDTH = 4
ROPE_THETA = 500000.0
ROT_DIM = 16
FFN_DIM = 2048
NORM_EPS = 1e-6
L2_EPS = 1e-6
IN_SIZES = (512, 64, 64, 256, 64, 4, 1536, 8, 1536, 512, 4, 4, 3072)

LANES = 128
SC_BF = 0
SC_IW = 8
SC_BETA = 12
SC_DECAY = 16
W_A, W_SC, W_B, W_CQKV, W_CZ, W_G = 1024, 128, 1536, 1536, 512, 3072
W_TOTAL = W_A + W_SC + W_B + W_CQKV + W_CZ + W_G
ROT_TILES = 7

NEG_BIG = -1e30
NEG_MASK = -2e30
VMEM_LIMIT = 56 * 1024 * 1024


def _cparams(sem):
    return pltpu.CompilerParams(dimension_semantics=sem, vmem_limit_bytes=VMEM_LIMIT)


def _dot(a, b):
    return jnp.dot(a, b, preferred_element_type=F32)


def _dot_hi(a, b):
    return jnp.dot(a, b, preferred_element_type=F32, precision=lax.Precision.HIGHEST)


def _dot_solve(a, b):
    a_hi = a.astype(BF16)
    b_hi = b.astype(BF16)
    a_lo = (a - a_hi.astype(F32)).astype(BF16)
    b_lo = (b - b_hi.astype(F32)).astype(BF16)
    m = a.shape[0]
    top = _dot(jnp.concatenate([a_hi, a_lo], axis=0), b_hi)
    return top[0:m] + top[m:2 * m] + _dot(a_hi, b_lo)


def _dot_nt(a, b):
    return lax.dot_general(a, b, (((1,), (1,)), ((), ())), preferred_element_type=F32)


def _rms(x, gain):
    return x * lax.rsqrt(jnp.mean(x * x, axis=-1, keepdims=True) + NORM_EPS) * gain


def _silu(x):
    return x * jax.nn.sigmoid(x)


def _resident(shape):
    nd = len(shape)
    return pl.BlockSpec(shape, lambda *_: (0,) * nd, pipeline_mode=pl.Buffered(1))


FFN_TM = 512
FFN_CHUNK = 512


def _ffn_kernel(x_ref, g_ref, win_ref, wout_ref, fg_ref, o_ref, *, final):
    x = x_ref[...]
    h = _rms(x, g_ref[...]).astype(BF16)
    acc = jnp.zeros(x.shape, F32)
    for c in range(FFN_DIM // FFN_CHUNK):
        lo = c * FFN_CHUNK
        gate = _dot(h, win_ref[:, lo:lo + FFN_CHUNK])
        up = _dot(h, win_ref[:, FFN_DIM + lo:FFN_DIM + lo + FFN_CHUNK])
        act = (_silu(gate) * up).astype(BF16)
        acc = acc + _dot(act, wout_ref[lo:lo + FFN_CHUNK, :])
    y = x + 0.5 * acc
    if final:
        y = _rms(y, fg_ref[...])
    o_ref[...] = y


def _ffn(x, gain, w_in, w_out, final_gain, final):
    n = x.shape[0]
    tm = min(FFN_TM, n)
    return pl.pallas_call(
        functools.partial(_ffn_kernel, final=final),
        grid=(n // tm,),
        in_specs=[
            pl.BlockSpec((tm, D_MODEL), lambda i: (i, 0)),
            _resident((1, D_MODEL)),
            _resident((D_MODEL, 2 * FFN_DIM)),
            _resident((FFN_DIM, D_MODEL)),
            _resident((1, D_MODEL)),
        ],
        out_specs=pl.BlockSpec((tm, D_MODEL), lambda i: (i, 0)),
        out_shape=jax.ShapeDtypeStruct((n, D_MODEL), F32),
        compiler_params=_cparams(("parallel",)),
        name="ffn_half",
    )(x, gain, w_in, w_out, final_gain)


PROJ_TM = 256
PROJ_CHUNK = 512


def _proj_kernel(x_ref, g_ref, w_ref, cos_ref, s1_ref, s2_ref,
                 a_ref, sc_ref, b_ref, cqkv_ref, cz_ref, gl_ref):
    h = _rms(x_ref[...], g_ref[...]).astype(BF16)
    cos, s1, s2 = cos_ref[...], s1_ref[...], s2_ref[...]

    off = 0
    for j in range(W_A // LANES):
        t = _dot(h, w_ref[:, off:off + LANES])
        if j < ROT_TILES:
            t = t * cos + pltpu.roll(t, 8, 1) * s1 + pltpu.roll(t, LANES - 8, 1) * s2
        a_ref[:, off:off + LANES] = t.astype(BF16)
        off += LANES
    sc_ref[...] = _dot(h, w_ref[:, off:off + W_SC])
    off += W_SC
    for ref, width in ((b_ref, W_B), (cqkv_ref, W_CQKV), (cz_ref, W_CZ), (gl_ref, W_G)):
        for lo in range(0, width, PROJ_CHUNK):
            ref[:, lo:lo + PROJ_CHUNK] = _dot(
                h, w_ref[:, off + lo:off + lo + PROJ_CHUNK]).astype(ref.dtype)
        off += width


def _proj(x, gain, w, cos, s1, s2, seq):
    n = x.shape[0]
    tm = min(PROJ_TM, seq)
    per_seq = seq // tm
    row = lambda i: (i, 0)
    tab = lambda i: (i % per_seq, 0)
    widths = (W_A, W_SC, W_B, W_CQKV, W_CZ, W_G)
    dtypes = (BF16, F32, BF16, F32, F32, F32)
    return pl.pallas_call(
        _proj_kernel,
        grid=(n // tm,),
        in_specs=[
            pl.BlockSpec((tm, D_MODEL), row),
            _resident((1, D_MODEL)),
            _resident((D_MODEL, W_TOTAL)),
            pl.BlockSpec((tm, LANES), tab),
            pl.BlockSpec((tm, LANES), tab),
            pl.BlockSpec((tm, LANES), tab),
        ],
        out_specs=[pl.BlockSpec((tm, wd), row) for wd in widths],
        out_shape=[jax.ShapeDtypeStruct((n, wd), dt) for wd, dt in zip(widths, dtypes)],
        compiler_params=_cparams(("parallel",)),
        name="mixer_proj",
    )(x, gain, w, cos, s1, s2)


DSA_TQ = 256
DSA_CK = 512
SEL_ROWS = 128
INT_MIN = -2 ** 31
KEY_NEG_INF = int(np.array(-np.inf, np.float32).view(np.int32)) ^ 0x7FFFFFFF


def _dsa_kernel(qa_ref, qi_ref, kk_ref, vv_ref, sc_ref, o_ref,
                key_ref, bias_ref, qs_ref, m_ref, l_ref, alpha_ref, acc_ref, p_ref, *, top_k, ck):
    i = pl.program_id(1)
    tq = qa_ref.shape[1]
    nck = (i * tq + tq + ck - 1) // ck
    lane = lax.broadcasted_iota(jnp.int32, (tq, LANES), 1)
    lo_half = lane < HEAD_DIM
    scale = HEAD_DIM ** -0.5
    n_att = A_HEADS * tq

    def put(h, t):
        qs_ref[h * tq:(h + 1) * tq, :] = (t * scale).astype(BF16)

    for j in range(A_HEADS // 2):
        t = qa_ref[0, :, j * LANES:(j + 1) * LANES].astype(F32)
        put(2 * j, jnp.where(lo_half, t, 0.0))
        put(2 * j + 1, jnp.where(lo_half, pltpu.roll(t, HEAD_DIM, 1), 0.0))
    for j in range(IDX_HEADS // 2):
        t = qi_ref[0, :, j * LANES:(j + 1) * LANES].astype(F32)
        put(A_HEADS + 2 * j, jnp.where(lo_half, 0.0, pltpu.roll(t, HEAD_DIM, 1)))
        put(A_HEADS + 2 * j + 1, jnp.where(lo_half, 0.0, t))

    w_idx = sc_ref[0, :, SC_IW:SC_IW + IDX_HEADS] * (IDX_HEADS ** -0.5)
    row_pos = i * tq + lax.broadcasted_iota(jnp.int32, (tq, ck), 0)
    col_iota = lax.broadcasted_iota(jnp.int32, (tq, ck), 1)

    def key_rows(c):
        return pl.ds(pl.multiple_of(c * ck, ck), ck)

    def score_chunk(c, _):
        rel = jnp.maximum(_dot_nt(qs_ref[n_att:, :], kk_ref[0, key_rows(c), :]), 0.0)
        score = rel[0:tq] * w_idx[:, 0:1]
        for h in range(1, IDX_HEADS):
            score = score + rel[h * tq:(h + 1) * tq] * w_idx[:, h:h + 1]
        score = jnp.where(score == 0.0, 0.0, score)
        score = jnp.where(col_iota + c * ck <= row_pos, score, -jnp.inf)
        bits = lax.bitcast_convert_type(score, jnp.int32)
        key_ref[c] = jnp.where(bits < 0, bits ^ 0x7FFFFFFF, bits)
        return 0

    lax.fori_loop(0, nck, score_chunk, 0)

    groups = tq // SEL_ROWS

    def count(preds):
        accs = []
        for g, pred in enumerate(preds):
            def body(c, acc, g=g, pred=pred):
                kc = key_ref[c, g * SEL_ROWS:(g + 1) * SEL_ROWS, :]
                for j in range(ck // LANES):
                    acc = acc + jnp.where(pred(kc[:, j * LANES:(j + 1) * LANES]), 1.0, 0.0)
                return acc
            accs.append(lax.fori_loop(0, nck, body, jnp.zeros((SEL_ROWS, LANES), F32)))
        return [jnp.sum(acc, axis=-1, keepdims=True) for acc in accs]

    kf = float(top_k)

    def bit_step(b, thrs):
        cands = [t + lax.shift_left(jnp.int32(1), 31 - b) for t in thrs]
        cnts = count([lambda kc, cand=cand: kc >= cand for cand in cands])
        return tuple(jnp.where(cnt >= kf, cand, t) for cnt, cand, t in zip(cnts, cands, thrs))

    thrs = lax.fori_loop(0, 32, bit_step,
                         tuple(jnp.full((SEL_ROWS, 1), INT_MIN, jnp.int32) for _ in range(groups)))
    thrs = [jnp.maximum(t, KEY_NEG_INF + 1) for t in thrs]
    cnt_ge = count([lambda kc, t=t: kc >= t for t in thrs])
    thr = jnp.concatenate(thrs, axis=0) if groups > 1 else thrs[0]

    def write_bias(c, _):
        bias_ref[c] = jnp.where(key_ref[c] >= thr, 0.0, NEG_MASK)
        return 0

    lax.fori_loop(0, nck, write_bias, 0)

    excess = cnt_ge[0]
    for extra in cnt_ge[1:]:
        excess = jnp.maximum(excess, extra)

    @pl.when(jnp.max(excess) > kf)
    def _():
        cnt_gt = count([lambda kc, t=t: kc > t for t in thrs])
        need = kf - (jnp.concatenate(cnt_gt, axis=0) if groups > 1 else cnt_gt[0])
        r = lax.broadcasted_iota(jnp.int32, (ck, ck), 0)
        cidx = lax.broadcasted_iota(jnp.int32, (ck, ck), 1)
        upper = jnp.where(r <= cidx, 1.0, 0.0).astype(BF16)

        def tie_chunk(c, before):
            kc = key_ref[c]
            tie = kc == thr
            rank = before + _dot(jnp.where(tie, 1.0, 0.0).astype(BF16), upper)
            keep = (kc > thr) | (tie & (rank <= need))
            bias_ref[c] = jnp.where(keep, 0.0, NEG_MASK)
            return rank[:, ck - 1:ck]

        lax.fori_loop(0, nck, tie_chunk, jnp.zeros((tq, 1), F32))

    m_ref[...] = jnp.full(m_ref.shape, NEG_BIG, F32)
    l_ref[...] = jnp.zeros(l_ref.shape, F32)
    acc_ref[...] = jnp.zeros(acc_ref.shape, F32)

    def attend(c, _):
        s_all = _dot_nt(qs_ref[0:n_att, :], kk_ref[0, key_rows(c), :])
        bias = bias_ref[c]
        for h in range(A_HEADS):
            r = slice(h * tq, (h + 1) * tq)
            s = s_all[r] + bias
            m_old = m_ref[r]
            m_new = jnp.maximum(m_old, jnp.max(s, axis=-1, keepdims=True))
            alpha = jnp.exp(m_old - m_new)
            p = jnp.exp(s - jnp.tile(m_new, (1, ck // LANES)))
            l_ref[r] = alpha * l_ref[r] + jnp.sum(p, axis=-1, keepdims=True)
            m_ref[r] = m_new
            alpha_ref[r] = alpha
            p_ref[r] = p.astype(BF16)
        acc_ref[...] = alpha_ref[...] * acc_ref[...] + _dot(p_ref[...], vv_ref[0, key_rows(c), :])
        return 0

    lax.fori_loop(0, nck, attend, 0)

    out = acc_ref[...] / l_ref[...]
    for j in range(A_HEADS // 2):
        even = out[(2 * j) * tq:(2 * j + 1) * tq]
        odd = out[(2 * j + 1) * tq:(2 * j + 2) * tq]
        o_ref[0, :, j * LANES:(j + 1) * LANES] = (
            even + pltpu.roll(odd, HEAD_DIM, 1)).astype(o_ref.dtype)


def _dsa(a_grp, sc, top_k):
    b, s, _ = a_grp.shape
    tq = min(DSA_TQ, s)
    ck = min(DSA_CK, s)
    return pl.pallas_call(
        functools.partial(_dsa_kernel, top_k=top_k, ck=ck),
        grid=(b, s // tq),
        in_specs=[
            pl.BlockSpec((1, tq, 512), lambda bi, i: (bi, i, 0)),
            pl.BlockSpec((1, tq, 256), lambda bi, i: (bi, i, 2)),
            pl.BlockSpec((1, s, LANES), lambda bi, i: (bi, 0, 6)),
            pl.BlockSpec((1, s, LANES), lambda bi, i: (bi, 0, 7)),
            pl.BlockSpec((1, tq, LANES), lambda bi, i: (bi, i, 0)),
        ],
        out_specs=pl.BlockSpec((1, tq, BRANCH), lambda bi, i: (bi, i, 0)),
        out_shape=jax.ShapeDtypeStruct((b, s, BRANCH), BF16),
        scratch_shapes=[
            pltpu.VMEM((s // ck, tq, ck), jnp.int32),
            pltpu.VMEM((s // ck, tq, ck), F32),
            pltpu.VMEM(((A_HEADS + IDX_HEADS) * tq, LANES), BF16),
            pltpu.VMEM((A_HEADS * tq, LANES), F32),
            pltpu.VMEM((A_HEADS * tq, LANES), F32),
            pltpu.VMEM((A_HEADS * tq, LANES), F32),
            pltpu.VMEM((A_HEADS * tq, LANES), F32),
            pltpu.VMEM((A_HEADS * tq, ck), BF16),
        ],
        compiler_params=_cparams(("parallel", "arbitrary")),
        name="dsa_attention",
    )(a_grp, a_grp, a_grp, a_grp, sc)


SLAB = 64


def _dsa_t_kernel(qa_ref, qi_ref, kk_ref, vv_ref, sc_ref, o_ref,
                  vt_ref, key_ref, bias_ref, qs_ref, m_ref, l_ref, alpha_ref, acc_ref,
                  s_ref, p_ref, *, top_k, ck):
    i = pl.program_id(1)
    tq = qa_ref.shape[1]
    n_chunks_total = key_ref.shape[0]
    nck = (i * tq + tq + ck - 1) // ck
    lane = lax.broadcasted_iota(jnp.int32, (tq, LANES), 1)
    lo_half = lane < HEAD_DIM
    scale = HEAD_DIM ** -0.5

    @pl.when(i == 0)
    def _():
        for c in range(n_chunks_total):
            vt_ref[c] = vv_ref[0, c * ck:(c + 1) * ck, :].astype(F32).T.astype(BF16)

    def put(h, t):
        qs_ref[h * tq:(h + 1) * tq, :] = (t * scale).astype(BF16)

    for j in range(A_HEADS // 2):
        t = qa_ref[0, :, j * LANES:(j + 1) * LANES].astype(F32)
        put(2 * j, jnp.where(lo_half, t, 0.0))
        put(2 * j + 1, jnp.where(lo_half, pltpu.roll(t, HEAD_DIM, 1), 0.0))
    for j in range(IDX_HEADS // 2):
        t = qi_ref[0, :, j * LANES:(j + 1) * LANES].astype(F32)
        put(A_HEADS + 2 * j, jnp.where(lo_half, 0.0, pltpu.roll(t, HEAD_DIM, 1)))
        put(A_HEADS + 2 * j + 1, jnp.where(lo_half, 0.0, t))

    w_rows = (sc_ref[0] * (IDX_HEADS ** -0.5)).T
    key_pos = lax.broadcasted_iota(jnp.int32, (ck, tq), 0)
    qry_pos = i * tq + lax.broadcasted_iota(jnp.int32, (ck, tq), 1)

    def key_rows(c):
        return pl.ds(pl.multiple_of(c * ck, ck), ck)

    def score_chunk(c, _):
        kc = kk_ref[0, key_rows(c), :]
        score = None
        for h in range(IDX_HEADS):
            rel = jnp.maximum(
                _dot_nt(kc, qs_ref[(A_HEADS + h) * tq:(A_HEADS + h + 1) * tq, :]), 0.0)
            term = rel * w_rows[SC_IW + h:SC_IW + h + 1, :]
            score = term if score is None else score + term
        score = jnp.where(score == 0.0, 0.0, score)
        score = jnp.where(key_pos + c * ck <= qry_pos, score, -jnp.inf)
        bits = lax.bitcast_convert_type(score, jnp.int32)
        key_ref[c] = jnp.where(bits < 0, bits ^ 0x7FFFFFFF, bits)
        return 0

    lax.fori_loop(0, nck, score_chunk, 0)

    def count(pred):
        def body(c, acc):
            for r0 in range(0, ck, SLAB):
                acc = acc + jnp.where(pred(key_ref[c, r0:r0 + SLAB, :]), 1.0, 0.0)
            return acc
        acc = lax.fori_loop(0, nck, body, jnp.zeros((SLAB, tq), F32))
        return jnp.sum(acc, axis=0, keepdims=True)

    kf = float(top_k)

    def bit_step(b, thr):
        cand = thr + lax.shift_left(jnp.int32(1), 31 - b)
        cand_slab = jnp.broadcast_to(cand, (SLAB, tq))
        cnt = count(lambda kc: kc >= cand_slab)
        return jnp.where(cnt >= kf, cand, thr)

    thr = lax.fori_loop(0, 32, bit_step, jnp.full((1, tq), INT_MIN, jnp.int32))
    thr = jnp.maximum(thr, KEY_NEG_INF + 1)
    thr_slab = jnp.broadcast_to(thr, (SLAB, tq))
    cnt_ge = count(lambda kc: kc >= thr_slab)

    def write_bias(c, _):
        bias_ref[c] = jnp.where(key_ref[c] >= thr, 0.0, NEG_MASK)
        return 0

    lax.fori_loop(0, nck, write_bias, 0)

    @pl.when(jnp.max(cnt_ge) > kf)
    def _():
        need = kf - count(lambda kc: kc > thr_slab)
        r = lax.broadcasted_iota(jnp.int32, (ck, ck), 0)
        cidx = lax.broadcasted_iota(jnp.int32, (ck, ck), 1)
        lower = jnp.where(cidx <= r, 1.0, 0.0).astype(BF16)

        def tie_chunk(c, before):
            kc = key_ref[c]
            tie = kc == thr
            rank = before + _dot(lower, jnp.where(tie, 1.0, 0.0).astype(BF16))
            keep = (kc > thr) | (tie & (rank <= need))
            bias_ref[c] = jnp.where(keep, 0.0, NEG_MASK)
            return rank[ck - 1:ck, :]

        lax.fori_loop(0, nck, tie_chunk, jnp.zeros((1, tq), F32))

    m_ref[...] = jnp.full(m_ref.shape, NEG_BIG, F32)
    l_ref[...] = jnp.zeros(l_ref.shape, F32)
    acc_ref[...] = jnp.zeros(acc_ref.shape, F32)
    reps = ck // 8

    n_att = A_HEADS * tq

    def attend(c, _):
        s_ref[...] = _dot_nt(kk_ref[0, key_rows(c), :], qs_ref[0:n_att, :])
        bias = bias_ref[c]
        for h in range(A_HEADS):
            cols = slice(h * tq, (h + 1) * tq)
            s = s_ref[:, cols] + bias
            m_old = m_ref[:, cols]
            m_new = jnp.maximum(m_old, jnp.max(s, axis=0, keepdims=True))
            alpha = jnp.exp(m_old - m_new)
            p = jnp.exp(s - jnp.tile(m_new, (reps, 1)))
            l_ref[:, cols] = alpha * l_ref[:, cols] + jnp.sum(p, axis=0, keepdims=True)
            m_ref[:, cols] = m_new
            alpha_ref[:, cols] = alpha
            p_ref[:, cols] = p.astype(BF16)
        acc_ref[...] = (jnp.tile(alpha_ref[...], (LANES // 8, 1)) * acc_ref[...]
                        + _dot(vt_ref[c], p_ref[...]))
        return 0

    lax.fori_loop(0, nck, attend, 0)

    for j in range(A_HEADS // 2):
        outs = []
        for h in (2 * j, 2 * j + 1):
            cols = slice(h * tq, (h + 1) * tq)
            out_t = acc_ref[:, cols] / jnp.tile(l_ref[:, cols], (LANES // 8, 1))
            outs.append(out_t.T)
        o_ref[0, :, j * LANES:(j + 1) * LANES] = (
            outs[0] + pltpu.roll(outs[1], HEAD_DIM, 1)).astype(o_ref.dtype)


def _dsa_t(a_grp, sc, top_k):
    b, s, _ = a_grp.shape
    tq = min(DSA_TQ, s)
    ck = min(DSA_CK, s)
    return pl.pallas_call(
        functools.partial(_dsa_t_kernel, top_k=top_k, ck=ck),
        grid=(b, s // tq),
        in_specs=[
            pl.BlockSpec((1, tq, 512), lambda bi, i: (bi, i, 0)),
            pl.BlockSpec((1, tq, 256), lambda bi, i: (bi, i, 2)),
            pl.BlockSpec((1, s, LANES), lambda bi, i: (bi, 0, 6)),
            pl.BlockSpec((1, s, LANES), lambda bi, i: (bi, 0, 7)),
            pl.BlockSpec((1, tq, LANES), lambda bi, i: (bi, i, 0)),
        ],
        out_specs=pl.BlockSpec((1, tq, BRANCH), lambda bi, i: (bi, i, 0)),
        out_shape=jax.ShapeDtypeStruct((b, s, BRANCH), BF16),
        scratch_shapes=[
            pltpu.VMEM((s // ck, LANES, ck), BF16),
            pltpu.VMEM((s // ck, ck, tq), jnp.int32),
            pltpu.VMEM((s // ck, ck, tq), F32),
            pltpu.VMEM(((A_HEADS + IDX_HEADS) * tq, LANES), BF16),
            pltpu.VMEM((8, A_HEADS * tq), F32),
            pltpu.VMEM((8, A_HEADS * tq), F32),
            pltpu.VMEM((8, A_HEADS * tq), F32),
            pltpu.VMEM((LANES, A_HEADS * tq), F32),
            pltpu.VMEM((ck, A_HEADS * tq), F32),
            pltpu.VMEM((ck, A_HEADS * tq), BF16),
        ],
        compiler_params=_cparams(("parallel", "arbitrary")),
        name="dsa_attention",
    )(a_grp, a_grp, a_grp, a_grp, sc)


CUM_CHUNK = 256


def _fox_gate_kernel(sc_ref, bf_ref, o_ref, *, chunk):
    s = sc_ref.shape[1]
    r = lax.broadcasted_iota(jnp.int32, (chunk, chunk), 0)
    c = lax.broadcasted_iota(jnp.int32, (chunk, chunk), 1)
    lower = jnp.where(c <= r, 1.0, 0.0)
    carry = jnp.zeros((1, LANES), F32)
    for n in range(s // chunk):
        x = sc_ref[0, n * chunk:(n + 1) * chunk, :] + bf_ref[...]
        log_f = jnp.minimum(x, 0.0) - jnp.log1p(jnp.exp(-jnp.abs(x)))
        cum = _dot_hi(lower, log_f) + carry
        carry = cum[chunk - 1:chunk, :]
        o_ref[0, :, n * chunk:(n + 1) * chunk] = -(cum.T[:B_HEADS, :])


def _fox_gate(sc, bf_row):
    b, s, _ = sc.shape
    chunk = min(CUM_CHUNK, s)
    return pl.pallas_call(
        functools.partial(_fox_gate_kernel, chunk=chunk),
        grid=(b,),
        in_specs=[pl.BlockSpec((1, s, LANES), lambda bi: (bi, 0, 0)), _resident((1, LANES))],
        out_specs=pl.BlockSpec((1, B_HEADS, s), lambda bi: (bi, 0, 0)),
        out_shape=jax.ShapeDtypeStruct((b, B_HEADS, s), F32),
        compiler_params=_cparams(("parallel",)),
        name="fox_gate_cumsum",
    )(sc, bf_row)


FOX_T = 256


def _fox_kernel(q_ref, k_ref, v_ref, nc_ref, o_ref,
                qz_ref, m_ref, l_ref, alpha_ref, acc_ref, s_ref, p_ref, *, t):
    i = pl.program_id(1)
    pairs = B_HEADS // 2
    lane = lax.broadcasted_iota(jnp.int32, (t, LANES), 1)
    lo_half = lane < HEAD_DIM
    for hp in range(pairs):
        q = q_ref[0, :, hp * LANES:(hp + 1) * LANES].astype(F32) * (HEAD_DIM ** -0.5)
        qz_ref[(2 * hp) * t:(2 * hp + 1) * t, :] = jnp.where(lo_half, q, 0.0).astype(BF16)
        qz_ref[(2 * hp + 1) * t:(2 * hp + 2) * t, :] = jnp.where(lo_half, 0.0, q).astype(BF16)
    causal = (lax.broadcasted_iota(jnp.int32, (t, t), 1)
              <= lax.broadcasted_iota(jnp.int32, (t, t), 0))
    m_ref[...] = jnp.full(m_ref.shape, NEG_BIG, F32)
    l_ref[...] = jnp.zeros(l_ref.shape, F32)
    acc_ref[...] = jnp.zeros(acc_ref.shape, F32)

    def attend(c, _, masked):
        rows = pl.ds(pl.multiple_of(c * t, t), t)
        for hp in range(pairs):
            r2 = slice(2 * hp * t, (2 * hp + 2) * t)
            s_ref[r2, :] = _dot_nt(qz_ref[r2, :], k_ref[0, rows, hp * LANES:(hp + 1) * LANES])
        for h in range(B_HEADS):
            r = slice(h * t, (h + 1) * t)
            s = s_ref[r, :] + nc_ref[0, h, pl.ds(c, 1), :]
            if masked:
                s = jnp.where(causal, s, -jnp.inf)
            m_old = m_ref[r]
            m_new = jnp.maximum(m_old, jnp.max(s, axis=-1, keepdims=True))
            alpha = jnp.exp(m_old - m_new)
            p = jnp.exp(s - jnp.tile(m_new, (1, t // LANES)))
            l_ref[r] = alpha * l_ref[r] + jnp.sum(p, axis=-1, keepdims=True)
            m_ref[r] = m_new
            alpha_ref[r] = alpha
            p_ref[r, :] = p.astype(BF16)
        for hp in range(pairs):
            r2 = slice(2 * hp * t, (2 * hp + 2) * t)
            acc_ref[r2] = alpha_ref[r2] * acc_ref[r2] + _dot(
                p_ref[r2, :], v_ref[0, rows, hp * LANES:(hp + 1) * LANES])
        return 0

    lax.fori_loop(0, i, functools.partial(attend, masked=False), 0)
    attend(i, 0, masked=True)
    for hp in range(pairs):
        even = slice(2 * hp * t, (2 * hp + 1) * t)
        odd = slice((2 * hp + 1) * t, (2 * hp + 2) * t)
        out = jnp.where(lo_half, acc_ref[even] / l_ref[even], acc_ref[odd] / l_ref[odd])
        o_ref[0, :, hp * LANES:(hp + 1) * LANES] = out.astype(o_ref.dtype)


def _fox(b_grp, neg_c):
    b, s, _ = b_grp.shape
    t = min(FOX_T, s)
    pairs = B_HEADS // 2
    neg_c = neg_c.reshape(b, B_HEADS, s // t, t)
    return pl.pallas_call(
        functools.partial(_fox_kernel, t=t),
        grid=(b, s // t),
        in_specs=[
            pl.BlockSpec((1, t, BRANCH), lambda bi, i: (bi, i, 0)),
            pl.BlockSpec((1, s, BRANCH), lambda bi, i: (bi, 0, 1)),
            pl.BlockSpec((1, s, BRANCH), lambda bi, i: (bi, 0, 2)),
            pl.BlockSpec((1, B_HEADS, s // t, t), lambda bi, i: (bi, 0, 0, 0)),
        ],
        out_specs=pl.BlockSpec((1, t, BRANCH), lambda bi, i: (bi, i, 0)),
        out_shape=jax.ShapeDtypeStruct((b, s, BRANCH), BF16),
        scratch_shapes=[
            pltpu.VMEM((B_HEADS * t, LANES), BF16),
            pltpu.VMEM((B_HEADS * t, LANES), F32),
            pltpu.VMEM((B_HEADS * t, LANES), F32),
            pltpu.VMEM((B_HEADS * t, LANES), F32),
            pltpu.VMEM((B_HEADS * t, LANES), F32),
            pltpu.VMEM((B_HEADS * t, t), F32),
            pltpu.VMEM((B_HEADS * t, t), BF16),
        ],
        compiler_params=_cparams(("parallel", "arbitrary")),
        name="fox_attention",
    )(b_grp, b_grp, b_grp, neg_c)


GDN_TS = 512
GDN_C = 128
TAIL = 8
NEUMANN_ROUNDS = 6


def _gdn_kernel(x_ref, z_ref, sc_ref, cw_ref, alog_ref, dtb_ref, dn_ref, o_ref,
                state_ref, tail_ref, ext_ref, act_ref, np_ref, rhs_ref,
                qk_ref, qg_ref, kdt_ref, dec_ref, *, ts):
    j = pl.program_id(1)
    width = x_ref.shape[2]
    n_chunks = ts // GDN_C
    chains = [(h, n) for n in range(n_chunks) for h in range(C_HEADS)]

    @pl.when(j == 0)
    def _():
        state_ref[...] = jnp.zeros_like(state_ref)
        tail_ref[...] = jnp.zeros_like(tail_ref)

    x = x_ref[0]
    ext_ref[0:TAIL, :] = tail_ref[...]
    ext_ref[TAIL:TAIL + ts, :] = x
    tail_ref[...] = x[ts - TAIL:ts, :]
    conv = jnp.zeros((ts, width), F32)
    for tap in range(CONV_WIDTH):
        start = TAIL - (CONV_WIDTH - 1) + tap
        conv = conv + cw_ref[tap:tap + 1, :] * ext_ref[start:start + ts, :]
    act_ref[...] = _silu(conv)

    sc = sc_ref[0]
    beta_all = jax.nn.sigmoid(sc)
    xg = sc + dtb_ref[...]
    softplus = jnp.maximum(xg, 0.0) + jnp.log1p(jnp.exp(-jnp.abs(xg)))
    g_all = -jnp.exp(alog_ref[...]) * softplus
    r = lax.broadcasted_iota(jnp.int32, (ts, ts), 0)
    c = lax.broadcasted_iota(jnp.int32, (ts, ts), 1)
    same_chunk_lower = jnp.where((c <= r) & (r // GDN_C == c // GDN_C), 1.0, 0.0)
    gc_all = _dot_hi(same_chunk_lower, g_all)
    gc_t = gc_all.T

    ri = lax.broadcasted_iota(jnp.int32, (GDN_C, GDN_C), 0)
    ci = lax.broadcasted_iota(jnp.int32, (GDN_C, GDN_C), 1)
    incl = ri >= ci
    strict = ri > ci

    def l2n(t):
        return t * lax.rsqrt(jnp.sum(t * t, axis=-1, keepdims=True) + L2_EPS)

    for idx, (h, n) in enumerate(chains):
        rows = slice(n * GDN_C, (n + 1) * GDN_C)
        q = l2n(act_ref[rows, h * C_DIM:(h + 1) * C_DIM]) * (C_DIM ** -0.5)
        k = l2n(act_ref[rows, BRANCH + h * C_DIM:BRANCH + (h + 1) * C_DIM])
        v = act_ref[rows, 2 * BRANCH + h * C_DIM:2 * BRANCH + (h + 1) * C_DIM]
        beta = beta_all[rows, SC_BETA + h:SC_BETA + h + 1]
        gcol = gc_all[rows, SC_DECAY + h:SC_DECAY + h + 1]
        grow = gc_t[SC_DECAY + h:SC_DECAY + h + 1, n * GDN_C:(n + 1) * GDN_C]
        g_last = gcol[GDN_C - 1:GDN_C, :]
        decay = jnp.exp(jnp.where(incl, gcol - grow, -jnp.inf))
        kb = k * beta
        k16 = k.astype(BF16)
        q16 = q.astype(BF16)
        a_mat = jnp.where(strict, _dot_nt(kb.astype(BF16), k16) * decay, 0.0)
        np_ref[idx, :, 0:GDN_C] = -a_mat
        np_ref[idx, :, GDN_C:2 * GDN_C] = _dot_solve(a_mat, a_mat)
        rhs_ref[idx, :, 0:C_DIM] = v * beta
        rhs_ref[idx, :, C_DIM:2 * C_DIM] = kb * jnp.exp(gcol)
        qk_ref[idx] = jnp.where(incl, _dot_nt(q16, k16) * decay, 0.0).astype(BF16)
        qg_ref[idx] = (q * jnp.exp(gcol)).astype(BF16)
        kdt_ref[idx] = (k * jnp.exp(g_last - gcol)).T.astype(BF16)
        dec_ref[idx] = jnp.broadcast_to(jnp.exp(g_last), (1, C_DIM))

    for rnd in range(1, NEUMANN_ROUNDS + 1):
        last = rnd == NEUMANN_ROUNDS
        for idx in range(len(chains)):
            n_old = np_ref[idx, :, 0:GDN_C]
            p_old = np_ref[idx, :, GDN_C:2 * GDN_C]
            if last:
                np_ref[idx, :, 0:GDN_C] = n_old + p_old + _dot_solve(p_old, n_old)
            else:
                prod = _dot_solve(p_old, np_ref[idx])
                np_ref[idx, :, 0:GDN_C] = n_old + p_old + prod[:, 0:GDN_C]
                np_ref[idx, :, GDN_C:2 * GDN_C] = prod[:, GDN_C:2 * GDN_C]

    for idx in range(len(chains)):
        rhs = rhs_ref[idx]
        rhs_ref[idx] = rhs + _dot_solve(np_ref[idx, :, 0:GDN_C], rhs)

    for idx, (h, n) in enumerate(chains):
        rows = slice(n * GDN_C, (n + 1) * GDN_C)
        state = state_ref[h]
        s16 = state.astype(BF16)
        v_new = rhs_ref[idx, :, 0:C_DIM] - _dot(rhs_ref[idx, :, C_DIM:2 * C_DIM].astype(BF16), s16)
        vn16 = v_new.astype(BF16)
        o = _dot(qg_ref[idx], s16) + _dot(qk_ref[idx], vn16)
        state_ref[h] = state * dec_ref[idx] + _dot(kdt_ref[idx], vn16)
        z = z_ref[0, rows, h * C_DIM:(h + 1) * C_DIM]
        y = _rms(o, dn_ref[...]) * _silu(z)
        o_ref[0, rows, h * C_DIM:(h + 1) * C_DIM] = y.astype(o_ref.dtype)


def _gdn(cqkv, cz, sc, conv_w, alog_row, dtb_row, dn_row):
    b, s, width = cqkv.shape
    ts = min(GDN_TS, s)
    nch = C_HEADS * (ts // GDN_C)
    blk = lambda bi, j: (bi, j, 0)
    return pl.pallas_call(
        functools.partial(_gdn_kernel, ts=ts),
        grid=(b, s // ts),
        in_specs=[
            pl.BlockSpec((1, ts, width), blk),
            pl.BlockSpec((1, ts, BRANCH), blk),
            pl.BlockSpec((1, ts, LANES), blk),
            _resident((CONV_WIDTH, width)),
            _resident((1, LANES)),
            _resident((1, LANES)),
            _resident((1, C_DIM)),
        ],
        out_specs=pl.BlockSpec((1, ts, BRANCH), blk),
        out_shape=jax.ShapeDtypeStruct((b, s, BRANCH), BF16),
        scratch_shapes=[
            pltpu.VMEM((C_HEADS, C_DIM, C_DIM), F32),
            pltpu.VMEM((TAIL, width), F32),
            pltpu.VMEM((TAIL + ts, width), F32),
            pltpu.VMEM((ts, width), F32),
            pltpu.VMEM((nch, GDN_C, 2 * GDN_C), F32),
            pltpu.VMEM((nch, GDN_C, 2 * C_DIM), F32),
            pltpu.VMEM((nch, GDN_C, GDN_C), BF16),
            pltpu.VMEM((nch, GDN_C, C_DIM), BF16),
            pltpu.VMEM((nch, C_DIM, GDN_C), BF16),
            pltpu.VMEM((nch, 1, C_DIM), F32),
        ],
        compiler_params=_cparams(("parallel", "arbitrary")),
        name="gated_deltanet",
    )(cqkv, cz, sc, conv_w, alog_row, dtb_row, dn_row)


MERGE_TM = 512


def _merge_kernel(x_ref, ya_ref, yb_ref, yc_ref, gl_ref, bg_ref,
                  wa_ref, wb_ref, wc_ref, wo_ref, o_ref):
    merged = None
    for n, (y_ref, w_ref) in enumerate(((ya_ref, wa_ref), (yb_ref, wb_ref), (yc_ref, wc_ref))):
        cols = slice(n * D_MODEL, (n + 1) * D_MODEL)
        gate = jax.nn.sigmoid(gl_ref[:, cols] + bg_ref[:, cols])
        term = gate * _dot(y_ref[...], w_ref[...])
        merged = term if merged is None else merged + term
    o_ref[...] = x_ref[...] + _dot(merged.astype(BF16), wo_ref[...])


def _merge(x, ya, yb, yc, gl, b_gate, wa, wb, wc, wo):
    n = x.shape[0]
    tm = min(MERGE_TM, n)
    row = lambda i: (i, 0)
    return pl.pallas_call(
        _merge_kernel,
        grid=(n // tm,),
        in_specs=[
            pl.BlockSpec((tm, D_MODEL), row),
            pl.BlockSpec((tm, BRANCH), row),
            pl.BlockSpec((tm, BRANCH), row),
            pl.BlockSpec((tm, BRANCH), row),
            pl.BlockSpec((tm, 3 * D_MODEL), row),
            _resident((1, 3 * D_MODEL)),
            _resident((BRANCH, D_MODEL)),
            _resident((BRANCH, D_MODEL)),
            _resident((BRANCH, D_MODEL)),
            _resident((D_MODEL, D_MODEL)),
        ],
        out_specs=pl.BlockSpec((tm, D_MODEL), row),
        out_shape=jax.ShapeDtypeStruct((n, D_MODEL), F32),
        compiler_params=_cparams(("parallel",)),
        name="gated_merge",
    )(x, ya, yb, yc, gl, b_gate, wa, wb, wc, wo)


def _layout_w_in(w_in):
    offs = np.concatenate([[0], np.cumsum(IN_SIZES)])
    seg = lambda n: w_in[:, offs[n]:offs[n + 1]]
    (a_q, a_k, a_v, i_q, i_k, i_w, b_qkv, b_f, c_qkv, c_z, c_beta, c_a, gates) = (
        seg(n) for n in range(len(IN_SIZES)))
    zeros = lambda n: jnp.zeros((w_in.shape[0], n), w_in.dtype)
    a_grp = jnp.concatenate([a_q, i_q, a_k, i_k, a_v, zeros(HEAD_DIM)], axis=1)
    sc_grp = jnp.concatenate([b_f, i_w, c_beta, c_a, zeros(W_SC - 20)], axis=1)
    return jnp.concatenate([a_grp, sc_grp, b_qkv, c_qkv, c_z, gates], axis=1).astype(BF16)


def _rotary_tables(seq):
    pos = jnp.arange(seq, dtype=F32)
    inv_freq = jnp.power(ROPE_THETA, -jnp.arange(0, ROT_DIM, 2, dtype=F32) / ROT_DIM)
    ang = pos[:, None] * inv_freq[None, :]
    cos, sin = jnp.cos(ang), jnp.sin(ang)
    half = ROT_DIM // 2
    ones = jnp.ones((seq, HEAD_DIM - ROT_DIM), F32)
    zeros_h = jnp.zeros((seq, half), F32)
    zeros_r = jnp.zeros((seq, HEAD_DIM - ROT_DIM), F32)
    c64 = jnp.concatenate([cos, cos, ones], axis=1)
    s1_64 = jnp.concatenate([zeros_h, sin, zeros_r], axis=1)
    s2_64 = jnp.concatenate([-sin, zeros_h, zeros_r], axis=1)
    twice = lambda t: jnp.concatenate([t, t], axis=1)
    return twice(c64), twice(s1_64), twice(s2_64)


def _lane_row(values, start):
    return jnp.zeros((1, LANES), F32).at[0, start:start + values.shape[0]].set(values.astype(F32))


def kernel(x, ffn1_norm, ffn1_w_in, ffn1_w_out, mix_norm, w_in, b_gate, b_forget, conv_w, a_log, dt_bias, delta_norm, w_branch_a, w_branch_b, w_branch_c, w_out, ffn2_norm, ffn2_w_in, ffn2_w_out, final_norm):
    batch, seq, _ = x.shape
    depth = w_in.shape[0]
    top_k = min(INDEX_TOPK, seq // 4)
    cos, s1, s2 = _rotary_tables(seq)
    final_row = final_norm.reshape(1, D_MODEL)
    xt = x.reshape(batch * seq, D_MODEL)
    for l in range(depth):
        xt = _ffn(xt, ffn1_norm[l].reshape(1, D_MODEL), ffn1_w_in[l].astype(BF16),
                  ffn1_w_out[l].astype(BF16), final_row, False)
        a_grp, sc, b_grp, cqkv, cz, gl = _proj(
            xt, mix_norm[l].reshape(1, D_MODEL), _layout_w_in(w_in[l]), cos, s1, s2, seq)
        a_grp = a_grp.reshape(batch, seq, W_A)
        sc = sc.reshape(batch, seq, W_SC)
        y_a = _dsa_t(a_grp, sc, top_k)
        neg_c = _fox_gate(sc, _lane_row(b_forget[l], SC_BF))
        y_b = _fox(b_grp.reshape(batch, seq, W_B), neg_c)
        y_c = _gdn(cqkv.reshape(batch, seq, W_CQKV), cz.reshape(batch, seq, W_CZ), sc,
                   conv_w[l], _lane_row(a_log[l], SC_DECAY), _lane_row(dt_bias[l], SC_DECAY),
                   delta_norm[l].reshape(1, C_DIM))
        flat = lambda t: t.reshape(batch * seq, BRANCH)
        xt = _merge(xt, flat(y_a), flat(y_b), flat(y_c), gl, b_gate[l].reshape(1, 3 * D_MODEL),
                    w_branch_a[l].astype(BF16), w_branch_b[l].astype(BF16),
                    w_branch_c[l].astype(BF16), w_out[l].astype(BF16))
        xt = _ffn(xt, ffn2_norm[l].reshape(1, D_MODEL), ffn2_w_in[l].astype(BF16),
                  ffn2_w_out[l].astype(BF16), final_row, l == depth - 1)
    return xt.reshape(batch, seq, D_MODEL)
```

```python
import functools

import numpy as np
import jax
import jax.numpy as jnp
from jax import lax
from jax.experimental import pallas as pl
from jax.experimental.pallas import tpu as pltpu

F32 = jnp.float32
BF16 = jnp.bfloat16

D_MODEL = 1024
BRANCH = 512
A_HEADS = 8
HEAD_DIM = 64
IDX_HEADS = 4
INDEX_TOPK = 256
B_HEADS = 8
C_HEADS = 4
C_DIM = 128
CONV_WIDTH = 4
ROPE_THETA = 500000.0
ROT_DIM = 16
FFN_DIM = 2048
NORM_EPS = 1e-6
L2_EPS = 1e-6
IN_SIZES = (512, 64, 64, 256, 64, 4, 1536, 8, 1536, 512, 4, 4, 3072)

LANES = 128
SC_BF = 0
SC_IW = 8
SC_BETA = 12
SC_DECAY = 16
W_A, W_SC, W_B, W_CQKV, W_CZ, W_G = 1024, 128, 1536, 1536, 512, 3072
W_TOTAL = W_A + W_SC + W_B + W_CQKV + W_CZ + W_G
ROT_TILES = 7

NEG_BIG = -1e30
NEG_MASK = -2e30
VMEM_LIMIT = 56 * 1024 * 1024


def _cparams(sem):
    return pltpu.CompilerParams(dimension_semantics=sem, vmem_limit_bytes=VMEM_LIMIT)


def _dot(a, b):
    return jnp.dot(a, b, preferred_element_type=F32)


def _dot_hi(a, b):
    return jnp.dot(a, b, preferred_element_type=F32, precision=lax.Precision.HIGHEST)


def _dot_solve(a, b):
    a_hi = a.astype(BF16)
    b_hi = b.astype(BF16)
    a_lo = (a - a_hi.astype(F32)).astype(BF16)
    b_lo = (b - b_hi.astype(F32)).astype(BF16)
    m = a.shape[0]
    top = _dot(jnp.concatenate([a_hi, a_lo], axis=0), b_hi)
    return top[0:m] + top[m:2 * m] + _dot(a_hi, b_lo)


def _dot_nt(a, b):
    return lax.dot_general(a, b, (((1,), (1,)), ((), ())), preferred_element_type=F32)


def _rms(x, gain):
    return x * lax.rsqrt(jnp.mean(x * x, axis=-1, keepdims=True) + NORM_EPS) * gain


def _silu(x):
    return x * jax.nn.sigmoid(x)


def _resident(shape):
    nd = len(shape)
    return pl.BlockSpec(shape, lambda *_: (0,) * nd, pipeline_mode=pl.Buffered(1))


FFN_TM = 512
FFN_CHUNK = 512


def _ffn_kernel(x_ref, g_ref, win_ref, wout_ref, fg_ref, o_ref, *, final):
    x = x_ref[...]
    h = _rms(x, g_ref[...]).astype(BF16)
    acc = jnp.zeros(x.shape, F32)
    for c in range(FFN_DIM // FFN_CHUNK):
        lo = c * FFN_CHUNK
        gate = _dot(h, win_ref[:, lo:lo + FFN_CHUNK])
        up = _dot(h, win_ref[:, FFN_DIM + lo:FFN_DIM + lo + FFN_CHUNK])
        act = (_silu(gate) * up).astype(BF16)
        acc = acc + _dot(act, wout_ref[lo:lo + FFN_CHUNK, :])
    y = x + 0.5 * acc
    if final:
        y = _rms(y, fg_ref[...])
    o_ref[...] = y


def _ffn(x, gain, w_in, w_out, final_gain, final):
    n = x.shape[0]
    tm = min(FFN_TM, n)
    return pl.pallas_call(
        functools.partial(_ffn_kernel, final=final),
        grid=(n // tm,),
        in_specs=[
            pl.BlockSpec((tm, D_MODEL), lambda i: (i, 0)),
            _resident((1, D_MODEL)),
            _resident((D_MODEL, 2 * FFN_DIM)),
            _resident((FFN_DIM, D_MODEL)),
            _resident((1, D_MODEL)),
        ],
        out_specs=pl.BlockSpec((tm, D_MODEL), lambda i: (i, 0)),
        out_shape=jax.ShapeDtypeStruct((n, D_MODEL), F32),
        compiler_params=_cparams(("parallel",)),
        name="ffn_half",
    )(x, gain, w_in, w_out, final_gain)


PROJ_TM = 256
PROJ_CHUNK = 512


def _proj_kernel(x_ref, g_ref, w_ref, cos_ref, s1_ref, s2_ref,
                 a_ref, sc_ref, b_ref, cqkv_ref, cz_ref, gl_ref):
    h = _rms(x_ref[...], g_ref[...]).astype(BF16)
    cos, s1, s2 = cos_ref[...], s1_ref[...], s2_ref[...]

    off = 0
    for j in range(W_A // LANES):
        t = _dot(h, w_ref[:, off:off + LANES])
        if j < ROT_TILES:
            t = t * cos + pltpu.roll(t, 8, 1) * s1 + pltpu.roll(t, LANES - 8, 1) * s2
        a_ref[:, off:off + LANES] = t.astype(BF16)
        off += LANES
    sc_ref[...] = _dot(h, w_ref[:, off:off + W_SC])
    off += W_SC
    for ref, width in ((b_ref, W_B), (cqkv_ref, W_CQKV), (cz_ref, W_CZ), (gl_ref, W_G)):
        for lo in range(0, width, PROJ_CHUNK):
            ref[:, lo:lo + PROJ_CHUNK] = _dot(
                h, w_ref[:, off + lo:off + lo + PROJ_CHUNK]).astype(ref.dtype)
        off += width


def _proj(x, gain, w, cos, s1, s2, seq):
    n = x.shape[0]
    tm = min(PROJ_TM, seq)
    per_seq = seq // tm
    row = lambda i: (i, 0)
    tab = lambda i: (i % per_seq, 0)
    widths = (W_A, W_SC, W_B, W_CQKV, W_CZ, W_G)
    dtypes = (BF16, F32, BF16, F32, F32, F32)
    return pl.pallas_call(
        _proj_kernel,
        grid=(n // tm,),
        in_specs=[
            pl.BlockSpec((tm, D_MODEL), row),
            _resident((1, D_MODEL)),
            _resident((D_MODEL, W_TOTAL)),
            pl.BlockSpec((tm, LANES), tab),
            pl.BlockSpec((tm, LANES), tab),
            pl.BlockSpec((tm, LANES), tab),
        ],
        out_specs=[pl.BlockSpec((tm, wd), row) for wd in widths],
        out_shape=[jax.ShapeDtypeStruct((n, wd), dt) for wd, dt in zip(widths, dtypes)],
        compiler_params=_cparams(("parallel",)),
        name="mixer_proj",
    )(x, gain, w, cos, s1, s2)


DSA_TQ = 256
DSA_CK = 512
SEL_ROWS = 128
INT_MIN = -2 ** 31
KEY_NEG_INF = int(np.array(-np.inf, np.float32).view(np.int32)) ^ 0x7FFFFFFF


def _dsa_kernel(qa_ref, qi_ref, kk_ref, vv_ref, sc_ref, o_ref,
                key_ref, bias_ref, qs_ref, m_ref, l_ref, alpha_ref, acc_ref, p_ref, *, top_k, ck):
    i = pl.program_id(1)
    tq = qa_ref.shape[1]
    nck = (i * tq + tq + ck - 1) // ck
    lane = lax.broadcasted_iota(jnp.int32, (tq, LANES), 1)
    lo_half = lane < HEAD_DIM
    scale = HEAD_DIM ** -0.5
    n_att = A_HEADS * tq

    def put(h, t):
        qs_ref[h * tq:(h + 1) * tq, :] = (t * scale).astype(BF16)

    for j in range(A_HEADS // 2):
        t = qa_ref[0, :, j * LANES:(j + 1) * LANES].astype(F32)
        put(2 * j, jnp.where(lo_half, t, 0.0))
        put(2 * j + 1, jnp.where(lo_half, pltpu.roll(t, HEAD_DIM, 1), 0.0))
    for j in range(IDX_HEADS // 2):
        t = qi_ref[0, :, j * LANES:(j + 1) * LANES].astype(F32)
        put(A_HEADS + 2 * j, jnp.where(lo_half, 0.0, pltpu.roll(t, HEAD_DIM, 1)))
        put(A_HEADS + 2 * j + 1, jnp.where(lo_half, 0.0, t))

    w_idx = sc_ref[0, :, SC_IW:SC_IW + IDX_HEADS] * (IDX_HEADS ** -0.5)
    row_pos = i * tq + lax.broadcasted_iota(jnp.int32, (tq, ck), 0)
    col_iota = lax.broadcasted_iota(jnp.int32, (tq, ck), 1)

    def key_rows(c):
        return pl.ds(pl.multiple_of(c * ck, ck), ck)

    def score_chunk(c, _):
        rel = jnp.maximum(_dot_nt(qs_ref[n_att:, :], kk_ref[0, key_rows(c), :]), 0.0)
        score = rel[0:tq] * w_idx[:, 0:1]
        for h in range(1, IDX_HEADS):
            score = score + rel[h * tq:(h + 1) * tq] * w_idx[:, h:h + 1]
        score = jnp.where(score == 0.0, 0.0, score)
        score = jnp.where(col_iota + c * ck <= row_pos, score, -jnp.inf)
        bits = lax.bitcast_convert_type(score, jnp.int32)
        key_ref[c] = jnp.where(bits < 0, bits ^ 0x7FFFFFFF, bits)
        return 0

    lax.fori_loop(0, nck, score_chunk, 0)

    groups = tq // SEL_ROWS

    def count(preds):
        accs = []
        for g, pred in enumerate(preds):
            def body(c, acc, g=g, pred=pred):
                kc = key_ref[c, g * SEL_ROWS:(g + 1) * SEL_ROWS, :]
                for j in range(ck // LANES):
                    acc = acc + jnp.where(pred(kc[:, j * LANES:(j + 1) * LANES]), 1.0, 0.0)
                return acc
            accs.append(lax.fori_loop(0, nck, body, jnp.zeros((SEL_ROWS, LANES), F32)))
        return [jnp.sum(acc, axis=-1, keepdims=True) for acc in accs]

    kf = float(top_k)

    def bit_step(b, thrs):
        cands = [t + lax.shift_left(jnp.int32(1), 31 - b) for t in thrs]
        cnts = count([lambda kc, cand=cand: kc >= cand for cand in cands])
        return tuple(jnp.where(cnt >= kf, cand, t) for cnt, cand, t in zip(cnts, cands, thrs))

    thrs = lax.fori_loop(0, 32, bit_step,
                         tuple(jnp.full((SEL_ROWS, 1), INT_MIN, jnp.int32) for _ in range(groups)))
    thrs = [jnp.maximum(t, KEY_NEG_INF + 1) for t in thrs]
    cnt_ge = count([lambda kc, t=t: kc >= t for t in thrs])
    thr = jnp.concatenate(thrs, axis=0) if groups > 1 else thrs[0]

    def write_bias(c, _):
        bias_ref[c] = jnp.where(key_ref[c] >= thr, 0.0, NEG_MASK)
        return 0

    lax.fori_loop(0, nck, write_bias, 0)

    excess = cnt_ge[0]
    for extra in cnt_ge[1:]:
        excess = jnp.maximum(excess, extra)

    @pl.when(jnp.max(excess) > kf)
    def _():
        cnt_gt = count([lambda kc, t=t: kc > t for t in thrs])
        need = kf - (jnp.concatenate(cnt_gt, axis=0) if groups > 1 else cnt_gt[0])
        r = lax.broadcasted_iota(jnp.int32, (ck, ck), 0)
        cidx = lax.broadcasted_iota(jnp.int32, (ck, ck), 1)
        upper = jnp.where(r <= cidx, 1.0, 0.0).astype(BF16)

        def tie_chunk(c, before):
            kc = key_ref[c]
            tie = kc == thr
            rank = before + _dot(jnp.where(tie, 1.0, 0.0).astype(BF16), upper)
            keep = (kc > thr) | (tie & (rank <= need))
            bias_ref[c] = jnp.where(keep, 0.0, NEG_MASK)
            return rank[:, ck - 1:ck]

        lax.fori_loop(0, nck, tie_chunk, jnp.zeros((tq, 1), F32))

    m_ref[...] = jnp.full(m_ref.shape, NEG_BIG, F32)
    l_ref[...] = jnp.zeros(l_ref.shape, F32)
    acc_ref[...] = jnp.zeros(acc_ref.shape, F32)

    def attend(c, _):
        s_all = _dot_nt(qs_ref[0:n_att, :], kk_ref[0, key_rows(c), :])
        bias = bias_ref[c]
        for h in range(A_HEADS):
            r = slice(h * tq, (h + 1) * tq)
            s = s_all[r] + bias
            m_old = m_ref[r]
            m_new = jnp.maximum(m_old, jnp.max(s, axis=-1, keepdims=True))
            alpha = jnp.exp(m_old - m_new)
            p = jnp.exp(s - jnp.tile(m_new, (1, ck // LANES)))
            l_ref[r] = alpha * l_ref[r] + jnp.sum(p, axis=-1, keepdims=True)
            m_ref[r] = m_new
            alpha_ref[r] = alpha
            p_ref[r] = p.astype(BF16)
        acc_ref[...] = alpha_ref[...] * acc_ref[...] + _dot(p_ref[...], vv_ref[0, key_rows(c), :])
        return 0

    lax.fori_loop(0, nck, attend, 0)

    out = acc_ref[...] / l_ref[...]
    for j in range(A_HEADS // 2):
        even = out[(2 * j) * tq:(2 * j + 1) * tq]
        odd = out[(2 * j + 1) * tq:(2 * j + 2) * tq]
        o_ref[0, :, j * LANES:(j + 1) * LANES] = (
            even + pltpu.roll(odd, HEAD_DIM, 1)).astype(o_ref.dtype)


def _dsa(a_grp, sc, top_k):
    b, s, _ = a_grp.shape
    tq = min(DSA_TQ, s)
    ck = min(DSA_CK, s)
    return pl.pallas_call(
        functools.partial(_dsa_kernel, top_k=top_k, ck=ck),
        grid=(b, s // tq),
        in_specs=[
            pl.BlockSpec((1, tq, 512), lambda bi, i: (bi, i, 0)),
            pl.BlockSpec((1, tq, 256), lambda bi, i: (bi, i, 2)),
            pl.BlockSpec((1, s, LANES), lambda bi, i: (bi, 0, 6)),
            pl.BlockSpec((1, s, LANES), lambda bi, i: (bi, 0, 7)),
            pl.BlockSpec((1, tq, LANES), lambda bi, i: (bi, i, 0)),
        ],
        out_specs=pl.BlockSpec((1, tq, BRANCH), lambda bi, i: (bi, i, 0)),
        out_shape=jax.ShapeDtypeStruct((b, s, BRANCH), BF16),
        scratch_shapes=[
            pltpu.VMEM((s // ck, tq, ck), jnp.int32),
            pltpu.VMEM((s // ck, tq, ck), F32),
            pltpu.VMEM(((A_HEADS + IDX_HEADS) * tq, LANES), BF16),
            pltpu.VMEM((A_HEADS * tq, LANES), F32),
            pltpu.VMEM((A_HEADS * tq, LANES), F32),
            pltpu.VMEM((A_HEADS * tq, LANES), F32),
            pltpu.VMEM((A_HEADS * tq, LANES), F32),
            pltpu.VMEM((A_HEADS * tq, ck), BF16),
        ],
        compiler_params=_cparams(("parallel", "arbitrary")),
        name="dsa_attention",
    )(a_grp, a_grp, a_grp, a_grp, sc)


SLAB = 64


def _dsa_t_kernel(qa_ref, qi_ref, kk_ref, vv_ref, sc_ref, o_ref,
                  vt_ref, key_ref, bias_ref, qs_ref, m_ref, l_ref, alpha_ref, acc_ref,
                  s_ref, p_ref, *, top_k, ck):
    i = pl.program_id(1)
    tq = qa_ref.shape[1]
    n_chunks_total = key_ref.shape[0]
    nck = (i * tq + tq + ck - 1) // ck
    lane = lax.broadcasted_iota(jnp.int32, (tq, LANES), 1)
    lo_half = lane < HEAD_DIM
    scale = HEAD_DIM ** -0.5

    @pl.when(i == 0)
    def _():
        for c in range(n_chunks_total):
            vt_ref[c] = vv_ref[0, c * ck:(c + 1) * ck, :].astype(F32).T.astype(BF16)

    def put(h, t):
        qs_ref[h * tq:(h + 1) * tq, :] = (t * scale).astype(BF16)

    for j in range(A_HEADS // 2):
        t = qa_ref[0, :, j * LANES:(j + 1) * LANES].astype(F32)
        put(2 * j, jnp.where(lo_half, t, 0.0))
        put(2 * j + 1, jnp.where(lo_half, pltpu.roll(t, HEAD_DIM, 1), 0.0))
    for j in range(IDX_HEADS // 2):
        t = qi_ref[0, :, j * LANES:(j + 1) * LANES].astype(F32)
        put(A_HEADS + 2 * j, jnp.where(lo_half, 0.0, pltpu.roll(t, HEAD_DIM, 1)))
        put(A_HEADS + 2 * j + 1, jnp.where(lo_half, 0.0, t))

    w_rows = (sc_ref[0] * (IDX_HEADS ** -0.5)).T
    key_pos = lax.broadcasted_iota(jnp.int32, (ck, tq), 0)
    qry_pos = i * tq + lax.broadcasted_iota(jnp.int32, (ck, tq), 1)

    def key_rows(c):
        return pl.ds(pl.multiple_of(c * ck, ck), ck)

    def score_chunk(c, _):
        kc = kk_ref[0, key_rows(c), :]
        score = None
        for h in range(IDX_HEADS):
            rel = jnp.maximum(
                _dot_nt(kc, qs_ref[(A_HEADS + h) * tq:(A_HEADS + h + 1) * tq, :]), 0.0)
            term = rel * w_rows[SC_IW + h:SC_IW + h + 1, :]
            score = term if score is None else score + term
        score = jnp.where(score == 0.0, 0.0, score)
        score = jnp.where(key_pos + c * ck <= qry_pos, score, -jnp.inf)
        bits = lax.bitcast_convert_type(score, jnp.int32)
        key_ref[c] = jnp.where(bits < 0, bits ^ 0x7FFFFFFF, bits)
        return 0

    lax.fori_loop(0, nck, score_chunk, 0)

    def count(pred):
        def body(c, acc):
            for r0 in range(0, ck, SLAB):
                acc = acc + jnp.where(pred(key_ref[c, r0:r0 + SLAB, :]), 1.0, 0.0)
            return acc
        acc = lax.fori_loop(0, nck, body, jnp.zeros((SLAB, tq), F32))
        return jnp.sum(acc, axis=0, keepdims=True)

    kf = float(top_k)

    def bit_step(b, thr):
        cand = thr + lax.shift_left(jnp.int32(1), 31 - b)
        cand_slab = jnp.broadcast_to(cand, (SLAB, tq))
        cnt = count(lambda kc: kc >= cand_slab)
        return jnp.where(cnt >= kf, cand, thr)

    thr = lax.fori_loop(0, 32, bit_step, jnp.full((1, tq), INT_MIN, jnp.int32))
    thr = jnp.maximum(thr, KEY_NEG_INF + 1)
    thr_slab = jnp.broadcast_to(thr, (SLAB, tq))
    cnt_ge = count(lambda kc: kc >= thr_slab)

    def write_bias(c, _):
        bias_ref[c] = jnp.where(key_ref[c] >= thr, 0.0, NEG_MASK)
        return 0

    lax.fori_loop(0, nck, write_bias, 0)

    @pl.when(jnp.max(cnt_ge) > kf)
    def _():
        need = kf - count(lambda kc: kc > thr_slab)
        r = lax.broadcasted_iota(jnp.int32, (ck, ck), 0)
        cidx = lax.broadcasted_iota(jnp.int32, (ck, ck), 1)
        lower = jnp.where(cidx <= r, 1.0, 0.0).astype(BF16)

        def tie_chunk(c, before):
            kc = key_ref[c]
            tie = kc == thr
            rank = before + _dot(lower, jnp.where(tie, 1.0, 0.0).astype(BF16))
            keep = (kc > thr) | (tie & (rank <= need))
            bias_ref[c] = jnp.where(keep, 0.0, NEG_MASK)
            return rank[ck - 1:ck, :]

        lax.fori_loop(0, nck, tie_chunk, jnp.zeros((1, tq), F32))

    m_ref[...] = jnp.full(m_ref.shape, NEG_BIG, F32)
    l_ref[...] = jnp.zeros(l_ref.shape, F32)
    acc_ref[...] = jnp.zeros(acc_ref.shape, F32)
    reps = ck // 8

    n_att = A_HEADS * tq

    def attend(c, _):
        s_ref[...] = _dot_nt(kk_ref[0, key_rows(c), :], qs_ref[0:n_att, :])
        bias = bias_ref[c]
        for h in range(A_HEADS):
            cols = slice(h * tq, (h + 1) * tq)
            s = s_ref[:, cols] + bias
            m_old = m_ref[:, cols]
            m_new = jnp.maximum(m_old, jnp.max(s, axis=0, keepdims=True))
            alpha = jnp.exp(m_old - m_new)
            p = jnp.exp(s - jnp.tile(m_new, (reps, 1)))
            l_ref[:, cols] = alpha * l_ref[:, cols] + jnp.sum(p, axis=0, keepdims=True)
            m_ref[:, cols] = m_new
            alpha_ref[:, cols] = alpha
            p_ref[:, cols] = p.astype(BF16)
        acc_ref[...] = (jnp.tile(alpha_ref[...], (LANES // 8, 1)) * acc_ref[...]
                        + _dot(vt_ref[c], p_ref[...]))
        return 0

    lax.fori_loop(0, nck, attend, 0)

    for j in range(A_HEADS // 2):
        outs = []
        for h in (2 * j, 2 * j + 1):
            cols = slice(h * tq, (h + 1) * tq)
            out_t = acc_ref[:, cols] / jnp.tile(l_ref[:, cols], (LANES // 8, 1))
            outs.append(out_t.T)
        o_ref[0, :, j * LANES:(j + 1) * LANES] = (
            outs[0] + pltpu.roll(outs[1], HEAD_DIM, 1)).astype(o_ref.dtype)


def _dsa_t(a_grp, sc, top_k):
    b, s, _ = a_grp.shape
    tq = min(DSA_TQ, s)
    ck = min(DSA_CK, s)
    return pl.pallas_call(
        functools.partial(_dsa_t_kernel, top_k=top_k, ck=ck),
        grid=(b, s // tq),
        in_specs=[
            pl.BlockSpec((1, tq, 512), lambda bi, i: (bi, i, 0)),
            pl.BlockSpec((1, tq, 256), lambda bi, i: (bi, i, 2)),
            pl.BlockSpec((1, s, LANES), lambda bi, i: (bi, 0, 6)),
            pl.BlockSpec((1, s, LANES), lambda bi, i: (bi, 0, 7)),
            pl.BlockSpec((1, tq, LANES), lambda bi, i: (bi, i, 0)),
        ],
        out_specs=pl.BlockSpec((1, tq, BRANCH), lambda bi, i: (bi, i, 0)),
        out_shape=jax.ShapeDtypeStruct((b, s, BRANCH), BF16),
        scratch_shapes=[
            pltpu.VMEM((s // ck, LANES, ck), BF16),
            pltpu.VMEM((s // ck, ck, tq), jnp.int32),
            pltpu.VMEM((s // ck, ck, tq), F32),
            pltpu.VMEM(((A_HEADS + IDX_HEADS) * tq, LANES), BF16),
            pltpu.VMEM((8, A_HEADS * tq), F32),
            pltpu.VMEM((8, A_HEADS * tq), F32),
            pltpu.VMEM((8, A_HEADS * tq), F32),
            pltpu.VMEM((LANES, A_HEADS * tq), F32),
            pltpu.VMEM((ck, A_HEADS * tq), F32),
            pltpu.VMEM((ck, A_HEADS * tq), BF16),
        ],
        compiler_params=_cparams(("parallel", "arbitrary")),
        name="dsa_attention",
    )(a_grp, a_grp, a_grp, a_grp, sc)


CUM_CHUNK = 256


def _fox_gate_kernel(sc_ref, bf_ref, o_ref, *, chunk):
    s = sc_ref.shape[1]
    r = lax.broadcasted_iota(jnp.int32, (chunk, chunk), 0)
    c = lax.broadcasted_iota(jnp.int32, (chunk, chunk), 1)
    lower = jnp.where(c <= r, 1.0, 0.0)
    carry = jnp.zeros((1, LANES), F32)
    for n in range(s // chunk):
        x = sc_ref[0, n * chunk:(n + 1) * chunk, :] + bf_ref[...]
        log_f = jnp.minimum(x, 0.0) - jnp.log1p(jnp.exp(-jnp.abs(x)))
        cum = _dot_hi(lower, log_f) + carry
        carry = cum[chunk - 1:chunk, :]
        o_ref[0, :, n * chunk:(n + 1) * chunk] = -(cum.T[:B_HEADS, :])


def _fox_gate(sc, bf_row):
    b, s, _ = sc.shape
    chunk = min(CUM_CHUNK, s)
    return pl.pallas_call(
        functools.partial(_fox_gate_kernel, chunk=chunk),
        grid=(b,),
        in_specs=[pl.BlockSpec((1, s, LANES), lambda bi: (bi, 0, 0)), _resident((1, LANES))],
        out_specs=pl.BlockSpec((1, B_HEADS, s), lambda bi: (bi, 0, 0)),
        out_shape=jax.ShapeDtypeStruct((b, B_HEADS, s), F32),
        compiler_params=_cparams(("parallel",)),
        name="fox_gate_cumsum",
    )(sc, bf_row)


FOX_T = 256


def _fox_kernel(q_ref, k_ref, v_ref, nc_ref, o_ref,
                qz_ref, m_ref, l_ref, alpha_ref, acc_ref, s_ref, p_ref, *, t):
    i = pl.program_id(1)
    pairs = B_HEADS // 2
    lane = lax.broadcasted_iota(jnp.int32, (t, LANES), 1)
    lo_half = lane < HEAD_DIM
    for hp in range(pairs):
        q = q_ref[0, :, hp * LANES:(hp + 1) * LANES].astype(F32) * (HEAD_DIM ** -0.5)
        qz_ref[(2 * hp) * t:(2 * hp + 1) * t, :] = jnp.where(lo_half, q, 0.0).astype(BF16)
        qz_ref[(2 * hp + 1) * t:(2 * hp + 2) * t, :] = jnp.where(lo_half, 0.0, q).astype(BF16)
    causal = (lax.broadcasted_iota(jnp.int32, (t, t), 1)
              <= lax.broadcasted_iota(jnp.int32, (t, t), 0))
    m_ref[...] = jnp.full(m_ref.shape, NEG_BIG, F32)
    l_ref[...] = jnp.zeros(l_ref.shape, F32)
    acc_ref[...] = jnp.zeros(acc_ref.shape, F32)

    def attend(c, _, masked):
        rows = pl.ds(pl.multiple_of(c * t, t), t)
        for hp in range(pairs):
            r2 = slice(2 * hp * t, (2 * hp + 2) * t)
            s_ref[r2, :] = _dot_nt(qz_ref[r2, :], k_ref[0, rows, hp * LANES:(hp + 1) * LANES])
        for h in range(B_HEADS):
            r = slice(h * t, (h + 1) * t)
            s = s_ref[r, :] + nc_ref[0, h, pl.ds(c, 1), :]
            if masked:
                s = jnp.where(causal, s, -jnp.inf)
            m_old = m_ref[r]
            m_new = jnp.maximum(m_old, jnp.max(s, axis=-1, keepdims=True))
            alpha = jnp.exp(m_old - m_new)
            p = jnp.exp(s - jnp.tile(m_new, (1, t // LANES)))
            l_ref[r] = alpha * l_ref[r] + jnp.sum(p, axis=-1, keepdims=True)
            m_ref[r] = m_new
            alpha_ref[r] = alpha
            p_ref[r, :] = p.astype(BF16)
        for hp in range(pairs):
            r2 = slice(2 * hp * t, (2 * hp + 2) * t)
            acc_ref[r2] = alpha_ref[r2] * acc_ref[r2] + _dot(
                p_ref[r2, :], v_ref[0, rows, hp * LANES:(hp + 1) * LANES])
        return 0

    lax.fori_loop(0, i, functools.partial(attend, masked=False), 0)
    attend(i, 0, masked=True)
    for hp in range(pairs):
        even = slice(2 * hp * t, (2 * hp + 1) * t)
        odd = slice((2 * hp + 1) * t, (2 * hp + 2) * t)
        out = jnp.where(lo_half, acc_ref[even] / l_ref[even], acc_ref[odd] / l_ref[odd])
        o_ref[0, :, hp * LANES:(hp + 1) * LANES] = out.astype(o_ref.dtype)


def _fox(b_grp, neg_c):
    b, s, _ = b_grp.shape
    t = min(FOX_T, s)
    pairs = B_HEADS // 2
    neg_c = neg_c.reshape(b, B_HEADS, s // t, t)
    return pl.pallas_call(
        functools.partial(_fox_kernel, t=t),
        grid=(b, s // t),
        in_specs=[
            pl.BlockSpec((1, t, BRANCH), lambda bi, i: (bi, i, 0)),
            pl.BlockSpec((1, s, BRANCH), lambda bi, i: (bi, 0, 1)),
            pl.BlockSpec((1, s, BRANCH), lambda bi, i: (bi, 0, 2)),
            pl.BlockSpec((1, B_HEADS, s // t, t), lambda bi, i: (bi, 0, 0, 0)),
        ],
        out_specs=pl.BlockSpec((1, t, BRANCH), lambda bi, i: (bi, i, 0)),
        out_shape=jax.ShapeDtypeStruct((b, s, BRANCH), BF16),
        scratch_shapes=[
            pltpu.VMEM((B_HEADS * t, LANES), BF16),
            pltpu.VMEM((B_HEADS * t, LANES), F32),
            pltpu.VMEM((B_HEADS * t, LANES), F32),
            pltpu.VMEM((B_HEADS * t, LANES), F32),
            pltpu.VMEM((B_HEADS * t, LANES), F32),
            pltpu.VMEM((B_HEADS * t, t), F32),
            pltpu.VMEM((B_HEADS * t, t), BF16),
        ],
        compiler_params=_cparams(("parallel", "arbitrary")),
        name="fox_attention",
    )(b_grp, b_grp, b_grp, neg_c)


GDN_TS = 512
GDN_C = 128
TAIL = 8
NEUMANN_ROUNDS = 6


def _gdn_kernel(x_ref, z_ref, sc_ref, cw_ref, alog_ref, dtb_ref, dn_ref, o_ref,
                state_ref, tail_ref, ext_ref, act_ref, np_ref, rhs_ref,
                qk_ref, qg_ref, kdt_ref, dec_ref, *, ts):
    j = pl.program_id(1)
    width = x_ref.shape[2]
    n_chunks = ts // GDN_C
    chains = [(h, n) for n in range(n_chunks) for h in range(C_HEADS)]

    @pl.when(j == 0)
    def _():
        state_ref[...] = jnp.zeros_like(state_ref)
        tail_ref[...] = jnp.zeros_like(tail_ref)

    x = x_ref[0]
    ext_ref[0:TAIL, :] = tail_ref[...]
    ext_ref[TAIL:TAIL + ts, :] = x
    tail_ref[...] = x[ts - TAIL:ts, :]
    conv = jnp.zeros((ts, width), F32)
    for tap in range(CONV_WIDTH):
        start = TAIL - (CONV_WIDTH - 1) + tap
        conv = conv + cw_ref[tap:tap + 1, :] * ext_ref[start:start + ts, :]
    act_ref[...] = _silu(conv)

    sc = sc_ref[0]
    beta_all = jax.nn.sigmoid(sc)
    xg = sc + dtb_ref[...]
    softplus = jnp.maximum(xg, 0.0) + jnp.log1p(jnp.exp(-jnp.abs(xg)))
    g_all = -jnp.exp(alog_ref[...]) * softplus
    r = lax.broadcasted_iota(jnp.int32, (ts, ts), 0)
    c = lax.broadcasted_iota(jnp.int32, (ts, ts), 1)
    same_chunk_lower = jnp.where((c <= r) & (r // GDN_C == c // GDN_C), 1.0, 0.0)
    gc_all = _dot_hi(same_chunk_lower, g_all)
    gc_t = gc_all.T

    ri = lax.broadcasted_iota(jnp.int32, (GDN_C, GDN_C), 0)
    ci = lax.broadcasted_iota(jnp.int32, (GDN_C, GDN_C), 1)
    incl = ri >= ci
    strict = ri > ci

    def l2n(t):
        return t * lax.rsqrt(jnp.sum(t * t, axis=-1, keepdims=True) + L2_EPS)

    for idx, (h, n) in enumerate(chains):
        rows = slice(n * GDN_C, (n + 1) * GDN_C)
        q = l2n(act_ref[rows, h * C_DIM:(h + 1) * C_DIM]) * (C_DIM ** -0.5)
        k = l2n(act_ref[rows, BRANCH + h * C_DIM:BRANCH + (h + 1) * C_DIM])
        v = act_ref[rows, 2 * BRANCH + h * C_DIM:2 * BRANCH + (h + 1) * C_DIM]
        beta = beta_all[rows, SC_BETA + h:SC_BETA + h + 1]
        gcol = gc_all[rows, SC_DECAY + h:SC_DECAY + h + 1]
        grow = gc_t[SC_DECAY + h:SC_DECAY + h + 1, n * GDN_C:(n + 1) * GDN_C]
        g_last = gcol[GDN_C - 1:GDN_C, :]
        decay = jnp.exp(jnp.where(incl, gcol - grow, -jnp.inf))
        kb = k * beta
        k16 = k.astype(BF16)
        q16 = q.astype(BF16)
        a_mat = jnp.where(strict, _dot_nt(kb.astype(BF16), k16) * decay, 0.0)
        np_ref[idx, :, 0:GDN_C] = -a_mat
        np_ref[idx, :, GDN_C:2 * GDN_C] = _dot_solve(a_mat, a_mat)
        rhs_ref[idx, :, 0:C_DIM] = v * beta
        rhs_ref[idx, :, C_DIM:2 * C_DIM] = kb * jnp.exp(gcol)
        qk_ref[idx] = jnp.where(incl, _dot_nt(q16, k16) * decay, 0.0).astype(BF16)
        qg_ref[idx] = (q * jnp.exp(gcol)).astype(BF16)
        kdt_ref[idx] = (k * jnp.exp(g_last - gcol)).T.astype(BF16)
        dec_ref[idx] = jnp.broadcast_to(jnp.exp(g_last), (1, C_DIM))

    for rnd in range(1, NEUMANN_ROUNDS + 1):
        last = rnd == NEUMANN_ROUNDS
        for idx in range(len(chains)):
            n_old = np_ref[idx, :, 0:GDN_C]
            p_old = np_ref[idx, :, GDN_C:2 * GDN_C]
            if last:
                np_ref[idx, :, 0:GDN_C] = n_old + p_old + _dot_solve(p_old, n_old)
            else:
                prod = _dot_solve(p_old, np_ref[idx])
                np_ref[idx, :, 0:GDN_C] = n_old + p_old + prod[:, 0:GDN_C]
                np_ref[idx, :, GDN_C:2 * GDN_C] = prod[:, GDN_C:2 * GDN_C]

    for idx in range(len(chains)):
        rhs = rhs_ref[idx]
        rhs_ref[idx] = rhs + _dot_solve(np_ref[idx, :, 0:GDN_C], rhs)

    for idx, (h, n) in enumerate(chains):
        rows = slice(n * GDN_C, (n + 1) * GDN_C)
        state = state_ref[h]
        s16 = state.astype(BF16)
        v_new = rhs_ref[idx, :, 0:C_DIM] - _dot(rhs_ref[idx, :, C_DIM:2 * C_DIM].astype(BF16), s16)
        vn16 = v_new.astype(BF16)
        o = _dot(qg_ref[idx], s16) + _dot(qk_ref[idx], vn16)
        state_ref[h] = state * dec_ref[idx] + _dot(kdt_ref[idx], vn16)
        z = z_ref[0, rows, h * C_DIM:(h + 1) * C_DIM]
        y = _rms(o, dn_ref[...]) * _silu(z)
        o_ref[0, rows, h * C_DIM:(h + 1) * C_DIM] = y.astype(o_ref.dtype)


def _gdn(cqkv, cz, sc, conv_w, alog_row, dtb_row, dn_row):
    b, s, width = cqkv.shape
    ts = min(GDN_TS, s)
    nch = C_HEADS * (ts // GDN_C)
    blk = lambda bi, j: (bi, j, 0)
    return pl.pallas_call(
        functools.partial(_gdn_kernel, ts=ts),
        grid=(b, s // ts),
        in_specs=[
            pl.BlockSpec((1, ts, width), blk),
            pl.BlockSpec((1, ts, BRANCH), blk),
            pl.BlockSpec((1, ts, LANES), blk),
            _resident((CONV_WIDTH, width)),
            _resident((1, LANES)),
            _resident((1, LANES)),
            _resident((1, C_DIM)),
        ],
        out_specs=pl.BlockSpec((1, ts, BRANCH), blk),
        out_shape=jax.ShapeDtypeStruct((b, s, BRANCH), BF16),
        scratch_shapes=[
            pltpu.VMEM((C_HEADS, C_DIM, C_DIM), F32),
            pltpu.VMEM((TAIL, width), F32),
            pltpu.VMEM((TAIL + ts, width), F32),
            pltpu.VMEM((ts, width), F32),
            pltpu.VMEM((nch, GDN_C, 2 * GDN_C), F32),
            pltpu.VMEM((nch, GDN_C, 2 * C_DIM), F32),
            pltpu.VMEM((nch, GDN_C, GDN_C), BF16),
            pltpu.VMEM((nch, GDN_C, C_DIM), BF16),
            pltpu.VMEM((nch, C_DIM, GDN_C), BF16),
            pltpu.VMEM((nch, 1, C_DIM), F32),
        ],
        compiler_params=_cparams(("parallel", "arbitrary")),
        name="gated_deltanet",
    )(cqkv, cz, sc, conv_w, alog_row, dtb_row, dn_row)


MERGE_TM = 512


def _merge_kernel(x_ref, ya_ref, yb_ref, yc_ref, gl_ref, bg_ref,
                  wa_ref, wb_ref, wc_ref, wo_ref, o_ref):
    merged = None
    for n, (y_ref, w_ref) in enumerate(((ya_ref, wa_ref), (yb_ref, wb_ref), (yc_ref, wc_ref))):
        cols = slice(n * D_MODEL, (n + 1) * D_MODEL)
        gate = jax.nn.sigmoid(gl_ref[:, cols] + bg_ref[:, cols])
        term = gate * _dot(y_ref[...], w_ref[...])
        merged = term if merged is None else merged + term
    o_ref[...] = x_ref[...] + _dot(merged.astype(BF16), wo_ref[...])


def _merge(x, ya, yb, yc, gl, b_gate, wa, wb, wc, wo):
    n = x.shape[0]
    tm = min(MERGE_TM, n)
    row = lambda i: (i, 0)
    return pl.pallas_call(
        _merge_kernel,
        grid=(n // tm,),
        in_specs=[
            pl.BlockSpec((tm, D_MODEL), row),
            pl.BlockSpec((tm, BRANCH), row),
            pl.BlockSpec((tm, BRANCH), row),
            pl.BlockSpec((tm, BRANCH), row),
            pl.BlockSpec((tm, 3 * D_MODEL), row),
            _resident((1, 3 * D_MODEL)),
            _resident((BRANCH, D_MODEL)),
            _resident((BRANCH, D_MODEL)),
            _resident((BRANCH, D_MODEL)),
            _resident((D_MODEL, D_MODEL)),
        ],
        out_specs=pl.BlockSpec((tm, D_MODEL), row),
        out_shape=jax.ShapeDtypeStruct((n, D_MODEL), F32),
        compiler_params=_cparams(("parallel",)),
        name="gated_merge",
    )(x, ya, yb, yc, gl, b_gate, wa, wb, wc, wo)


def _w_in_pieces():
    offs = np.concatenate([[0], np.cumsum(IN_SIZES)])
    names = ("a_q", "a_k", "a_v", "i_q", "i_k", "i_w", "b_qkv", "b_f", "c_qkv", "c_z",
             "c_beta", "c_a", "gates")
    src = {n: (int(offs[i]), int(IN_SIZES[i])) for i, n in enumerate(names)}
    order = (("a_q", 0), ("i_q", 512), ("a_k", 768), ("i_k", 832), ("a_v", 896),
             ("b_f", W_A + SC_BF), ("i_w", W_A + SC_IW), ("c_beta", W_A + SC_BETA),
             ("c_a", W_A + SC_DECAY), ("b_qkv", W_A + W_SC), ("c_qkv", W_A + W_SC + W_B),
             ("c_z", W_A + W_SC + W_B + W_CQKV), ("gates", W_A + W_SC + W_B + W_CQKV + W_CZ))
    pieces = []
    for name, dst in order:
        s0, width = src[name]
        done = 0
        while done < width:
            step = min(width - done, LANES - (dst + done) % LANES)
            pieces.append((dst + done, s0 + done, step))
            done += step
    return pieces


RELAYOUT_TM = 256


def _relayout_kernel(w_ref, tail_ref, o_ref, *, aligned_cols):
    by_tile = {}
    for dst, src, n in _w_in_pieces():
        by_tile.setdefault(dst // LANES, []).append((dst % LANES, src, n))
    row = lax.broadcasted_iota(jnp.int32, (2 * LANES, LANES), 0)
    col = lax.broadcasted_iota(jnp.int32, (2 * LANES, LANES), 1)

    def src_tile(start):
        if start + LANES <= aligned_cols:
            return w_ref[0, :, start:start + LANES]
        return tail_ref[0, :, start - aligned_cols:start - aligned_cols + LANES]

    for tile in range(W_TOTAL // LANES):
        acc = None
        for dst_lane, src, n in by_tile.get(tile, ()):
            base = (src // LANES) * LANES
            shift = src - base - dst_lane
            select = (row == col + shift) & (col >= dst_lane) & (col < dst_lane + n)
            window = jnp.concatenate([src_tile(base), src_tile(base + LANES)], axis=1)
            part = _dot(window.astype(BF16), jnp.where(select, 1.0, 0.0).astype(BF16))
            acc = part if acc is None else acc + part
        if acc is None:
            acc = jnp.zeros((w_ref.shape[1], LANES), F32)
        o_ref[0, :, tile * LANES:(tile + 1) * LANES] = acc.astype(BF16)


def _layout_w_in(w_in):
    depth, rows, cols = w_in.shape
    aligned_cols = (cols // LANES) * LANES
    tail = jnp.pad(w_in[:, :, aligned_cols:], ((0, 0), (0, 0), (0, 2 * LANES - (cols - aligned_cols))))
    tm = min(RELAYOUT_TM, rows)
    return pl.pallas_call(
        functools.partial(_relayout_kernel, aligned_cols=aligned_cols),
        grid=(depth, rows // tm),
        in_specs=[pl.BlockSpec((1, tm, cols), lambda l, i: (l, i, 0)),
                  pl.BlockSpec((1, tm, 2 * LANES), lambda l, i: (l, i, 0))],
        out_specs=pl.BlockSpec((1, tm, W_TOTAL), lambda l, i: (l, i, 0)),
        out_shape=jax.ShapeDtypeStruct((depth, rows, W_TOTAL), BF16),
        compiler_params=_cparams(("parallel", "parallel")),
        name="w_in_relayout",
    )(w_in, tail)


def _rotary_tables(seq):
    pos = jnp.arange(seq, dtype=F32)
    inv_freq = jnp.power(ROPE_THETA, -jnp.arange(0, ROT_DIM, 2, dtype=F32) / ROT_DIM)
    ang = pos[:, None] * inv_freq[None, :]
    cos, sin = jnp.cos(ang), jnp.sin(ang)
    half = ROT_DIM // 2
    ones = jnp.ones((seq, HEAD_DIM - ROT_DIM), F32)
    zeros_h = jnp.zeros((seq, half), F32)
    zeros_r = jnp.zeros((seq, HEAD_DIM - ROT_DIM), F32)
    c64 = jnp.concatenate([cos, cos, ones], axis=1)
    s1_64 = jnp.concatenate([zeros_h, sin, zeros_r], axis=1)
    s2_64 = jnp.concatenate([-sin, zeros_h, zeros_r], axis=1)
    twice = lambda t: jnp.concatenate([t, t], axis=1)
    return twice(c64), twice(s1_64), twice(s2_64)


def _lane_row(values, start):
    return jnp.zeros((1, LANES), F32).at[0, start:start + values.shape[0]].set(values.astype(F32))


def kernel(x, ffn1_norm, ffn1_w_in, ffn1_w_out, mix_norm, w_in, b_gate, b_forget, conv_w, a_log, dt_bias, delta_norm, w_branch_a, w_branch_b, w_branch_c, w_out, ffn2_norm, ffn2_w_in, ffn2_w_out, final_norm):
    batch, seq, _ = x.shape
    depth = w_in.shape[0]
    top_k = min(INDEX_TOPK, seq // 4)
    cos, s1, s2 = _rotary_tables(seq)
    w_mix = _layout_w_in(w_in)
    final_row = final_norm.reshape(1, D_MODEL)
    xt = x.reshape(batch * seq, D_MODEL)
    for l in range(depth):
        xt = _ffn(xt, ffn1_norm[l].reshape(1, D_MODEL), ffn1_w_in[l].astype(BF16),
                  ffn1_w_out[l].astype(BF16), final_row, False)
        a_grp, sc, b_grp, cqkv, cz, gl = _proj(
            xt, mix_norm[l].reshape(1, D_MODEL), w_mix[l], cos, s1, s2, seq)
        a_grp = a_grp.reshape(batch, seq, W_A)
        sc = sc.reshape(batch, seq, W_SC)
        y_a = _dsa_t(a_grp, sc, top_k)
        neg_c = _fox_gate(sc, _lane_row(b_forget[l], SC_BF))
        y_b = _fox(b_grp.reshape(batch, seq, W_B), neg_c)
        y_c = _gdn(cqkv.reshape(batch, seq, W_CQKV), cz.reshape(batch, seq, W_CZ), sc,
                   conv_w[l], _lane_row(a_log[l], SC_DECAY), _lane_row(dt_bias[l], SC_DECAY),
                   delta_norm[l].reshape(1, C_DIM))
        flat = lambda t: t.reshape(batch * seq, BRANCH)
        xt = _merge(xt, flat(y_a), flat(y_b), flat(y_c), gl, b_gate[l].reshape(1, 3 * D_MODEL),
                    w_branch_a[l].astype(BF16), w_branch_b[l].astype(BF16),
                    w_branch_c[l].astype(BF16), w_out[l].astype(BF16))
        xt = _ffn(xt, ffn2_norm[l].reshape(1, D_MODEL), ffn2_w_in[l].astype(BF16),
                  ffn2_w_out[l].astype(BF16), final_row, l == depth - 1)
    return xt.reshape(batch, seq, D_MODEL)
```

```python
import functools

import numpy as np
import jax
import jax.numpy as jnp
from jax import lax
from jax.experimental import pallas as pl
from jax.experimental.pallas import tpu as pltpu

F32 = jnp.float32
BF16 = jnp.bfloat16

D_MODEL = 1024
BRANCH = 512
A_HEADS = 8
HEAD_DIM = 64
IDX_HEADS = 4
INDEX_TOPK = 256
B_HEADS = 8
C_HEADS = 4
C_DIM = 128
CONV_WIDTH = 4
ROPE_THETA = 500000.0
ROT_DIM = 16
FFN_DIM = 2048
NORM_EPS = 1e-6
L2_EPS = 1e-6
IN_SIZES = (512, 64, 64, 256, 64, 4, 1536, 8, 1536, 512, 4, 4, 3072)

LANES = 128
SC_BF = 0
SC_IW = 8
SC_BETA = 12
SC_DECAY = 16
W_A, W_SC, W_B, W_CQKV, W_CZ, W_G = 1024, 128, 1536, 1536, 512, 3072
W_TOTAL = W_A + W_SC + W_B + W_CQKV + W_CZ + W_G
ROT_TILES = 7

NEG_BIG = -1e30
NEG_MASK = -2e30
VMEM_LIMIT = 56 * 1024 * 1024


def _cparams(sem):
    return pltpu.CompilerParams(dimension_semantics=sem, vmem_limit_bytes=VMEM_LIMIT)


def _dot(a, b):
    return jnp.dot(a, b, preferred_element_type=F32)


def _dot_hi(a, b):
    return jnp.dot(a, b, preferred_element_type=F32, precision=lax.Precision.HIGHEST)


def _dot_solve(a, b):
    a_hi = a.astype(BF16)
    b_hi = b.astype(BF16)
    a_lo = (a - a_hi.astype(F32)).astype(BF16)
    b_lo = (b - b_hi.astype(F32)).astype(BF16)
    m = a.shape[0]
    top = _dot(jnp.concatenate([a_hi, a_lo], axis=0), b_hi)
    return top[0:m] + top[m:2 * m] + _dot(a_hi, b_lo)


def _dot_nt(a, b):
    return lax.dot_general(a, b, (((1,), (1,)), ((), ())), preferred_element_type=F32)


def _rms(x, gain):
    return x * lax.rsqrt(jnp.mean(x * x, axis=-1, keepdims=True) + NORM_EPS) * gain


def _silu(x):
    return x * jax.nn.sigmoid(x)


def _resident(shape):
    nd = len(shape)
    return pl.BlockSpec(shape, lambda *_: (0,) * nd, pipeline_mode=pl.Buffered(1))


FFN_TM = 512
FFN_CHUNK = 512


def _ffn_kernel(x_ref, g_ref, win_ref, wout_ref, fg_ref, o_ref, *, final):
    x = x_ref[...]
    h = _rms(x, g_ref[...]).astype(BF16)
    acc = jnp.zeros(x.shape, F32)
    for c in range(FFN_DIM // FFN_CHUNK):
        lo = c * FFN_CHUNK
        gate = _dot(h, win_ref[:, lo:lo + FFN_CHUNK])
        up = _dot(h, win_ref[:, FFN_DIM + lo:FFN_DIM + lo + FFN_CHUNK])
        act = (_silu(gate) * up).astype(BF16)
        acc = acc + _dot(act, wout_ref[lo:lo + FFN_CHUNK, :])
    y = x + 0.5 * acc
    if final:
        y = _rms(y, fg_ref[...])
    o_ref[...] = y


def _ffn(x, gain, w_in, w_out, final_gain, final):
    n = x.shape[0]
    tm = min(FFN_TM, n)
    return pl.pallas_call(
        functools.partial(_ffn_kernel, final=final),
        grid=(n // tm,),
        in_specs=[
            pl.BlockSpec((tm, D_MODEL), lambda i: (i, 0)),
            _resident((1, D_MODEL)),
            _resident((D_MODEL, 2 * FFN_DIM)),
            _resident((FFN_DIM, D_MODEL)),
            _resident((1, D_MODEL)),
        ],
        out_specs=pl.BlockSpec((tm, D_MODEL), lambda i: (i, 0)),
        out_shape=jax.ShapeDtypeStruct((n, D_MODEL), F32),
        compiler_params=_cparams(("parallel",)),
        name="ffn_half",
    )(x, gain, w_in, w_out, final_gain)


PROJ_TM = 256
PROJ_CHUNK = 512


def _proj_kernel(x_ref, g_ref, w_ref, cos_ref, s1_ref, s2_ref,
                 a_ref, sc_ref, b_ref, cqkv_ref, cz_ref, gl_ref):
    h = _rms(x_ref[...], g_ref[...]).astype(BF16)
    cos, s1, s2 = cos_ref[...], s1_ref[...], s2_ref[...]

    off = 0
    for j in range(W_A // LANES):
        t = _dot(h, w_ref[:, off:off + LANES])
        if j < ROT_TILES:
            t = t * cos + pltpu.roll(t, 8, 1) * s1 + pltpu.roll(t, LANES - 8, 1) * s2
        a_ref[:, off:off + LANES] = t.astype(BF16)
        off += LANES
    sc_ref[...] = _dot(h, w_ref[:, off:off + W_SC])
    off += W_SC
    for ref, width in ((b_ref, W_B), (cqkv_ref, W_CQKV), (cz_ref, W_CZ), (gl_ref, W_G)):
        for lo in range(0, width, PROJ_CHUNK):
            ref[:, lo:lo + PROJ_CHUNK] = _dot(
                h, w_ref[:, off + lo:off + lo + PROJ_CHUNK]).astype(ref.dtype)
        off += width


def _proj(x, gain, w, cos, s1, s2, seq):
    n = x.shape[0]
    tm = min(PROJ_TM, seq)
    per_seq = seq // tm
    row = lambda i: (i, 0)
    tab = lambda i: (i % per_seq, 0)
    widths = (W_A, W_SC, W_B, W_CQKV, W_CZ, W_G)
    dtypes = (BF16, F32, BF16, F32, F32, F32)
    return pl.pallas_call(
        _proj_kernel,
        grid=(n // tm,),
        in_specs=[
            pl.BlockSpec((tm, D_MODEL), row),
            _resident((1, D_MODEL)),
            _resident((D_MODEL, W_TOTAL)),
            pl.BlockSpec((tm, LANES), tab),
            pl.BlockSpec((tm, LANES), tab),
            pl.BlockSpec((tm, LANES), tab),
        ],
        out_specs=[pl.BlockSpec((tm, wd), row) for wd in widths],
        out_shape=[jax.ShapeDtypeStruct((n, wd), dt) for wd, dt in zip(widths, dtypes)],
        compiler_params=_cparams(("parallel",)),
        name="mixer_proj",
    )(x, gain, w, cos, s1, s2)


DSA_TQ = 256
DSA_CK = 512
INT_MIN = -2 ** 31
KEY_NEG_INF = int(np.array(-np.inf, np.float32).view(np.int32)) ^ 0x7FFFFFFF
SLAB = 64


def _dsa_t_kernel(qa_ref, qi_ref, kk_ref, vv_ref, sc_ref, o_ref,
                  vt_ref, key_ref, bias_ref, qs_ref, m_ref, l_ref, alpha_ref, acc_ref,
                  s_ref, p_ref, *, top_k, ck):
    i = pl.program_id(1)
    tq = qa_ref.shape[1]
    n_chunks_total = key_ref.shape[0]
    nck = (i * tq + tq + ck - 1) // ck
    lane = lax.broadcasted_iota(jnp.int32, (tq, LANES), 1)
    lo_half = lane < HEAD_DIM
    scale = HEAD_DIM ** -0.5

    @pl.when(i == 0)
    def _():
        for c in range(n_chunks_total):
            vt_ref[c] = vv_ref[0, c * ck:(c + 1) * ck, :].astype(F32).T.astype(BF16)

    def put(h, t):
        qs_ref[h * tq:(h + 1) * tq, :] = (t * scale).astype(BF16)

    for j in range(A_HEADS // 2):
        t = qa_ref[0, :, j * LANES:(j + 1) * LANES].astype(F32)
        put(2 * j, jnp.where(lo_half, t, 0.0))
        put(2 * j + 1, jnp.where(lo_half, pltpu.roll(t, HEAD_DIM, 1), 0.0))
    for j in range(IDX_HEADS // 2):
        t = qi_ref[0, :, j * LANES:(j + 1) * LANES].astype(F32)
        put(A_HEADS + 2 * j, jnp.where(lo_half, 0.0, pltpu.roll(t, HEAD_DIM, 1)))
        put(A_HEADS + 2 * j + 1, jnp.where(lo_half, 0.0, t))

    w_rows = (sc_ref[0] * (IDX_HEADS ** -0.5)).T
    key_pos = lax.broadcasted_iota(jnp.int32, (ck, tq), 0)
    qry_pos = i * tq + lax.broadcasted_iota(jnp.int32, (ck, tq), 1)

    def key_rows(c):
        return pl.ds(pl.multiple_of(c * ck, ck), ck)

    def score_chunk(c, _):
        kc = kk_ref[0, key_rows(c), :]
        score = None
        for h in range(IDX_HEADS):
            rel = jnp.maximum(
                _dot_nt(kc, qs_ref[(A_HEADS + h) * tq:(A_HEADS + h + 1) * tq, :]), 0.0)
            term = rel * w_rows[SC_IW + h:SC_IW + h + 1, :]
            score = term if score is None else score + term
        score = jnp.where(score == 0.0, 0.0, score)
        score = jnp.where(key_pos + c * ck <= qry_pos, score, -jnp.inf)
        bits = lax.bitcast_convert_type(score, jnp.int32)
        key_ref[c] = jnp.where(bits < 0, bits ^ 0x7FFFFFFF, bits)
        return 0

    lax.fori_loop(0, nck, score_chunk, 0)

    def count(pred):
        def body(c, acc):
            for r0 in range(0, ck, SLAB):
                acc = acc + jnp.where(pred(key_ref[c, r0:r0 + SLAB, :]), 1.0, 0.0)
            return acc
        acc = lax.fori_loop(0, nck, body, jnp.zeros((SLAB, tq), F32))
        return jnp.sum(acc, axis=0, keepdims=True)

    kf = float(top_k)

    def bit_step(b, thr):
        cand = thr + lax.shift_left(jnp.int32(1), 31 - b)
        cand_slab = jnp.broadcast_to(cand, (SLAB, tq))
        cnt = count(lambda kc: kc >= cand_slab)
        return jnp.where(cnt >= kf, cand, thr)

    thr = lax.fori_loop(0, 32, bit_step, jnp.full((1, tq), INT_MIN, jnp.int32))
    thr = jnp.maximum(thr, KEY_NEG_INF + 1)
    thr_slab = jnp.broadcast_to(thr, (SLAB, tq))
    cnt_ge = count(lambda kc: kc >= thr_slab)

    def write_bias(c, _):
        bias_ref[c] = jnp.where(key_ref[c] >= thr, 0.0, NEG_MASK)
        return 0

    lax.fori_loop(0, nck, write_bias, 0)

    @pl.when(jnp.max(cnt_ge) > kf)
    def _():
        need = kf - count(lambda kc: kc > thr_slab)
        r = lax.broadcasted_iota(jnp.int32, (ck, ck), 0)
        cidx = lax.broadcasted_iota(jnp.int32, (ck, ck), 1)
        lower = jnp.where(cidx <= r, 1.0, 0.0).astype(BF16)

        def tie_chunk(c, before):
            kc = key_ref[c]
            tie = kc == thr
            rank = before + _dot(lower, jnp.where(tie, 1.0, 0.0).astype(BF16))
            keep = (kc > thr) | (tie & (rank <= need))
            bias_ref[c] = jnp.where(keep, 0.0, NEG_MASK)
            return rank[ck - 1:ck, :]

        lax.fori_loop(0, nck, tie_chunk, jnp.zeros((1, tq), F32))

    m_ref[...] = jnp.full(m_ref.shape, NEG_BIG, F32)
    l_ref[...] = jnp.zeros(l_ref.shape, F32)
    acc_ref[...] = jnp.zeros(acc_ref.shape, F32)
    reps = ck // 8

    n_att = A_HEADS * tq

    def attend(c, _):
        s_ref[...] = _dot_nt(kk_ref[0, key_rows(c), :], qs_ref[0:n_att, :])
        bias = bias_ref[c]
        for h in range(A_HEADS):
            cols = slice(h * tq, (h + 1) * tq)
            s = s_ref[:, cols] + bias
            m_old = m_ref[:, cols]
            m_new = jnp.maximum(m_old, jnp.max(s, axis=0, keepdims=True))
            alpha = jnp.exp(m_old - m_new)
            p = jnp.exp(s - jnp.tile(m_new, (reps, 1)))
            l_ref[:, cols] = alpha * l_ref[:, cols] + jnp.sum(p, axis=0, keepdims=True)
            m_ref[:, cols] = m_new
            alpha_ref[:, cols] = alpha
            p_ref[:, cols] = p.astype(BF16)
        acc_ref[...] = (jnp.tile(alpha_ref[...], (LANES // 8, 1)) * acc_ref[...]
                        + _dot(vt_ref[c], p_ref[...]))
        return 0

    lax.fori_loop(0, nck, attend, 0)

    for j in range(A_HEADS // 2):
        outs = []
        for h in (2 * j, 2 * j + 1):
            cols = slice(h * tq, (h + 1) * tq)
            out_t = acc_ref[:, cols] / jnp.tile(l_ref[:, cols], (LANES // 8, 1))
            outs.append(out_t.T)
        o_ref[0, :, j * LANES:(j + 1) * LANES] = (
            outs[0] + pltpu.roll(outs[1], HEAD_DIM, 1)).astype(o_ref.dtype)


def _dsa_t(a_grp, sc, top_k):
    b, s, _ = a_grp.shape
    tq = min(DSA_TQ, s)
    ck = min(DSA_CK, s)
    return pl.pallas_call(
        functools.partial(_dsa_t_kernel, top_k=top_k, ck=ck),
        grid=(b, s // tq),
        in_specs=[
            pl.BlockSpec((1, tq, 512), lambda bi, i: (bi, i, 0)),
            pl.BlockSpec((1, tq, 256), lambda bi, i: (bi, i, 2)),
            pl.BlockSpec((1, s, LANES), lambda bi, i: (bi, 0, 6)),
            pl.BlockSpec((1, s, LANES), lambda bi, i: (bi, 0, 7)),
            pl.BlockSpec((1, tq, LANES), lambda bi, i: (bi, i, 0)),
        ],
        out_specs=pl.BlockSpec((1, tq, BRANCH), lambda bi, i: (bi, i, 0)),
        out_shape=jax.ShapeDtypeStruct((b, s, BRANCH), BF16),
        scratch_shapes=[
            pltpu.VMEM((s // ck, LANES, ck), BF16),
            pltpu.VMEM((s // ck, ck, tq), jnp.int32),
            pltpu.VMEM((s // ck, ck, tq), F32),
            pltpu.VMEM(((A_HEADS + IDX_HEADS) * tq, LANES), BF16),
            pltpu.VMEM((8, A_HEADS * tq), F32),
            pltpu.VMEM((8, A_HEADS * tq), F32),
            pltpu.VMEM((8, A_HEADS * tq), F32),
            pltpu.VMEM((LANES, A_HEADS * tq), F32),
            pltpu.VMEM((ck, A_HEADS * tq), F32),
            pltpu.VMEM((ck, A_HEADS * tq), BF16),
        ],
        compiler_params=_cparams(("parallel", "arbitrary")),
        name="dsa_attention",
    )(a_grp, a_grp, a_grp, a_grp, sc)


FOX_T = 256
BIAS_LANE = HEAD_DIM
BIAS_TERMS = 3


def _fox_t_kernel(q_ref, k_ref, v_ref, sc_ref, bf_ref, o_ref,
                  kaug_ref, vt_ref, qz_ref, m_ref, l_ref, alpha_ref, acc_ref, s_ref, p_ref, *, t):
    i = pl.program_id(1)
    seq = k_ref.shape[1]
    n_chunks = seq // t
    pairs = B_HEADS // 2
    lane = lax.broadcasted_iota(jnp.int32, (t, LANES), 1)
    lo_half = lane < HEAD_DIM

    def bias_lanes(h):
        first = BIAS_LANE + BIAS_TERMS * h
        return (lane >= first) & (lane < first + BIAS_TERMS)

    @pl.when(i == 0)
    def _():
        r = lax.broadcasted_iota(jnp.int32, (t, t), 0)
        c = lax.broadcasted_iota(jnp.int32, (t, t), 1)
        lower = jnp.where(c <= r, 1.0, 0.0)
        src = lax.broadcasted_iota(jnp.int32, (LANES, LANES), 0)
        dst = lax.broadcasted_iota(jnp.int32, (LANES, LANES), 1)

        def build(n, carry):
            rows = pl.ds(pl.multiple_of(n * t, t), t)
            x = sc_ref[0, rows, :] + bf_ref[...]
            log_f = jnp.minimum(x, 0.0) - jnp.log1p(jnp.exp(-jnp.abs(x)))
            cum = _dot_hi(lower, log_f) + carry
            nc = -cum
            hi = nc.astype(BF16)
            rest = nc - hi.astype(F32)
            mid = rest.astype(BF16)
            lo = (rest - mid.astype(F32)).astype(BF16)
            placed = None
            for j, term in enumerate((hi, mid, lo)):
                place = jnp.where((src >= SC_BF) & (src < SC_BF + B_HEADS)
                                  & (dst == BIAS_LANE + BIAS_TERMS * (src - SC_BF) + j), 1.0, 0.0)
                part = _dot(term, place.astype(BF16))
                placed = part if placed is None else placed + part
            for h in range(B_HEADS):
                hp = h // 2
                kt = k_ref[0, rows, hp * LANES:(hp + 1) * LANES].astype(F32)
                if h % 2:
                    kt = pltpu.roll(kt, HEAD_DIM, 1)
                tile = jnp.where(lo_half, kt, jnp.where(bias_lanes(h), placed, 0.0))
                kaug_ref[h, rows, :] = tile.astype(BF16)
            for hp in range(pairs):
                vt_ref[hp, n] = v_ref[0, rows, hp * LANES:(hp + 1) * LANES].astype(F32).T.astype(BF16)
            return cum[t - 1:t, :]

        lax.fori_loop(0, n_chunks, build, jnp.zeros((1, LANES), F32))

    for hp in range(pairs):
        q = q_ref[0, :, hp * LANES:(hp + 1) * LANES].astype(F32) * (HEAD_DIM ** -0.5)
        qz_ref[(2 * hp) * t:(2 * hp + 1) * t, :] = jnp.where(
            lo_half, q, jnp.where(bias_lanes(2 * hp), 1.0, 0.0)).astype(BF16)
        qz_ref[(2 * hp + 1) * t:(2 * hp + 2) * t, :] = jnp.where(
            lo_half, pltpu.roll(q, HEAD_DIM, 1), jnp.where(bias_lanes(2 * hp + 1), 1.0, 0.0)).astype(BF16)
    causal = (lax.broadcasted_iota(jnp.int32, (t, t), 0)
              <= lax.broadcasted_iota(jnp.int32, (t, t), 1))
    m_ref[...] = jnp.full(m_ref.shape, NEG_BIG, F32)
    l_ref[...] = jnp.zeros(l_ref.shape, F32)
    acc_ref[...] = jnp.zeros(acc_ref.shape, F32)

    def logits(c, slot):
        rows = pl.ds(pl.multiple_of(c * t, t), t)
        for h in range(B_HEADS):
            cols = slice(h * t, (h + 1) * t)
            s_ref[slot, :, cols] = _dot_nt(kaug_ref[h, rows, :], qz_ref[cols, :])

    def attend(c, slot, masked):
        if not masked:
            logits(c + 1, 1 - slot)
        for h in range(B_HEADS):
            cols = slice(h * t, (h + 1) * t)
            s = s_ref[slot, :, cols]
            if masked:
                s = jnp.where(causal, s, -jnp.inf)
            m_old = m_ref[:, cols]
            m_new = jnp.maximum(m_old, jnp.max(s, axis=0, keepdims=True))
            alpha = jnp.exp(m_old - m_new)
            p = jnp.exp(s - jnp.tile(m_new, (t // 8, 1)))
            l_ref[:, cols] = alpha * l_ref[:, cols] + jnp.sum(p, axis=0, keepdims=True)
            m_ref[:, cols] = m_new
            alpha_ref[:, cols] = alpha
            p_ref[:, cols] = p.astype(BF16)
        for h in range(B_HEADS):
            cols = slice(h * t, (h + 1) * t)
            acc_ref[:, cols] = (jnp.tile(alpha_ref[:, cols], (LANES // 8, 1)) * acc_ref[:, cols]
                                + _dot(vt_ref[h // 2, c], p_ref[:, cols]))

    def attend_pair(k, _):
        attend(2 * k, 0, masked=False)
        attend(2 * k + 1, 1, masked=False)
        return 0

    logits(0, 0)
    lax.fori_loop(0, i // 2, attend_pair, 0)

    @pl.when(i % 2 == 0)
    def _():
        attend(i, 0, masked=True)

    @pl.when(i % 2 == 1)
    def _():
        attend(i - 1, 0, masked=False)
        attend(i, 1, masked=True)

    for hp in range(pairs):
        outs = []
        for h in (2 * hp, 2 * hp + 1):
            cols = slice(h * t, (h + 1) * t)
            out_t = acc_ref[:, cols] / jnp.tile(l_ref[:, cols], (LANES // 8, 1))
            outs.append(out_t.T)
        o_ref[0, :, hp * LANES:(hp + 1) * LANES] = jnp.where(
            lo_half, outs[0], outs[1]).astype(o_ref.dtype)


def _fox_t(b_grp, sc, bf_row):
    b, s, _ = b_grp.shape
    t = min(FOX_T, s)
    pairs = B_HEADS // 2
    return pl.pallas_call(
        functools.partial(_fox_t_kernel, t=t),
        grid=(b, s // t),
        in_specs=[
            pl.BlockSpec((1, t, BRANCH), lambda bi, i: (bi, i, 0)),
            pl.BlockSpec((1, s, BRANCH), lambda bi, i: (bi, 0, 1)),
            pl.BlockSpec((1, s, BRANCH), lambda bi, i: (bi, 0, 2)),
            pl.BlockSpec((1, s, LANES), lambda bi, i: (bi, 0, 0)),
            _resident((1, LANES)),
        ],
        out_specs=pl.BlockSpec((1, t, BRANCH), lambda bi, i: (bi, i, 0)),
        out_shape=jax.ShapeDtypeStruct((b, s, BRANCH), BF16),
        scratch_shapes=[
            pltpu.VMEM((B_HEADS, s, LANES), BF16),
            pltpu.VMEM((pairs, s // t, LANES, t), BF16),
            pltpu.VMEM((B_HEADS * t, LANES), BF16),
            pltpu.VMEM((8, B_HEADS * t), F32),
            pltpu.VMEM((8, B_HEADS * t), F32),
            pltpu.VMEM((8, B_HEADS * t), F32),
            pltpu.VMEM((LANES, B_HEADS * t), F32),
            pltpu.VMEM((2, t, B_HEADS * t), F32),
            pltpu.VMEM((t, B_HEADS * t), BF16),
        ],
        compiler_params=_cparams(("parallel", "arbitrary")),
        name="fox_attention",
    )(b_grp, b_grp, b_grp, sc, bf_row)


GDN_TS = 512
GDN_C = 128
TAIL = 8
NEUMANN_ROUNDS = 6


def _gdn_kernel(x_ref, z_ref, sc_ref, cw_ref, alog_ref, dtb_ref, dn_ref, o_ref,
                state_ref, tail_ref, ext_ref, act_ref, np_ref, rhs_ref,
                qk_ref, qg_ref, kdt_ref, dec_ref, *, ts):
    j = pl.program_id(1)
    width = x_ref.shape[2]
    n_chunks = ts // GDN_C
    chains = [(h, n) for n in range(n_chunks) for h in range(C_HEADS)]

    @pl.when(j == 0)
    def _():
        state_ref[...] = jnp.zeros_like(state_ref)
        tail_ref[...] = jnp.zeros_like(tail_ref)

    x = x_ref[0]
    ext_ref[0:TAIL, :] = tail_ref[...]
    ext_ref[TAIL:TAIL + ts, :] = x
    tail_ref[...] = x[ts - TAIL:ts, :]
    conv = jnp.zeros((ts, width), F32)
    for tap in range(CONV_WIDTH):
        start = TAIL - (CONV_WIDTH - 1) + tap
        conv = conv + cw_ref[tap:tap + 1, :] * ext_ref[start:start + ts, :]
    act_ref[...] = _silu(conv)

    sc = sc_ref[0]
    beta_all = jax.nn.sigmoid(sc)
    xg = sc + dtb_ref[...]
    softplus = jnp.maximum(xg, 0.0) + jnp.log1p(jnp.exp(-jnp.abs(xg)))
    g_all = -jnp.exp(alog_ref[...]) * softplus
    r = lax.broadcasted_iota(jnp.int32, (ts, ts), 0)
    c = lax.broadcasted_iota(jnp.int32, (ts, ts), 1)
    same_chunk_lower = jnp.where((c <= r) & (r // GDN_C == c // GDN_C), 1.0, 0.0)
    gc_all = _dot_hi(same_chunk_lower, g_all)
    gc_t = gc_all.T

    ri = lax.broadcasted_iota(jnp.int32, (GDN_C, GDN_C), 0)
    ci = lax.broadcasted_iota(jnp.int32, (GDN_C, GDN_C), 1)
    incl = ri >= ci
    strict = ri > ci

    def l2n(t):
        return t * lax.rsqrt(jnp.sum(t * t, axis=-1, keepdims=True) + L2_EPS)

    for idx, (h, n) in enumerate(chains):
        rows = slice(n * GDN_C, (n + 1) * GDN_C)
        q = l2n(act_ref[rows, h * C_DIM:(h + 1) * C_DIM]) * (C_DIM ** -0.5)
        k = l2n(act_ref[rows, BRANCH + h * C_DIM:BRANCH + (h + 1) * C_DIM])
        v = act_ref[rows, 2 * BRANCH + h * C_DIM:2 * BRANCH + (h + 1) * C_DIM]
        beta = beta_all[rows, SC_BETA + h:SC_BETA + h + 1]
        gcol = gc_all[rows, SC_DECAY + h:SC_DECAY + h + 1]
        grow = gc_t[SC_DECAY + h:SC_DECAY + h + 1, n * GDN_C:(n + 1) * GDN_C]
        g_last = gcol[GDN_C - 1:GDN_C, :]
        decay = jnp.exp(jnp.where(incl, gcol - grow, -jnp.inf))
        kb = k * beta
        k16 = k.astype(BF16)
        q16 = q.astype(BF16)
        a_mat = jnp.where(strict, _dot_nt(kb.astype(BF16), k16) * decay, 0.0)
        np_ref[idx, :, 0:GDN_C] = -a_mat
        np_ref[idx, :, GDN_C:2 * GDN_C] = _dot_solve(a_mat, a_mat)
        rhs_ref[idx, :, 0:C_DIM] = v * beta
        rhs_ref[idx, :, C_DIM:2 * C_DIM] = kb * jnp.exp(gcol)
        qk_ref[idx] = jnp.where(incl, _dot_nt(q16, k16) * decay, 0.0).astype(BF16)
        qg_ref[idx] = (q * jnp.exp(gcol)).astype(BF16)
        kdt_ref[idx] = (k * jnp.exp(g_last - gcol)).T.astype(BF16)
        dec_ref[idx] = jnp.broadcast_to(jnp.exp(g_last), (1, C_DIM))

    for rnd in range(1, NEUMANN_ROUNDS + 1):
        last = rnd == NEUMANN_ROUNDS
        for idx in range(len(chains)):
            n_old = np_ref[idx, :, 0:GDN_C]
            p_old = np_ref[idx, :, GDN_C:2 * GDN_C]
            if last:
                np_ref[idx, :, 0:GDN_C] = n_old + p_old + _dot_solve(p_old, n_old)
            else:
                prod = _dot_solve(p_old, np_ref[idx])
                np_ref[idx, :, 0:GDN_C] = n_old + p_old + prod[:, 0:GDN_C]
                np_ref[idx, :, GDN_C:2 * GDN_C] = prod[:, GDN_C:2 * GDN_C]

    for idx in range(len(chains)):
        rhs = rhs_ref[idx]
        rhs_ref[idx] = rhs + _dot_solve(np_ref[idx, :, 0:GDN_C], rhs)

    for idx, (h, n) in enumerate(chains):
        rows = slice(n * GDN_C, (n + 1) * GDN_C)
        state = state_ref[h]
        s16 = state.astype(BF16)
        v_new = rhs_ref[idx, :, 0:C_DIM] - _dot(rhs_ref[idx, :, C_DIM:2 * C_DIM].astype(BF16), s16)
        vn16 = v_new.astype(BF16)
        o = _dot(qg_ref[idx], s16) + _dot(qk_ref[idx], vn16)
        state_ref[h] = state * dec_ref[idx] + _dot(kdt_ref[idx], vn16)
        z = z_ref[0, rows, h * C_DIM:(h + 1) * C_DIM]
        y = _rms(o, dn_ref[...]) * _silu(z)
        o_ref[0, rows, h * C_DIM:(h + 1) * C_DIM] = y.astype(o_ref.dtype)


def _gdn(cqkv, cz, sc, conv_w, alog_row, dtb_row, dn_row):
    b, s, width = cqkv.shape
    ts = min(GDN_TS, s)
    nch = C_HEADS * (ts // GDN_C)
    blk = lambda bi, j: (bi, j, 0)
    return pl.pallas_call(
        functools.partial(_gdn_kernel, ts=ts),
        grid=(b, s // ts),
        in_specs=[
            pl.BlockSpec((1, ts, width), blk),
            pl.BlockSpec((1, ts, BRANCH), blk),
            pl.BlockSpec((1, ts, LANES), blk),
            _resident((CONV_WIDTH, width)),
            _resident((1, LANES)),
            _resident((1, LANES)),
            _resident((1, C_DIM)),
        ],
        out_specs=pl.BlockSpec((1, ts, BRANCH), blk),
        out_shape=jax.ShapeDtypeStruct((b, s, BRANCH), BF16),
        scratch_shapes=[
            pltpu.VMEM((C_HEADS, C_DIM, C_DIM), F32),
            pltpu.VMEM((TAIL, width), F32),
            pltpu.VMEM((TAIL + ts, width), F32),
            pltpu.VMEM((ts, width), F32),
            pltpu.VMEM((nch, GDN_C, 2 * GDN_C), F32),
            pltpu.VMEM((nch, GDN_C, 2 * C_DIM), F32),
            pltpu.VMEM((nch, GDN_C, GDN_C), BF16),
            pltpu.VMEM((nch, GDN_C, C_DIM), BF16),
            pltpu.VMEM((nch, C_DIM, GDN_C), BF16),
            pltpu.VMEM((nch, 1, C_DIM), F32),
        ],
        compiler_params=_cparams(("parallel", "arbitrary")),
        name="gated_deltanet",
    )(cqkv, cz, sc, conv_w, alog_row, dtb_row, dn_row)


MERGE_TM = 512


def _merge_kernel(x_ref, ya_ref, yb_ref, yc_ref, gl_ref, bg_ref,
                  wa_ref, wb_ref, wc_ref, wo_ref, o_ref):
    merged = None
    for n, (y_ref, w_ref) in enumerate(((ya_ref, wa_ref), (yb_ref, wb_ref), (yc_ref, wc_ref))):
        cols = slice(n * D_MODEL, (n + 1) * D_MODEL)
        gate = jax.nn.sigmoid(gl_ref[:, cols] + bg_ref[:, cols])
        term = gate * _dot(y_ref[...], w_ref[...])
        merged = term if merged is None else merged + term
    o_ref[...] = x_ref[...] + _dot(merged.astype(BF16), wo_ref[...])


def _merge(x, ya, yb, yc, gl, b_gate, wa, wb, wc, wo):
    n = x.shape[0]
    tm = min(MERGE_TM, n)
    row = lambda i: (i, 0)
    return pl.pallas_call(
        _merge_kernel,
        grid=(n // tm,),
        in_specs=[
            pl.BlockSpec((tm, D_MODEL), row),
            pl.BlockSpec((tm, BRANCH), row),
            pl.BlockSpec((tm, BRANCH), row),
            pl.BlockSpec((tm, BRANCH), row),
            pl.BlockSpec((tm, 3 * D_MODEL), row),
            _resident((1, 3 * D_MODEL)),
            _resident((BRANCH, D_MODEL)),
            _resident((BRANCH, D_MODEL)),
            _resident((BRANCH, D_MODEL)),
            _resident((D_MODEL, D_MODEL)),
        ],
        out_specs=pl.BlockSpec((tm, D_MODEL), row),
        out_shape=jax.ShapeDtypeStruct((n, D_MODEL), F32),
        compiler_params=_cparams(("parallel",)),
        name="gated_merge",
    )(x, ya, yb, yc, gl, b_gate, wa, wb, wc, wo)


def _w_in_pieces():
    offs = np.concatenate([[0], np.cumsum(IN_SIZES)])
    names = ("a_q", "a_k", "a_v", "i_q", "i_k", "i_w", "b_qkv", "b_f", "c_qkv", "c_z",
             "c_beta", "c_a", "gates")
    src = {n: (int(offs[i]), int(IN_SIZES[i])) for i, n in enumerate(names)}
    order = (("a_q", 0), ("i_q", 512), ("a_k", 768), ("i_k", 832), ("a_v", 896),
             ("b_f", W_A + SC_BF), ("i_w", W_A + SC_IW), ("c_beta", W_A + SC_BETA),
             ("c_a", W_A + SC_DECAY), ("b_qkv", W_A + W_SC), ("c_qkv", W_A + W_SC + W_B),
             ("c_z", W_A + W_SC + W_B + W_CQKV), ("gates", W_A + W_SC + W_B + W_CQKV + W_CZ))
    pieces = []
    for name, dst in order:
        s0, width = src[name]
        done = 0
        while done < width:
            step = min(width - done, LANES - (dst + done) % LANES)
            pieces.append((dst + done, s0 + done, step))
            done += step
    return pieces


RELAYOUT_TM = 256


def _relayout_kernel(w_ref, tail_ref, o_ref, *, aligned_cols):
    by_tile = {}
    for dst, src, n in _w_in_pieces():
        by_tile.setdefault(dst // LANES, []).append((dst % LANES, src, n))
    row = lax.broadcasted_iota(jnp.int32, (2 * LANES, LANES), 0)
    col = lax.broadcasted_iota(jnp.int32, (2 * LANES, LANES), 1)

    def src_tile(start):
        if start + LANES <= aligned_cols:
            return w_ref[0, :, start:start + LANES]
        return tail_ref[0, :, start - aligned_cols:start - aligned_cols + LANES]

    for tile in range(W_TOTAL // LANES):
        acc = None
        for dst_lane, src, n in by_tile.get(tile, ()):
            base = (src // LANES) * LANES
            shift = src - base - dst_lane
            select = (row == col + shift) & (col >= dst_lane) & (col < dst_lane + n)
            window = jnp.concatenate([src_tile(base), src_tile(base + LANES)], axis=1)
            part = _dot(window.astype(BF16), jnp.where(select, 1.0, 0.0).astype(BF16))
            acc = part if acc is None else acc + part
        if acc is None:
            acc = jnp.zeros((w_ref.shape[1], LANES), F32)
        o_ref[0, :, tile * LANES:(tile + 1) * LANES] = acc.astype(BF16)


def _layout_w_in(w_in):
    depth, rows, cols = w_in.shape
    aligned_cols = (cols // LANES) * LANES
    tail = jnp.pad(w_in[:, :, aligned_cols:], ((0, 0), (0, 0), (0, 2 * LANES - (cols - aligned_cols))))
    tm = min(RELAYOUT_TM, rows)
    return pl.pallas_call(
        functools.partial(_relayout_kernel, aligned_cols=aligned_cols),
        grid=(depth, rows // tm),
        in_specs=[pl.BlockSpec((1, tm, cols), lambda l, i: (l, i, 0)),
                  pl.BlockSpec((1, tm, 2 * LANES), lambda l, i: (l, i, 0))],
        out_specs=pl.BlockSpec((1, tm, W_TOTAL), lambda l, i: (l, i, 0)),
        out_shape=jax.ShapeDtypeStruct((depth, rows, W_TOTAL), BF16),
        compiler_params=_cparams(("parallel", "parallel")),
        name="w_in_relayout",
    )(w_in, tail)


def _rotary_tables(seq):
    pos = jnp.arange(seq, dtype=F32)
    inv_freq = jnp.power(ROPE_THETA, -jnp.arange(0, ROT_DIM, 2, dtype=F32) / ROT_DIM)
    ang = pos[:, None] * inv_freq[None, :]
    cos, sin = jnp.cos(ang), jnp.sin(ang)
    half = ROT_DIM // 2
    ones = jnp.ones((seq, HEAD_DIM - ROT_DIM), F32)
    zeros_h = jnp.zeros((seq, half), F32)
    zeros_r = jnp.zeros((seq, HEAD_DIM - ROT_DIM), F32)
    c64 = jnp.concatenate([cos, cos, ones], axis=1)
    s1_64 = jnp.concatenate([zeros_h, sin, zeros_r], axis=1)
    s2_64 = jnp.concatenate([-sin, zeros_h, zeros_r], axis=1)
    twice = lambda t: jnp.concatenate([t, t], axis=1)
    return twice(c64), twice(s1_64), twice(s2_64)


def _lane_row(values, start):
    return jnp.zeros((1, LANES), F32).at[0, start:start + values.shape[0]].set(values.astype(F32))


def kernel(x, ffn1_norm, ffn1_w_in, ffn1_w_out, mix_norm, w_in, b_gate, b_forget, conv_w, a_log, dt_bias, delta_norm, w_branch_a, w_branch_b, w_branch_c, w_out, ffn2_norm, ffn2_w_in, ffn2_w_out, final_norm):
    batch, seq, _ = x.shape
    depth = w_in.shape[0]
    top_k = min(INDEX_TOPK, seq // 4)
    cos, s1, s2 = _rotary_tables(seq)
    w_mix = _layout_w_in(w_in)
    final_row = final_norm.reshape(1, D_MODEL)
    xt = x.reshape(batch * seq, D_MODEL)
    for l in range(depth):
        xt = _ffn(xt, ffn1_norm[l].reshape(1, D_MODEL), ffn1_w_in[l].astype(BF16),
                  ffn1_w_out[l].astype(BF16), final_row, False)
        a_grp, sc, b_grp, cqkv, cz, gl = _proj(
            xt, mix_norm[l].reshape(1, D_MODEL), w_mix[l], cos, s1, s2, seq)
        a_grp = a_grp.reshape(batch, seq, W_A)
        sc = sc.reshape(batch, seq, W_SC)
        y_a = _dsa_t(a_grp, sc, top_k)
        y_b = _fox_t(b_grp.reshape(batch, seq, W_B), sc, _lane_row(b_forget[l], SC_BF))
        y_c = _gdn(cqkv.reshape(batch, seq, W_CQKV), cz.reshape(batch, seq, W_CZ), sc,
                   conv_w[l], _lane_row(a_log[l], SC_DECAY), _lane_row(dt_bias[l], SC_DECAY),
                   delta_norm[l].reshape(1, C_DIM))
        flat = lambda t: t.reshape(batch * seq, BRANCH)
        xt = _merge(xt, flat(y_a), flat(y_b), flat(y_c), gl, b_gate[l].reshape(1, 3 * D_MODEL),
                    w_branch_a[l].astype(BF16), w_branch_b[l].astype(BF16),
                    w_branch_c[l].astype(BF16), w_out[l].astype(BF16))
        xt = _ffn(xt, ffn2_norm[l].reshape(1, D_MODEL), ffn2_w_in[l].astype(BF16),
                  ffn2_w_out[l].astype(BF16), final_row, l == depth - 1)
    return xt.reshape(batch, seq, D_MODEL)
```

```python
import functools

import numpy as np
import jax
import jax.numpy as jnp
from jax import lax
from jax.experimental import pallas as pl
from jax.experimental.pallas import tpu as pltpu

F32 = jnp.float32
BF16 = jnp.bfloat16

D_MODEL = 1024
BRANCH = 512
A_HEADS = 8
HEAD_DIM = 64
IDX_HEADS = 4
INDEX_TOPK = 256
B_HEADS = 8
C_HEADS = 4
C_DIM = 128
CONV_WIDTH = 4
ROPE_THETA = 500000.0
ROT_DIM = 16
FFN_DIM = 2048
NORM_EPS = 1e-6
L2_EPS = 1e-6
IN_SIZES = (512, 64, 64, 256, 64, 4, 1536, 8, 1536, 512, 4, 4, 3072)

LANES = 128
SC_BF = 0
SC_IW = 8
SC_BETA = 12
SC_DECAY = 16
W_A, W_SC, W_B, W_CQKV, W_CZ, W_G = 1024, 128, 1536, 1536, 512, 3072
W_TOTAL = W_A + W_SC + W_B + W_CQKV + W_CZ + W_G
ROT_TILES = 7

NEG_BIG = -1e30
NEG_MASK = -2e30
VMEM_LIMIT = 56 * 1024 * 1024


def _cparams(sem):
    return pltpu.CompilerParams(dimension_semantics=sem, vmem_limit_bytes=VMEM_LIMIT)


def _dot(a, b):
    return jnp.dot(a, b, preferred_element_type=F32)


def _dot_hi(a, b):
    return jnp.dot(a, b, preferred_element_type=F32, precision=lax.Precision.HIGHEST)


def _dot_solve(a, b):
    a_hi = a.astype(BF16)
    b_hi = b.astype(BF16)
    a_lo = (a - a_hi.astype(F32)).astype(BF16)
    b_lo = (b - b_hi.astype(F32)).astype(BF16)
    m = a.shape[0]
    top = _dot(jnp.concatenate([a_hi, a_lo], axis=0), b_hi)
    return top[0:m] + top[m:2 * m] + _dot(a_hi, b_lo)


def _dot_nt(a, b):
    return lax.dot_general(a, b, (((1,), (1,)), ((), ())), preferred_element_type=F32)


def _rms(x, gain):
    return x * lax.rsqrt(jnp.mean(x * x, axis=-1, keepdims=True) + NORM_EPS) * gain


def _silu(x):
    return x * jax.nn.sigmoid(x)


def _resident(shape):
    nd = len(shape)
    return pl.BlockSpec(shape, lambda *_: (0,) * nd, pipeline_mode=pl.Buffered(1))


FFN_TM = 512
FFN_CHUNK = 512


def _ffn_kernel(x_ref, g_ref, win_ref, wout_ref, fg_ref, o_ref, *, final):
    x = x_ref[...]
    h = _rms(x, g_ref[...]).astype(BF16)
    acc = jnp.zeros(x.shape, F32)
    for c in range(FFN_DIM // FFN_CHUNK):
        lo = c * FFN_CHUNK
        gate = _dot(h, win_ref[:, lo:lo + FFN_CHUNK])
        up = _dot(h, win_ref[:, FFN_DIM + lo:FFN_DIM + lo + FFN_CHUNK])
        act = (_silu(gate) * up).astype(BF16)
        acc = acc + _dot(act, wout_ref[lo:lo + FFN_CHUNK, :])
    y = x + 0.5 * acc
    if final:
        y = _rms(y, fg_ref[...])
    o_ref[...] = y


def _ffn(x, gain, w_in, w_out, final_gain, final):
    n = x.shape[0]
    tm = min(FFN_TM, n)
    return pl.pallas_call(
        functools.partial(_ffn_kernel, final=final),
        grid=(n // tm,),
        in_specs=[
            pl.BlockSpec((tm, D_MODEL), lambda i: (i, 0)),
            _resident((1, D_MODEL)),
            _resident((D_MODEL, 2 * FFN_DIM)),
            _resident((FFN_DIM, D_MODEL)),
            _resident((1, D_MODEL)),
        ],
        out_specs=pl.BlockSpec((tm, D_MODEL), lambda i: (i, 0)),
        out_shape=jax.ShapeDtypeStruct((n, D_MODEL), F32),
        compiler_params=_cparams(("parallel",)),
        name="ffn_half",
    )(x, gain, w_in, w_out, final_gain)


PROJ_TM = 256
PROJ_CHUNK = 512


def _proj_kernel(x_ref, g_ref, w_ref, cos_ref, s1_ref, s2_ref,
                 a_ref, sc_ref, b_ref, cqkv_ref, cz_ref, gl_ref):
    h = _rms(x_ref[...], g_ref[...]).astype(BF16)
    cos, s1, s2 = cos_ref[...], s1_ref[...], s2_ref[...]

    off = 0
    for j in range(W_A // LANES):
        t = _dot(h, w_ref[:, off:off + LANES])
        if j < ROT_TILES:
            t = t * cos + pltpu.roll(t, 8, 1) * s1 + pltpu.roll(t, LANES - 8, 1) * s2
        a_ref[:, off:off + LANES] = t.astype(BF16)
        off += LANES
    sc_ref[...] = _dot(h, w_ref[:, off:off + W_SC])
    off += W_SC
    for ref, width in ((b_ref, W_B), (cqkv_ref, W_CQKV), (cz_ref, W_CZ), (gl_ref, W_G)):
        for lo in range(0, width, PROJ_CHUNK):
            ref[:, lo:lo + PROJ_CHUNK] = _dot(
                h, w_ref[:, off + lo:off + lo + PROJ_CHUNK]).astype(ref.dtype)
        off += width


def _proj(x, gain, w, cos, s1, s2, seq):
    n = x.shape[0]
    tm = min(PROJ_TM, seq)
    per_seq = seq // tm
    row = lambda i: (i, 0)
    tab = lambda i: (i % per_seq, 0)
    widths = (W_A, W_SC, W_B, W_CQKV, W_CZ, W_G)
    dtypes = (BF16, F32, BF16, F32, F32, F32)
    return pl.pallas_call(
        _proj_kernel,
        grid=(n // tm,),
        in_specs=[
            pl.BlockSpec((tm, D_MODEL), row),
            _resident((1, D_MODEL)),
            _resident((D_MODEL, W_TOTAL)),
            pl.BlockSpec((tm, LANES), tab),
            pl.BlockSpec((tm, LANES), tab),
            pl.BlockSpec((tm, LANES), tab),
        ],
        out_specs=[pl.BlockSpec((tm, wd), row) for wd in widths],
        out_shape=[jax.ShapeDtypeStruct((n, wd), dt) for wd, dt in zip(widths, dtypes)],
        compiler_params=_cparams(("parallel",)),
        name="mixer_proj",
    )(x, gain, w, cos, s1, s2)


DSA_TQ = 256
DSA_CK = 512
INT_MIN = -2 ** 31
KEY_NEG_INF = int(np.array(-np.inf, np.float32).view(np.int32)) ^ 0x7FFFFFFF
SLAB = 64
SUM_ROW = HEAD_DIM
EXP_ROWS = 128


def _dsa_t_kernel(qa_ref, qi_ref, kk_ref, vv_ref, sc_ref, o_ref,
                  vt_ref, key_ref, bias_ref, qs_ref, m_ref, alpha_ref, acc_ref,
                  s_ref, p_ref, *, top_k, ck):
    i = pl.program_id(1)
    tq = qa_ref.shape[1]
    n_chunks_total = key_ref.shape[0]
    nck = (i * tq + tq + ck - 1) // ck
    lane = lax.broadcasted_iota(jnp.int32, (tq, LANES), 1)
    lo_half = lane < HEAD_DIM
    scale = HEAD_DIM ** -0.5

    @pl.when(i == 0)
    def _():
        ones_row = lax.broadcasted_iota(jnp.int32, (LANES, ck), 0) == SUM_ROW
        for c in range(n_chunks_total):
            vt = vv_ref[0, c * ck:(c + 1) * ck, :].astype(F32).T
            vt_ref[c] = jnp.where(ones_row, 1.0, vt).astype(BF16)

    def put(h, t):
        qs_ref[h * tq:(h + 1) * tq, :] = (t * scale).astype(BF16)

    for j in range(A_HEADS // 2):
        t = qa_ref[0, :, j * LANES:(j + 1) * LANES].astype(F32)
        put(2 * j, jnp.where(lo_half, t, 0.0))
        put(2 * j + 1, jnp.where(lo_half, pltpu.roll(t, HEAD_DIM, 1), 0.0))
    for j in range(IDX_HEADS // 2):
        t = qi_ref[0, :, j * LANES:(j + 1) * LANES].astype(F32)
        put(A_HEADS + 2 * j, jnp.where(lo_half, 0.0, pltpu.roll(t, HEAD_DIM, 1)))
        put(A_HEADS + 2 * j + 1, jnp.where(lo_half, 0.0, t))

    w_rows = (sc_ref[0] * (IDX_HEADS ** -0.5)).T
    key_pos = lax.broadcasted_iota(jnp.int32, (ck, tq), 0)
    qry_pos = i * tq + lax.broadcasted_iota(jnp.int32, (ck, tq), 1)

    def key_rows(c):
        return pl.ds(pl.multiple_of(c * ck, ck), ck)

    def score_chunk(c, _):
        kc = kk_ref[0, key_rows(c), :]
        score = None
        for h in range(IDX_HEADS):
            rel = jnp.maximum(
                _dot_nt(kc, qs_ref[(A_HEADS + h) * tq:(A_HEADS + h + 1) * tq, :]), 0.0)
            term = rel * w_rows[SC_IW + h:SC_IW + h + 1, :]
            score = term if score is None else score + term
        score = jnp.where(score == 0.0, 0.0, score)
        score = jnp.where(key_pos + c * ck <= qry_pos, score, -jnp.inf)
        bits = lax.bitcast_convert_type(score, jnp.int32)
        key_ref[c] = jnp.where(bits < 0, bits ^ 0x7FFFFFFF, bits)
        return 0

    lax.fori_loop(0, nck, score_chunk, 0)

    def count(pred):
        def body(c, acc):
            for r0 in range(0, ck, SLAB):
                acc = acc + jnp.where(pred(key_ref[c, r0:r0 + SLAB, :]), 1.0, 0.0)
            return acc
        acc = lax.fori_loop(0, nck, body, jnp.zeros((SLAB, tq), F32))
        return jnp.sum(acc, axis=0, keepdims=True)

    kf = float(top_k)

    def bit_step(b, thr):
        cand = thr + lax.shift_left(jnp.int32(1), 31 - b)
        cand_slab = jnp.broadcast_to(cand, (SLAB, tq))
        cnt = count(lambda kc: kc >= cand_slab)
        return jnp.where(cnt >= kf, cand, thr)

    thr = lax.fori_loop(0, 32, bit_step, jnp.full((1, tq), INT_MIN, jnp.int32))
    thr = jnp.maximum(thr, KEY_NEG_INF + 1)
    thr_slab = jnp.broadcast_to(thr, (SLAB, tq))
    cnt_ge = count(lambda kc: kc >= thr_slab)

    def write_bias(c, _):
        bias_ref[c] = jnp.where(key_ref[c] >= thr, 0.0, NEG_MASK)
        return 0

    lax.fori_loop(0, nck, write_bias, 0)

    @pl.when(jnp.max(cnt_ge) > kf)
    def _():
        need = kf - count(lambda kc: kc > thr_slab)
        r = lax.broadcasted_iota(jnp.int32, (ck, ck), 0)
        cidx = lax.broadcasted_iota(jnp.int32, (ck, ck), 1)
        lower = jnp.where(cidx <= r, 1.0, 0.0).astype(BF16)

        def tie_chunk(c, before):
            kc = key_ref[c]
            tie = kc == thr
            rank = before + _dot(lower, jnp.where(tie, 1.0, 0.0).astype(BF16))
            keep = (kc > thr) | (tie & (rank <= need))
            bias_ref[c] = jnp.where(keep, 0.0, NEG_MASK)
            return rank[ck - 1:ck, :]

        lax.fori_loop(0, nck, tie_chunk, jnp.zeros((1, tq), F32))

    m_ref[...] = jnp.full(m_ref.shape, NEG_BIG, F32)
    acc_ref[...] = jnp.zeros(acc_ref.shape, F32)
    reps = ck // 8

    n_att = A_HEADS * tq

    def attend(c, _):
        s_ref[...] = _dot_nt(kk_ref[0, key_rows(c), :], qs_ref[0:n_att, :])
        for tile in range(n_att // LANES):
            cols = slice(tile * LANES, (tile + 1) * LANES)
            qcols = slice((tile * LANES) % tq, (tile * LANES) % tq + LANES)
            part = None
            for r0 in range(0, ck, EXP_ROWS):
                rows = slice(r0, r0 + EXP_ROWS)
                s = s_ref[rows, cols] + bias_ref[c, rows, qcols]
                s_ref[rows, cols] = s
                unit = jnp.max(s.reshape(EXP_ROWS // 8, 8, LANES), axis=0)
                part = unit if part is None else jnp.maximum(part, unit)
            m_old = m_ref[:, cols]
            m_new = jnp.maximum(m_old, jnp.max(part, axis=0, keepdims=True))
            alpha_ref[:, cols] = jnp.exp(m_old - m_new)
            m_ref[:, cols] = m_new
        for tile in range(n_att // LANES):
            cols = slice(tile * LANES, (tile + 1) * LANES)
            for r0 in range(0, ck, EXP_ROWS):
                rows = slice(r0, r0 + EXP_ROWS)
                p = jnp.exp(s_ref[rows, cols] - jnp.tile(m_ref[:, cols], (EXP_ROWS // 8, 1)))
                p_ref[rows, cols] = p.astype(BF16)
        acc_ref[...] = (jnp.tile(alpha_ref[...], (LANES // 8, 1)) * acc_ref[...]
                        + _dot(vt_ref[c], p_ref[...]))
        return 0

    lax.fori_loop(0, nck, attend, 0)

    value_rows = lax.broadcasted_iota(jnp.int32, (LANES, tq), 0) < HEAD_DIM
    for j in range(A_HEADS // 2):
        outs = []
        for h in (2 * j, 2 * j + 1):
            cols = slice(h * tq, (h + 1) * tq)
            out_t = jnp.where(value_rows,
                              acc_ref[:, cols] / acc_ref[SUM_ROW:SUM_ROW + 1, cols], 0.0)
            outs.append(out_t.T)
        o_ref[0, :, j * LANES:(j + 1) * LANES] = (
            outs[0] + pltpu.roll(outs[1], HEAD_DIM, 1)).astype(o_ref.dtype)


def _dsa_t(a_grp, sc, top_k):
    b, s, _ = a_grp.shape
    tq = min(DSA_TQ, s)
    ck = min(DSA_CK, s)
    return pl.pallas_call(
        functools.partial(_dsa_t_kernel, top_k=top_k, ck=ck),
        grid=(b, s // tq),
        in_specs=[
            pl.BlockSpec((1, tq, 512), lambda bi, i: (bi, i, 0)),
            pl.BlockSpec((1, tq, 256), lambda bi, i: (bi, i, 2)),
            pl.BlockSpec((1, s, LANES), lambda bi, i: (bi, 0, 6)),
            pl.BlockSpec((1, s, LANES), lambda bi, i: (bi, 0, 7)),
            pl.BlockSpec((1, tq, LANES), lambda bi, i: (bi, i, 0)),
        ],
        out_specs=pl.BlockSpec((1, tq, BRANCH), lambda bi, i: (bi, i, 0)),
        out_shape=jax.ShapeDtypeStruct((b, s, BRANCH), BF16),
        scratch_shapes=[
            pltpu.VMEM((s // ck, LANES, ck), BF16),
            pltpu.VMEM((s // ck, ck, tq), jnp.int32),
            pltpu.VMEM((s // ck, ck, tq), F32),
            pltpu.VMEM(((A_HEADS + IDX_HEADS) * tq, LANES), BF16),
            pltpu.VMEM((8, A_HEADS * tq), F32),
            pltpu.VMEM((8, A_HEADS * tq), F32),
            pltpu.VMEM((LANES, A_HEADS * tq), F32),
            pltpu.VMEM((ck, A_HEADS * tq), F32),
            pltpu.VMEM((ck, A_HEADS * tq), BF16),
        ],
        compiler_params=_cparams(("parallel", "arbitrary")),
        name="dsa_attention",
    )(a_grp, a_grp, a_grp, a_grp, sc)


FOX_T = 256
BIAS_LANE = HEAD_DIM
BIAS_TERMS = 3


def _fox_t_kernel(q_ref, k_ref, v_ref, sc_ref, bf_ref, o_ref,
                  kaug_ref, vt_ref, qz_ref, m_ref, l_ref, alpha_ref, acc_ref, s_ref, p_ref, *, t):
    i = pl.program_id(1)
    seq = k_ref.shape[1]
    n_chunks = seq // t
    pairs = B_HEADS // 2
    lane = lax.broadcasted_iota(jnp.int32, (t, LANES), 1)
    lo_half = lane < HEAD_DIM

    def bias_lanes(h):
        first = BIAS_LANE + BIAS_TERMS * h
        return (lane >= first) & (lane < first + BIAS_TERMS)

    @pl.when(i == 0)
    def _():
        r = lax.broadcasted_iota(jnp.int32, (t, t), 0)
        c = lax.broadcasted_iota(jnp.int32, (t, t), 1)
        lower = jnp.where(c <= r, 1.0, 0.0)
        src = lax.broadcasted_iota(jnp.int32, (LANES, LANES), 0)
        dst = lax.broadcasted_iota(jnp.int32, (LANES, LANES), 1)

        def build(n, carry):
            rows = pl.ds(pl.multiple_of(n * t, t), t)
            x = sc_ref[0, rows, :] + bf_ref[...]
            log_f = jnp.minimum(x, 0.0) - jnp.log1p(jnp.exp(-jnp.abs(x)))
            cum = _dot_hi(lower, log_f) + carry
            nc = -cum
            hi = nc.astype(BF16)
            rest = nc - hi.astype(F32)
            mid = rest.astype(BF16)
            lo = (rest - mid.astype(F32)).astype(BF16)
            placed = None
            for j, term in enumerate((hi, mid, lo)):
                place = jnp.where((src >= SC_BF) & (src < SC_BF + B_HEADS)
                                  & (dst == BIAS_LANE + BIAS_TERMS * (src - SC_BF) + j), 1.0, 0.0)
                part = _dot(term, place.astype(BF16))
                placed = part if placed is None else placed + part
            for h in range(B_HEADS):
                hp = h // 2
                kt = k_ref[0, rows, hp * LANES:(hp + 1) * LANES].astype(F32)
                if h % 2:
                    kt = pltpu.roll(kt, HEAD_DIM, 1)
                tile = jnp.where(lo_half, kt, jnp.where(bias_lanes(h), placed, 0.0))
                kaug_ref[h, rows, :] = tile.astype(BF16)
            for hp in range(pairs):
                vt_ref[hp, n] = v_ref[0, rows, hp * LANES:(hp + 1) * LANES].astype(F32).T.astype(BF16)
            return cum[t - 1:t, :]

        lax.fori_loop(0, n_chunks, build, jnp.zeros((1, LANES), F32))

    for hp in range(pairs):
        q = q_ref[0, :, hp * LANES:(hp + 1) * LANES].astype(F32) * (HEAD_DIM ** -0.5)
        qz_ref[(2 * hp) * t:(2 * hp + 1) * t, :] = jnp.where(
            lo_half, q, jnp.where(bias_lanes(2 * hp), 1.0, 0.0)).astype(BF16)
        qz_ref[(2 * hp + 1) * t:(2 * hp + 2) * t, :] = jnp.where(
            lo_half, pltpu.roll(q, HEAD_DIM, 1), jnp.where(bias_lanes(2 * hp + 1), 1.0, 0.0)).astype(BF16)
    causal = (lax.broadcasted_iota(jnp.int32, (EXP_ROWS, LANES), 0)
              <= lax.broadcasted_iota(jnp.int32, (EXP_ROWS, LANES), 1))
    m_ref[...] = jnp.full(m_ref.shape, NEG_BIG, F32)
    l_ref[...] = jnp.zeros(l_ref.shape, F32)
    acc_ref[...] = jnp.zeros(acc_ref.shape, F32)

    def logits(c, slot):
        rows = pl.ds(pl.multiple_of(c * t, t), t)
        for h in range(B_HEADS):
            cols = slice(h * t, (h + 1) * t)
            s_ref[slot, :, cols] = _dot_nt(kaug_ref[h, rows, :], qz_ref[cols, :])

    def attend(c, slot, masked):
        if not masked:
            logits(c + 1, 1 - slot)
        def units(tile):
            q0 = (tile * LANES) % t
            for r0 in range(0, t, EXP_ROWS):
                if masked and r0 > q0 + LANES - 1:
                    yield r0, "skip"
                elif masked and r0 + EXP_ROWS - 1 > q0:
                    yield r0, "diagonal"
                else:
                    yield r0, "full"

        for tile in range(B_HEADS * t // LANES):
            cols = slice(tile * LANES, (tile + 1) * LANES)
            part = None
            for r0, kind in units(tile):
                if kind == "skip":
                    continue
                rows = slice(r0, r0 + EXP_ROWS)
                s = s_ref[slot, rows, cols]
                if kind == "diagonal":
                    s = jnp.where(causal, s, -jnp.inf)
                    s_ref[slot, rows, cols] = s
                unit = jnp.max(s.reshape(EXP_ROWS // 8, 8, LANES), axis=0)
                part = unit if part is None else jnp.maximum(part, unit)
            m_old = m_ref[:, cols]
            m_new = jnp.maximum(m_old, jnp.max(part, axis=0, keepdims=True))
            alpha_ref[:, cols] = jnp.exp(m_old - m_new)
            m_ref[:, cols] = m_new
        for tile in range(B_HEADS * t // LANES):
            cols = slice(tile * LANES, (tile + 1) * LANES)
            total = None
            for r0, kind in units(tile):
                rows = slice(r0, r0 + EXP_ROWS)
                if kind == "skip":
                    p_ref[rows, cols] = jnp.zeros((EXP_ROWS, LANES), BF16)
                    continue
                p = jnp.exp(s_ref[slot, rows, cols] - jnp.tile(m_ref[:, cols], (EXP_ROWS // 8, 1)))
                p_ref[rows, cols] = p.astype(BF16)
                unit = jnp.sum(p.reshape(EXP_ROWS // 8, 8, LANES), axis=0)
                total = unit if total is None else total + unit
            l_ref[:, cols] = (alpha_ref[:, cols] * l_ref[:, cols]
                              + jnp.sum(total, axis=0, keepdims=True))
        for h in range(B_HEADS):
            cols = slice(h * t, (h + 1) * t)
            acc_ref[:, cols] = (jnp.tile(alpha_ref[:, cols], (LANES // 8, 1)) * acc_ref[:, cols]
                                + _dot(vt_ref[h // 2, c], p_ref[:, cols]))

    def attend_pair(k, _):
        attend(2 * k, 0, masked=False)
        attend(2 * k + 1, 1, masked=False)
        return 0

    logits(0, 0)
    lax.fori_loop(0, i // 2, attend_pair, 0)

    @pl.when(i % 2 == 0)
    def _():
        attend(i, 0, masked=True)

    @pl.when(i % 2 == 1)
    def _():
        attend(i - 1, 0, masked=False)
        attend(i, 1, masked=True)

    for hp in range(pairs):
        outs = []
        for h in (2 * hp, 2 * hp + 1):
            cols = slice(h * t, (h + 1) * t)
            out_t = acc_ref[:, cols] / jnp.tile(l_ref[:, cols], (LANES // 8, 1))
            outs.append(out_t.T)
        o_ref[0, :, hp * LANES:(hp + 1) * LANES] = jnp.where(
            lo_half, outs[0], outs[1]).astype(o_ref.dtype)


def _fox_t(b_grp, sc, bf_row):
    b, s, _ = b_grp.shape
    t = min(FOX_T, s)
    pairs = B_HEADS // 2
    return pl.pallas_call(
        functools.partial(_fox_t_kernel, t=t),
        grid=(b, s // t),
        in_specs=[
            pl.BlockSpec((1, t, BRANCH), lambda bi, i: (bi, i, 0)),
            pl.BlockSpec((1, s, BRANCH), lambda bi, i: (bi, 0, 1)),
            pl.BlockSpec((1, s, BRANCH), lambda bi, i: (bi, 0, 2)),
            pl.BlockSpec((1, s, LANES), lambda bi, i: (bi, 0, 0)),
            _resident((1, LANES)),
        ],
        out_specs=pl.BlockSpec((1, t, BRANCH), lambda bi, i: (bi, i, 0)),
        out_shape=jax.ShapeDtypeStruct((b, s, BRANCH), BF16),
        scratch_shapes=[
            pltpu.VMEM((B_HEADS, s, LANES), BF16),
            pltpu.VMEM((pairs, s // t, LANES, t), BF16),
            pltpu.VMEM((B_HEADS * t, LANES), BF16),
            pltpu.VMEM((8, B_HEADS * t), F32),
            pltpu.VMEM((8, B_HEADS * t), F32),
            pltpu.VMEM((8, B_HEADS * t), F32),
            pltpu.VMEM((LANES, B_HEADS * t), F32),
            pltpu.VMEM((2, t, B_HEADS * t), F32),
            pltpu.VMEM((t, B_HEADS * t), BF16),
        ],
        compiler_params=_cparams(("parallel", "arbitrary")),
        name="fox_attention",
    )(b_grp, b_grp, b_grp, sc, bf_row)


GDN_TS = 512
GDN_C = 128
TAIL = 8
NEUMANN_ROUNDS = 6


def _gdn_kernel(x_ref, z_ref, sc_ref, cw_ref, alog_ref, dtb_ref, dn_ref, o_ref,
                state_ref, tail_ref, ext_ref, act_ref, np_ref, rhs_ref,
                qk_ref, qg_ref, kdt_ref, dec_ref, *, ts):
    j = pl.program_id(1)
    width = x_ref.shape[2]
    n_chunks = ts // GDN_C
    chains = [(h, n) for n in range(n_chunks) for h in range(C_HEADS)]

    @pl.when(j == 0)
    def _():
        state_ref[...] = jnp.zeros_like(state_ref)
        tail_ref[...] = jnp.zeros_like(tail_ref)

    x = x_ref[0]
    ext_ref[0:TAIL, :] = tail_ref[...]
    ext_ref[TAIL:TAIL + ts, :] = x
    tail_ref[...] = x[ts - TAIL:ts, :]
    conv = jnp.zeros((ts, width), F32)
    for tap in range(CONV_WIDTH):
        start = TAIL - (CONV_WIDTH - 1) + tap
        conv = conv + cw_ref[tap:tap + 1, :] * ext_ref[start:start + ts, :]
    act_ref[...] = _silu(conv)

    sc = sc_ref[0]
    beta_all = jax.nn.sigmoid(sc)
    xg = sc + dtb_ref[...]
    softplus = jnp.maximum(xg, 0.0) + jnp.log1p(jnp.exp(-jnp.abs(xg)))
    g_all = -jnp.exp(alog_ref[...]) * softplus
    r = lax.broadcasted_iota(jnp.int32, (ts, ts), 0)
    c = lax.broadcasted_iota(jnp.int32, (ts, ts), 1)
    same_chunk_lower = jnp.where((c <= r) & (r // GDN_C == c // GDN_C), 1.0, 0.0)
    gc_all = _dot_hi(same_chunk_lower, g_all)
    gc_t = gc_all.T

    ri = lax.broadcasted_iota(jnp.int32, (GDN_C, GDN_C), 0)
    ci = lax.broadcasted_iota(jnp.int32, (GDN_C, GDN_C), 1)
    incl = ri >= ci
    strict = ri > ci

    def l2n(t):
        return t * lax.rsqrt(jnp.sum(t * t, axis=-1, keepdims=True) + L2_EPS)

    for idx, (h, n) in enumerate(chains):
        rows = slice(n * GDN_C, (n + 1) * GDN_C)
        q = l2n(act_ref[rows, h * C_DIM:(h + 1) * C_DIM]) * (C_DIM ** -0.5)
        k = l2n(act_ref[rows, BRANCH + h * C_DIM:BRANCH + (h + 1) * C_DIM])
        v = act_ref[rows, 2 * BRANCH + h * C_DIM:2 * BRANCH + (h + 1) * C_DIM]
        beta = beta_all[rows, SC_BETA + h:SC_BETA + h + 1]
        gcol = gc_all[rows, SC_DECAY + h:SC_DECAY + h + 1]
        grow = gc_t[SC_DECAY + h:SC_DECAY + h + 1, n * GDN_C:(n + 1) * GDN_C]
        g_last = gcol[GDN_C - 1:GDN_C, :]
        decay = jnp.exp(jnp.where(incl, gcol - grow, -jnp.inf))
        kb = k * beta
        k16 = k.astype(BF16)
        q16 = q.astype(BF16)
        a_mat = jnp.where(strict, _dot_nt(kb.astype(BF16), k16) * decay, 0.0)
        np_ref[idx, :, 0:GDN_C] = -a_mat
        np_ref[idx, :, GDN_C:2 * GDN_C] = _dot_solve(a_mat, a_mat)
        rhs_ref[idx, :, 0:C_DIM] = v * beta
        rhs_ref[idx, :, C_DIM:2 * C_DIM] = kb * jnp.exp(gcol)
        qk_ref[idx] = jnp.where(incl, _dot_nt(q16, k16) * decay, 0.0).astype(BF16)
        qg_ref[idx] = (q * jnp.exp(gcol)).astype(BF16)
        kdt_ref[idx] = (k * jnp.exp(g_last - gcol)).T.astype(BF16)
        dec_ref[idx] = jnp.broadcast_to(jnp.exp(g_last), (1, C_DIM))

    for rnd in range(1, NEUMANN_ROUNDS + 1):
        last = rnd == NEUMANN_ROUNDS
        for idx in range(len(chains)):
            n_old = np_ref[idx, :, 0:GDN_C]
            p_old = np_ref[idx, :, GDN_C:2 * GDN_C]
            if last:
                np_ref[idx, :, 0:GDN_C] = n_old + p_old + _dot_solve(p_old, n_old)
            else:
                prod = _dot_solve(p_old, np_ref[idx])
                np_ref[idx, :, 0:GDN_C] = n_old + p_old + prod[:, 0:GDN_C]
                np_ref[idx, :, GDN_C:2 * GDN_C] = prod[:, GDN_C:2 * GDN_C]

    for idx in range(len(chains)):
        rhs = rhs_ref[idx]
        rhs_ref[idx] = rhs + _dot_solve(np_ref[idx, :, 0:GDN_C], rhs)

    for idx, (h, n) in enumerate(chains):
        rows = slice(n * GDN_C, (n + 1) * GDN_C)
        state = state_ref[h]
        s16 = state.astype(BF16)
        v_new = rhs_ref[idx, :, 0:C_DIM] - _dot(rhs_ref[idx, :, C_DIM:2 * C_DIM].astype(BF16), s16)
        vn16 = v_new.astype(BF16)
        o = _dot(qg_ref[idx], s16) + _dot(qk_ref[idx], vn16)
        state_ref[h] = state * dec_ref[idx] + _dot(kdt_ref[idx], vn16)
        z = z_ref[0, rows, h * C_DIM:(h + 1) * C_DIM]
        y = _rms(o, dn_ref[...]) * _silu(z)
        o_ref[0, rows, h * C_DIM:(h + 1) * C_DIM] = y.astype(o_ref.dtype)


def _gdn(cqkv, cz, sc, conv_w, alog_row, dtb_row, dn_row):
    b, s, width = cqkv.shape
    ts = min(GDN_TS, s)
    nch = C_HEADS * (ts // GDN_C)
    blk = lambda bi, j: (bi, j, 0)
    return pl.pallas_call(
        functools.partial(_gdn_kernel, ts=ts),
        grid=(b, s // ts),
        in_specs=[
            pl.BlockSpec((1, ts, width), blk),
            pl.BlockSpec((1, ts, BRANCH), blk),
            pl.BlockSpec((1, ts, LANES), blk),
            _resident((CONV_WIDTH, width)),
            _resident((1, LANES)),
            _resident((1, LANES)),
            _resident((1, C_DIM)),
        ],
        out_specs=pl.BlockSpec((1, ts, BRANCH), blk),
        out_shape=jax.ShapeDtypeStruct((b, s, BRANCH), BF16),
        scratch_shapes=[
            pltpu.VMEM((C_HEADS, C_DIM, C_DIM), F32),
            pltpu.VMEM((TAIL, width), F32),
            pltpu.VMEM((TAIL + ts, width), F32),
            pltpu.VMEM((ts, width), F32),
            pltpu.VMEM((nch, GDN_C, 2 * GDN_C), F32),
            pltpu.VMEM((nch, GDN_C, 2 * C_DIM), F32),
            pltpu.VMEM((nch, GDN_C, GDN_C), BF16),
            pltpu.VMEM((nch, GDN_C, C_DIM), BF16),
            pltpu.VMEM((nch, C_DIM, GDN_C), BF16),
            pltpu.VMEM((nch, 1, C_DIM), F32),
        ],
        compiler_params=_cparams(("parallel", "arbitrary")),
        name="gated_deltanet",
    )(cqkv, cz, sc, conv_w, alog_row, dtb_row, dn_row)


MERGE_TM = 512


def _merge_kernel(x_ref, ya_ref, yb_ref, yc_ref, gl_ref, bg_ref,
                  wa_ref, wb_ref, wc_ref, wo_ref, o_ref):
    merged = None
    for n, (y_ref, w_ref) in enumerate(((ya_ref, wa_ref), (yb_ref, wb_ref), (yc_ref, wc_ref))):
        cols = slice(n * D_MODEL, (n + 1) * D_MODEL)
        gate = jax.nn.sigmoid(gl_ref[:, cols] + bg_ref[:, cols])
        term = gate * _dot(y_ref[...], w_ref[...])
        merged = term if merged is None else merged + term
    o_ref[...] = x_ref[...] + _dot(merged.astype(BF16), wo_ref[...])


def _merge(x, ya, yb, yc, gl, b_gate, wa, wb, wc, wo):
    n = x.shape[0]
    tm = min(MERGE_TM, n)
    row = lambda i: (i, 0)
    return pl.pallas_call(
        _merge_kernel,
        grid=(n // tm,),
        in_specs=[
            pl.BlockSpec((tm, D_MODEL), row),
            pl.BlockSpec((tm, BRANCH), row),
            pl.BlockSpec((tm, BRANCH), row),
            pl.BlockSpec((tm, BRANCH), row),
            pl.BlockSpec((tm, 3 * D_MODEL), row),
            _resident((1, 3 * D_MODEL)),
            _resident((BRANCH, D_MODEL)),
            _resident((BRANCH, D_MODEL)),
            _resident((BRANCH, D_MODEL)),
            _resident((D_MODEL, D_MODEL)),
        ],
        out_specs=pl.BlockSpec((tm, D_MODEL), row),
        out_shape=jax.ShapeDtypeStruct((n, D_MODEL), F32),
        compiler_params=_cparams(("parallel",)),
        name="gated_merge",
    )(x, ya, yb, yc, gl, b_gate, wa, wb, wc, wo)


def _w_in_pieces():
    offs = np.concatenate([[0], np.cumsum(IN_SIZES)])
    names = ("a_q", "a_k", "a_v", "i_q", "i_k", "i_w", "b_qkv", "b_f", "c_qkv", "c_z",
             "c_beta", "c_a", "gates")
    src = {n: (int(offs[i]), int(IN_SIZES[i])) for i, n in enumerate(names)}
    order = (("a_q", 0), ("i_q", 512), ("a_k", 768), ("i_k", 832), ("a_v", 896),
             ("b_f", W_A + SC_BF), ("i_w", W_A + SC_IW), ("c_beta", W_A + SC_BETA),
             ("c_a", W_A + SC_DECAY), ("b_qkv", W_A + W_SC), ("c_qkv", W_A + W_SC + W_B),
             ("c_z", W_A + W_SC + W_B + W_CQKV), ("gates", W_A + W_SC + W_B + W_CQKV + W_CZ))
    pieces = []
    for name, dst in order:
        s0, width = src[name]
        done = 0
        while done < width:
            step = min(width - done, LANES - (dst + done) % LANES)
            pieces.append((dst + done, s0 + done, step))
            done += step
    return pieces


RELAYOUT_TM = 256


def _relayout_kernel(w_ref, tail_ref, o_ref, *, aligned_cols):
    by_tile = {}
    for dst, src, n in _w_in_pieces():
        by_tile.setdefault(dst // LANES, []).append((dst % LANES, src, n))
    row = lax.broadcasted_iota(jnp.int32, (2 * LANES, LANES), 0)
    col = lax.broadcasted_iota(jnp.int32, (2 * LANES, LANES), 1)

    def src_tile(start):
        if start + LANES <= aligned_cols:
            return w_ref[0, :, start:start + LANES]
        return tail_ref[0, :, start - aligned_cols:start - aligned_cols + LANES]

    for tile in range(W_TOTAL // LANES):
        acc = None
        for dst_lane, src, n in by_tile.get(tile, ()):
            base = (src // LANES) * LANES
            shift = src - base - dst_lane
            select = (row == col + shift) & (col >= dst_lane) & (col < dst_lane + n)
            window = jnp.concatenate([src_tile(base), src_tile(base + LANES)], axis=1)
            part = _dot(window.astype(BF16), jnp.where(select, 1.0, 0.0).astype(BF16))
            acc = part if acc is None else acc + part
        if acc is None:
            acc = jnp.zeros((w_ref.shape[1], LANES), F32)
        o_ref[0, :, tile * LANES:(tile + 1) * LANES] = acc.astype(BF16)


def _layout_w_in(w_in):
    depth, rows, cols = w_in.shape
    aligned_cols = (cols // LANES) * LANES
    tail = jnp.pad(w_in[:, :, aligned_cols:], ((0, 0), (0, 0), (0, 2 * LANES - (cols - aligned_cols))))
    tm = min(RELAYOUT_TM, rows)
    return pl.pallas_call(
        functools.partial(_relayout_kernel, aligned_cols=aligned_cols),
        grid=(depth, rows // tm),
        in_specs=[pl.BlockSpec((1, tm, cols), lambda l, i: (l, i, 0)),
                  pl.BlockSpec((1, tm, 2 * LANES), lambda l, i: (l, i, 0))],
        out_specs=pl.BlockSpec((1, tm, W_TOTAL), lambda l, i: (l, i, 0)),
        out_shape=jax.ShapeDtypeStruct((depth, rows, W_TOTAL), BF16),
        compiler_params=_cparams(("parallel", "parallel")),
        name="w_in_relayout",
    )(w_in, tail)


def _rotary_tables(seq):
    pos = jnp.arange(seq, dtype=F32)
    inv_freq = jnp.power(ROPE_THETA, -jnp.arange(0, ROT_DIM, 2, dtype=F32) / ROT_DIM)
    ang = pos[:, None] * inv_freq[None, :]
    cos, sin = jnp.cos(ang), jnp.sin(ang)
    half = ROT_DIM // 2
    ones = jnp.ones((seq, HEAD_DIM - ROT_DIM), F32)
    zeros_h = jnp.zeros((seq, half), F32)
    zeros_r = jnp.zeros((seq, HEAD_DIM - ROT_DIM), F32)
    c64 = jnp.concatenate([cos, cos, ones], axis=1)
    s1_64 = jnp.concatenate([zeros_h, sin, zeros_r], axis=1)
    s2_64 = jnp.concatenate([-sin, zeros_h, zeros_r], axis=1)
    twice = lambda t: jnp.concatenate([t, t], axis=1)
    return twice(c64), twice(s1_64), twice(s2_64)


def _lane_row(values, start):
    return jnp.zeros((1, LANES), F32).at[0, start:start + values.shape[0]].set(values.astype(F32))


def kernel(x, ffn1_norm, ffn1_w_in, ffn1_w_out, mix_norm, w_in, b_gate, b_forget, conv_w, a_log, dt_bias, delta_norm, w_branch_a, w_branch_b, w_branch_c, w_out, ffn2_norm, ffn2_w_in, ffn2_w_out, final_norm):
    batch, seq, _ = x.shape
    depth = w_in.shape[0]
    top_k = min(INDEX_TOPK, seq // 4)
    cos, s1, s2 = _rotary_tables(seq)
    w_mix = _layout_w_in(w_in)
    final_row = final_norm.reshape(1, D_MODEL)
    xt = x.reshape(batch * seq, D_MODEL)
    for l in range(depth):
        xt = _ffn(xt, ffn1_norm[l].reshape(1, D_MODEL), ffn1_w_in[l].astype(BF16),
                  ffn1_w_out[l].astype(BF16), final_row, False)
        a_grp, sc, b_grp, cqkv, cz, gl = _proj(
            xt, mix_norm[l].reshape(1, D_MODEL), w_mix[l], cos, s1, s2, seq)
        a_grp = a_grp.reshape(batch, seq, W_A)
        sc = sc.reshape(batch, seq, W_SC)
        y_a = _dsa_t(a_grp, sc, top_k)
        y_b = _fox_t(b_grp.reshape(batch, seq, W_B), sc, _lane_row(b_forget[l], SC_BF))
        y_c = _gdn(cqkv.reshape(batch, seq, W_CQKV), cz.reshape(batch, seq, W_CZ), sc,
                   conv_w[l], _lane_row(a_log[l], SC_DECAY), _lane_row(dt_bias[l], SC_DECAY),
                   delta_norm[l].reshape(1, C_DIM))
        flat = lambda t: t.reshape(batch * seq, BRANCH)
        xt = _merge(xt, flat(y_a), flat(y_b), flat(y_c), gl, b_gate[l].reshape(1, 3 * D_MODEL),
                    w_branch_a[l].astype(BF16), w_branch_b[l].astype(BF16),
                    w_branch_c[l].astype(BF16), w_out[l].astype(BF16))
        xt = _ffn(xt, ffn2_norm[l].reshape(1, D_MODEL), ffn2_w_in[l].astype(BF16),
                  ffn2_w_out[l].astype(BF16), final_row, l == depth - 1)
    return xt.reshape(batch, seq, D_MODEL)
```

```python
import functools

import numpy as np
import jax
import jax.numpy as jnp
from jax import lax
from jax.experimental import pallas as pl
from jax.experimental.pallas import tpu as pltpu

F32 = jnp.float32
BF16 = jnp.bfloat16

D_MODEL = 1024
BRANCH = 512
A_HEADS = 8
HEAD_DIM = 64
IDX_HEADS = 4
INDEX_TOPK = 256
B_HEADS = 8
C_HEADS = 4
C_DIM = 128
CONV_WIDTH = 4
ROPE_THETA = 500000.0
ROT_DIM = 16
FFN_DIM = 2048
NORM_EPS = 1e-6
L2_EPS = 1e-6
IN_SIZES = (512, 64, 64, 256, 64, 4, 1536, 8, 1536, 512, 4, 4, 3072)

LANES = 128
SC_BF = 0
SC_IW = 8
SC_BETA = 12
SC_DECAY = 16
W_A, W_SC, W_B, W_CQKV, W_CZ, W_G = 1024, 128, 1536, 1536, 512, 3072
W_TOTAL = W_A + W_SC + W_B + W_CQKV + W_CZ + W_G
ROT_TILES = 7

NEG_BIG = -1e30
NEG_MASK = -2e30
VMEM_LIMIT = 56 * 1024 * 1024


def _cparams(sem):
    return pltpu.CompilerParams(dimension_semantics=sem, vmem_limit_bytes=VMEM_LIMIT)


def _dot(a, b):
    return jnp.dot(a, b, preferred_element_type=F32)


def _dot_hi(a, b):
    return jnp.dot(a, b, preferred_element_type=F32, precision=lax.Precision.HIGHEST)


def _dot_solve(a, b):
    a_hi = a.astype(BF16)
    b_hi = b.astype(BF16)
    a_lo = (a - a_hi.astype(F32)).astype(BF16)
    b_lo = (b - b_hi.astype(F32)).astype(BF16)
    m = a.shape[0]
    top = _dot(jnp.concatenate([a_hi, a_lo], axis=0), b_hi)
    return top[0:m] + top[m:2 * m] + _dot(a_hi, b_lo)


def _dot_nt(a, b):
    return lax.dot_general(a, b, (((1,), (1,)), ((), ())), preferred_element_type=F32)


def _rms(x, gain):
    return x * lax.rsqrt(jnp.mean(x * x, axis=-1, keepdims=True) + NORM_EPS) * gain


def _silu(x):
    return x * jax.nn.sigmoid(x)


def _resident(shape):
    nd = len(shape)
    return pl.BlockSpec(shape, lambda *_: (0,) * nd, pipeline_mode=pl.Buffered(1))


FFN_TM = 512
FFN_CHUNK = 512


def _ffn_kernel(x_ref, g_ref, win_ref, wout_ref, fg_ref, o_ref, *, final):
    x = x_ref[...]
    h = _rms(x, g_ref[...]).astype(BF16)
    acc = jnp.zeros(x.shape, F32)
    for c in range(FFN_DIM // FFN_CHUNK):
        lo = c * FFN_CHUNK
        gate = _dot(h, win_ref[:, lo:lo + FFN_CHUNK])
        up = _dot(h, win_ref[:, FFN_DIM + lo:FFN_DIM + lo + FFN_CHUNK])
        act = (_silu(gate) * up).astype(BF16)
        acc = acc + _dot(act, wout_ref[lo:lo + FFN_CHUNK, :])
    y = x + 0.5 * acc
    if final:
        y = _rms(y, fg_ref[...])
    o_ref[...] = y


def _ffn(x, gain, w_in, w_out, final_gain, final):
    n = x.shape[0]
    tm = min(FFN_TM, n)
    return pl.pallas_call(
        functools.partial(_ffn_kernel, final=final),
        grid=(n // tm,),
        in_specs=[
            pl.BlockSpec((tm, D_MODEL), lambda i: (i, 0)),
            _resident((1, D_MODEL)),
            _resident((D_MODEL, 2 * FFN_DIM)),
            _resident((FFN_DIM, D_MODEL)),
            _resident((1, D_MODEL)),
        ],
        out_specs=pl.BlockSpec((tm, D_MODEL), lambda i: (i, 0)),
        out_shape=jax.ShapeDtypeStruct((n, D_MODEL), F32),
        compiler_params=_cparams(("parallel",)),
        name="ffn_half",
    )(x, gain, w_in, w_out, final_gain)


PROJ_TM = 256
PROJ_CHUNK = 512


def _proj_kernel(x_ref, g_ref, w_ref, cos_ref, s1_ref, s2_ref,
                 a_ref, sc_ref, b_ref, cqkv_ref, cz_ref, gl_ref):
    h = _rms(x_ref[...], g_ref[...]).astype(BF16)
    cos, s1, s2 = cos_ref[...], s1_ref[...], s2_ref[...]

    off = 0
    for j in range(W_A // LANES):
        t = _dot(h, w_ref[:, off:off + LANES])
        if j < ROT_TILES:
            t = t * cos + pltpu.roll(t, 8, 1) * s1 + pltpu.roll(t, LANES - 8, 1) * s2
        a_ref[:, off:off + LANES] = t.astype(BF16)
        off += LANES
    sc_ref[...] = _dot(h, w_ref[:, off:off + W_SC])
    off += W_SC
    for ref, width in ((b_ref, W_B), (cqkv_ref, W_CQKV), (cz_ref, W_CZ), (gl_ref, W_G)):
        for lo in range(0, width, PROJ_CHUNK):
            ref[:, lo:lo + PROJ_CHUNK] = _dot(
                h, w_ref[:, off + lo:off + lo + PROJ_CHUNK]).astype(ref.dtype)
        off += width


def _proj(x, gain, w, cos, s1, s2, seq):
    n = x.shape[0]
    tm = min(PROJ_TM, seq)
    per_seq = seq // tm
    row = lambda i: (i, 0)
    tab = lambda i: (i % per_seq, 0)
    widths = (W_A, W_SC, W_B, W_CQKV, W_CZ, W_G)
    dtypes = (BF16, F32, BF16, F32, F32, F32)
    return pl.pallas_call(
        _proj_kernel,
        grid=(n // tm,),
        in_specs=[
            pl.BlockSpec((tm, D_MODEL), row),
            _resident((1, D_MODEL)),
            _resident((D_MODEL, W_TOTAL)),
            pl.BlockSpec((tm, LANES), tab),
            pl.BlockSpec((tm, LANES), tab),
            pl.BlockSpec((tm, LANES), tab),
        ],
        out_specs=[pl.BlockSpec((tm, wd), row) for wd in widths],
        out_shape=[jax.ShapeDtypeStruct((n, wd), dt) for wd, dt in zip(widths, dtypes)],
        compiler_params=_cparams(("parallel",)),
        name="mixer_proj",
    )(x, gain, w, cos, s1, s2)


DSA_TQ = 256
DSA_CK = 512
INT_MIN = -2 ** 31
KEY_NEG_INF = int(np.array(-np.inf, np.float32).view(np.int32)) ^ 0x7FFFFFFF
SLAB = 64
SUM_ROW = HEAD_DIM
EXP_ROWS = 128


def _dsa_t_kernel(qa_ref, qi_ref, kk_ref, vv_ref, sc_ref, o_ref,
                  vt_ref, key_ref, bias_ref, qs_ref, m_ref, alpha_ref, acc_ref,
                  s_ref, p_ref, *, top_k, ck):
    i = pl.program_id(1)
    tq = qa_ref.shape[1]
    n_chunks_total = key_ref.shape[0]
    nck = (i * tq + tq + ck - 1) // ck
    lane = lax.broadcasted_iota(jnp.int32, (tq, LANES), 1)
    lo_half = lane < HEAD_DIM
    scale = HEAD_DIM ** -0.5

    @pl.when(i == 0)
    def _():
        ones_row = lax.broadcasted_iota(jnp.int32, (LANES, ck), 0) == SUM_ROW
        for c in range(n_chunks_total):
            vt = vv_ref[0, c * ck:(c + 1) * ck, :].astype(F32).T
            vt_ref[c] = jnp.where(ones_row, 1.0, vt).astype(BF16)

    def put(h, t):
        qs_ref[h * tq:(h + 1) * tq, :] = (t * scale).astype(BF16)

    for j in range(A_HEADS // 2):
        t = qa_ref[0, :, j * LANES:(j + 1) * LANES].astype(F32)
        put(2 * j, jnp.where(lo_half, t, 0.0))
        put(2 * j + 1, jnp.where(lo_half, pltpu.roll(t, HEAD_DIM, 1), 0.0))
    for j in range(IDX_HEADS // 2):
        t = qi_ref[0, :, j * LANES:(j + 1) * LANES].astype(F32)
        put(A_HEADS + 2 * j, jnp.where(lo_half, 0.0, pltpu.roll(t, HEAD_DIM, 1)))
        put(A_HEADS + 2 * j + 1, jnp.where(lo_half, 0.0, t))

    w_rows = (sc_ref[0] * (IDX_HEADS ** -0.5)).T
    key_pos = lax.broadcasted_iota(jnp.int32, (ck, tq), 0)
    qry_pos = i * tq + lax.broadcasted_iota(jnp.int32, (ck, tq), 1)

    def key_rows(c):
        return pl.ds(pl.multiple_of(c * ck, ck), ck)

    def score_chunk(c, _):
        kc = kk_ref[0, key_rows(c), :]
        score = None
        for h in range(IDX_HEADS):
            rel = jnp.maximum(
                _dot_nt(kc, qs_ref[(A_HEADS + h) * tq:(A_HEADS + h + 1) * tq, :]), 0.0)
            term = rel * w_rows[SC_IW + h:SC_IW + h + 1, :]
            score = term if score is None else score + term
        score = jnp.where(score == 0.0, 0.0, score)
        score = jnp.where(key_pos + c * ck <= qry_pos, score, -jnp.inf)
        bits = lax.bitcast_convert_type(score, jnp.int32)
        key_ref[c] = jnp.where(bits < 0, bits ^ 0x7FFFFFFF, bits)
        return 0

    lax.fori_loop(0, nck, score_chunk, 0)

    def count(pred):
        def body(c, acc):
            for r0 in range(0, ck, SLAB):
                acc = acc + jnp.where(pred(key_ref[c, r0:r0 + SLAB, :]), 1.0, 0.0)
            return acc
        acc = lax.fori_loop(0, nck, body, jnp.zeros((SLAB, tq), F32))
        return jnp.sum(acc, axis=0, keepdims=True)

    kf = float(top_k)

    def bit_step(b, thr):
        cand = thr + lax.shift_left(jnp.int32(1), 31 - b)
        cand_slab = jnp.broadcast_to(cand, (SLAB, tq))
        cnt = count(lambda kc: kc >= cand_slab)
        return jnp.where(cnt >= kf, cand, thr)

    thr = lax.fori_loop(0, 32, bit_step, jnp.full((1, tq), INT_MIN, jnp.int32))
    thr = jnp.maximum(thr, KEY_NEG_INF + 1)
    thr_slab = jnp.broadcast_to(thr, (SLAB, tq))
    cnt_ge = count(lambda kc: kc >= thr_slab)

    def write_bias(c, _):
        bias_ref[c] = jnp.where(key_ref[c] >= thr, 0.0, NEG_MASK)
        return 0

    lax.fori_loop(0, nck, write_bias, 0)

    @pl.when(jnp.max(cnt_ge) > kf)
    def _():
        need = kf - count(lambda kc: kc > thr_slab)
        r = lax.broadcasted_iota(jnp.int32, (ck, ck), 0)
        cidx = lax.broadcasted_iota(jnp.int32, (ck, ck), 1)
        lower = jnp.where(cidx <= r, 1.0, 0.0).astype(BF16)

        def tie_chunk(c, before):
            kc = key_ref[c]
            tie = kc == thr
            rank = before + _dot(lower, jnp.where(tie, 1.0, 0.0).astype(BF16))
            keep = (kc > thr) | (tie & (rank <= need))
            bias_ref[c] = jnp.where(keep, 0.0, NEG_MASK)
            return rank[ck - 1:ck, :]

        lax.fori_loop(0, nck, tie_chunk, jnp.zeros((1, tq), F32))

    m_ref[...] = jnp.full(m_ref.shape, NEG_BIG, F32)
    acc_ref[...] = jnp.zeros(acc_ref.shape, F32)
    reps = ck // 8

    n_att = A_HEADS * tq

    def attend(c, _):
        s_ref[...] = _dot_nt(kk_ref[0, key_rows(c), :], qs_ref[0:n_att, :])
        for tile in range(n_att // LANES):
            cols = slice(tile * LANES, (tile + 1) * LANES)
            qcols = slice((tile * LANES) % tq, (tile * LANES) % tq + LANES)
            part = None
            for r0 in range(0, ck, EXP_ROWS):
                rows = slice(r0, r0 + EXP_ROWS)
                s = s_ref[rows, cols] + bias_ref[c, rows, qcols]
                s_ref[rows, cols] = s
                unit = jnp.max(s.reshape(EXP_ROWS // 8, 8, LANES), axis=0)
                part = unit if part is None else jnp.maximum(part, unit)
            m_old = m_ref[:, cols]
            m_new = jnp.maximum(m_old, jnp.max(part, axis=0, keepdims=True))
            alpha_ref[:, cols] = jnp.exp(m_old - m_new)
            m_ref[:, cols] = m_new
        for tile in range(n_att // LANES):
            cols = slice(tile * LANES, (tile + 1) * LANES)
            for r0 in range(0, ck, EXP_ROWS):
                rows = slice(r0, r0 + EXP_ROWS)
                p = jnp.exp(s_ref[rows, cols] - jnp.tile(m_ref[:, cols], (EXP_ROWS // 8, 1)))
                p_ref[rows, cols] = p.astype(BF16)
        acc_ref[...] = (jnp.tile(alpha_ref[...], (LANES // 8, 1)) * acc_ref[...]
                        + _dot(vt_ref[c], p_ref[...]))
        return 0

    lax.fori_loop(0, nck, attend, 0)

    value_rows = lax.broadcasted_iota(jnp.int32, (LANES, tq), 0) < HEAD_DIM
    for j in range(A_HEADS // 2):
        outs = []
        for h in (2 * j, 2 * j + 1):
            cols = slice(h * tq, (h + 1) * tq)
            out_t = jnp.where(value_rows,
                              acc_ref[:, cols] / acc_ref[SUM_ROW:SUM_ROW + 1, cols], 0.0)
            outs.append(out_t.T)
        o_ref[0, :, j * LANES:(j + 1) * LANES] = (
            outs[0] + pltpu.roll(outs[1], HEAD_DIM, 1)).astype(o_ref.dtype)


def _dsa_t(a_grp, sc, top_k):
    b, s, _ = a_grp.shape
    tq = min(DSA_TQ, s)
    ck = min(DSA_CK, s)
    return pl.pallas_call(
        functools.partial(_dsa_t_kernel, top_k=top_k, ck=ck),
        grid=(b, s // tq),
        in_specs=[
            pl.BlockSpec((1, tq, 512), lambda bi, i: (bi, i, 0)),
            pl.BlockSpec((1, tq, 256), lambda bi, i: (bi, i, 2)),
            pl.BlockSpec((1, s, LANES), lambda bi, i: (bi, 0, 6)),
            pl.BlockSpec((1, s, LANES), lambda bi, i: (bi, 0, 7)),
            pl.BlockSpec((1, tq, LANES), lambda bi, i: (bi, i, 0)),
        ],
        out_specs=pl.BlockSpec((1, tq, BRANCH), lambda bi, i: (bi, i, 0)),
        out_shape=jax.ShapeDtypeStruct((b, s, BRANCH), BF16),
        scratch_shapes=[
            pltpu.VMEM((s // ck, LANES, ck), BF16),
            pltpu.VMEM((s // ck, ck, tq), jnp.int32),
            pltpu.VMEM((s // ck, ck, tq), F32),
            pltpu.VMEM(((A_HEADS + IDX_HEADS) * tq, LANES), BF16),
            pltpu.VMEM((8, A_HEADS * tq), F32),
            pltpu.VMEM((8, A_HEADS * tq), F32),
            pltpu.VMEM((LANES, A_HEADS * tq), F32),
            pltpu.VMEM((ck, A_HEADS * tq), F32),
            pltpu.VMEM((ck, A_HEADS * tq), BF16),
        ],
        compiler_params=_cparams(("parallel", "arbitrary")),
        name="dsa_attention",
    )(a_grp, a_grp, a_grp, a_grp, sc)


FOX_T = 256
BIAS_LANE = HEAD_DIM
BIAS_TERMS = 3


def _fox_t_kernel(q_ref, k_ref, v_ref, sc_ref, bf_ref, o_ref,
                  kaug_ref, vt_ref, qz_ref, m_ref, l_ref, alpha_ref, acc_ref, s_ref, p_ref, *, t):
    i = pl.program_id(1)
    seq = k_ref.shape[1]
    n_chunks = seq // t
    pairs = B_HEADS // 2
    lane = lax.broadcasted_iota(jnp.int32, (t, LANES), 1)
    lo_half = lane < HEAD_DIM

    def bias_lanes(h):
        first = BIAS_LANE + BIAS_TERMS * h
        return (lane >= first) & (lane < first + BIAS_TERMS)

    @pl.when(i == 0)
    def _():
        r = lax.broadcasted_iota(jnp.int32, (t, t), 0)
        c = lax.broadcasted_iota(jnp.int32, (t, t), 1)
        lower = jnp.where(c <= r, 1.0, 0.0)
        src = lax.broadcasted_iota(jnp.int32, (LANES, LANES), 0)
        dst = lax.broadcasted_iota(jnp.int32, (LANES, LANES), 1)

        def build(n, carry):
            rows = pl.ds(pl.multiple_of(n * t, t), t)
            x = sc_ref[0, rows, :] + bf_ref[...]
            log_f = jnp.minimum(x, 0.0) - jnp.log1p(jnp.exp(-jnp.abs(x)))
            cum = _dot_hi(lower, log_f) + carry
            nc = -cum
            hi = nc.astype(BF16)
            rest = nc - hi.astype(F32)
            mid = rest.astype(BF16)
            lo = (rest - mid.astype(F32)).astype(BF16)
            placed = None
            for j, term in enumerate((hi, mid, lo)):
                place = jnp.where((src >= SC_BF) & (src < SC_BF + B_HEADS)
                                  & (dst == BIAS_LANE + BIAS_TERMS * (src - SC_BF) + j), 1.0, 0.0)
                part = _dot(term, place.astype(BF16))
                placed = part if placed is None else placed + part
            for h in range(B_HEADS):
                hp = h // 2
                kt = k_ref[0, rows, hp * LANES:(hp + 1) * LANES].astype(F32)
                if h % 2:
                    kt = pltpu.roll(kt, HEAD_DIM, 1)
                tile = jnp.where(lo_half, kt, jnp.where(bias_lanes(h), placed, 0.0))
                kaug_ref[h, rows, :] = tile.astype(BF16)
            for hp in range(pairs):
                vt_ref[hp, n] = v_ref[0, rows, hp * LANES:(hp + 1) * LANES].astype(F32).T.astype(BF16)
            return cum[t - 1:t, :]

        lax.fori_loop(0, n_chunks, build, jnp.zeros((1, LANES), F32))

    for hp in range(pairs):
        q = q_ref[0, :, hp * LANES:(hp + 1) * LANES].astype(F32) * (HEAD_DIM ** -0.5)
        qz_ref[(2 * hp) * t:(2 * hp + 1) * t, :] = jnp.where(
            lo_half, q, jnp.where(bias_lanes(2 * hp), 1.0, 0.0)).astype(BF16)
        qz_ref[(2 * hp + 1) * t:(2 * hp + 2) * t, :] = jnp.where(
            lo_half, pltpu.roll(q, HEAD_DIM, 1), jnp.where(bias_lanes(2 * hp + 1), 1.0, 0.0)).astype(BF16)
    causal = (lax.broadcasted_iota(jnp.int32, (t, t), 0)
              <= lax.broadcasted_iota(jnp.int32, (t, t), 1))
    m_ref[...] = jnp.full(m_ref.shape, NEG_BIG, F32)
    l_ref[...] = jnp.zeros(l_ref.shape, F32)
    acc_ref[...] = jnp.zeros(acc_ref.shape, F32)

    def logits(c, slot):
        rows = pl.ds(pl.multiple_of(c * t, t), t)
        for h in range(B_HEADS):
            cols = slice(h * t, (h + 1) * t)
            s_ref[slot, :, cols] = _dot_nt(kaug_ref[h, rows, :], qz_ref[cols, :])

    def attend(c, slot, masked):
        if not masked:
            logits(c + 1, 1 - slot)
        for h in range(B_HEADS):
            cols = slice(h * t, (h + 1) * t)
            s = s_ref[slot, :, cols]
            if masked:
                s = jnp.where(causal, s, -jnp.inf)
            m_old = m_ref[:, cols]
            m_new = jnp.maximum(m_old, jnp.max(s, axis=0, keepdims=True))
            alpha = jnp.exp(m_old - m_new)
            p = jnp.exp(s - jnp.tile(m_new, (t // 8, 1)))
            l_ref[:, cols] = alpha * l_ref[:, cols] + jnp.sum(p, axis=0, keepdims=True)
            m_ref[:, cols] = m_new
            alpha_ref[:, cols] = alpha
            p_ref[:, cols] = p.astype(BF16)
        for h in range(B_HEADS):
            cols = slice(h * t, (h + 1) * t)
            acc_ref[:, cols] = (jnp.tile(alpha_ref[:, cols], (LANES // 8, 1)) * acc_ref[:, cols]
                                + _dot(vt_ref[h // 2, c], p_ref[:, cols]))

    def attend_pair(k, _):
        attend(2 * k, 0, masked=False)
        attend(2 * k + 1, 1, masked=False)
        return 0

    logits(0, 0)
    lax.fori_loop(0, i // 2, attend_pair, 0)

    @pl.when(i % 2 == 0)
    def _():
        attend(i, 0, masked=True)

    @pl.when(i % 2 == 1)
    def _():
        attend(i - 1, 0, masked=False)
        attend(i, 1, masked=True)

    for hp in range(pairs):
        outs = []
        for h in (2 * hp, 2 * hp + 1):
            cols = slice(h * t, (h + 1) * t)
            out_t = acc_ref[:, cols] / jnp.tile(l_ref[:, cols], (LANES // 8, 1))
            outs.append(out_t.T)
        o_ref[0, :, hp * LANES:(hp + 1) * LANES] = jnp.where(
            lo_half, outs[0], outs[1]).astype(o_ref.dtype)


def _fox_t(b_grp, sc, bf_row):
    b, s, _ = b_grp.shape
    t = min(FOX_T, s)
    pairs = B_HEADS // 2
    return pl.pallas_call(
        functools.partial(_fox_t_kernel, t=t),
        grid=(b, s // t),
        in_specs=[
            pl.BlockSpec((1, t, BRANCH), lambda bi, i: (bi, i, 0)),
            pl.BlockSpec((1, s, BRANCH), lambda bi, i: (bi, 0, 1)),
            pl.BlockSpec((1, s, BRANCH), lambda bi, i: (bi, 0, 2)),
            pl.BlockSpec((1, s, LANES), lambda bi, i: (bi, 0, 0)),
            _resident((1, LANES)),
        ],
        out_specs=pl.BlockSpec((1, t, BRANCH), lambda bi, i: (bi, i, 0)),
        out_shape=jax.ShapeDtypeStruct((b, s, BRANCH), BF16),
        scratch_shapes=[
            pltpu.VMEM((B_HEADS, s, LANES), BF16),
            pltpu.VMEM((pairs, s // t, LANES, t), BF16),
            pltpu.VMEM((B_HEADS * t, LANES), BF16),
            pltpu.VMEM((8, B_HEADS * t), F32),
            pltpu.VMEM((8, B_HEADS * t), F32),
            pltpu.VMEM((8, B_HEADS * t), F32),
            pltpu.VMEM((LANES, B_HEADS * t), F32),
            pltpu.VMEM((2, t, B_HEADS * t), F32),
            pltpu.VMEM((t, B_HEADS * t), BF16),
        ],
        compiler_params=_cparams(("parallel", "arbitrary")),
        name="fox_attention",
    )(b_grp, b_grp, b_grp, sc, bf_row)


GDN_TS = 512
GDN_C = 128
TAIL = 8
NEUMANN_ROUNDS = 6


def _gdn_kernel(x_ref, z_ref, sc_ref, cw_ref, alog_ref, dtb_ref, dn_ref, o_ref,
                state_ref, tail_ref, ext_ref, act_ref, np_ref, rhs_ref,
                qk_ref, qg_ref, kdt_ref, dec_ref, *, ts):
    j = pl.program_id(1)
    width = x_ref.shape[2]
    n_chunks = ts // GDN_C
    chains = [(h, n) for n in range(n_chunks) for h in range(C_HEADS)]

    @pl.when(j == 0)
    def _():
        state_ref[...] = jnp.zeros_like(state_ref)
        tail_ref[...] = jnp.zeros_like(tail_ref)

    x = x_ref[0]
    ext_ref[0:TAIL, :] = tail_ref[...]
    ext_ref[TAIL:TAIL + ts, :] = x
    tail_ref[...] = x[ts - TAIL:ts, :]
    conv = jnp.zeros((ts, width), F32)
    for tap in range(CONV_WIDTH):
        start = TAIL - (CONV_WIDTH - 1) + tap
        conv = conv + cw_ref[tap:tap + 1, :] * ext_ref[start:start + ts, :]
    act_ref[...] = _silu(conv)

    sc = sc_ref[0]
    beta_all = jax.nn.sigmoid(sc)
    xg = sc + dtb_ref[...]
    softplus = jnp.maximum(xg, 0.0) + jnp.log1p(jnp.exp(-jnp.abs(xg)))
    g_all = -jnp.exp(alog_ref[...]) * softplus
    r = lax.broadcasted_iota(jnp.int32, (ts, ts), 0)
    c = lax.broadcasted_iota(jnp.int32, (ts, ts), 1)
    same_chunk_lower = jnp.where((c <= r) & (r // GDN_C == c // GDN_C), 1.0, 0.0)
    gc_all = _dot_hi(same_chunk_lower, g_all)
    gc_t = gc_all.T

    ri = lax.broadcasted_iota(jnp.int32, (GDN_C, GDN_C), 0)
    ci = lax.broadcasted_iota(jnp.int32, (GDN_C, GDN_C), 1)
    incl = ri >= ci
    strict = ri > ci

    def l2n(t):
        return t * lax.rsqrt(jnp.sum(t * t, axis=-1, keepdims=True) + L2_EPS)

    for idx, (h, n) in enumerate(chains):
        rows = slice(n * GDN_C, (n + 1) * GDN_C)
        q = l2n(act_ref[rows, h * C_DIM:(h + 1) * C_DIM]) * (C_DIM ** -0.5)
        k = l2n(act_ref[rows, BRANCH + h * C_DIM:BRANCH + (h + 1) * C_DIM])
        v = act_ref[rows, 2 * BRANCH + h * C_DIM:2 * BRANCH + (h + 1) * C_DIM]
        beta = beta_all[rows, SC_BETA + h:SC_BETA + h + 1]
        gcol = gc_all[rows, SC_DECAY + h:SC_DECAY + h + 1]
        grow = gc_t[SC_DECAY + h:SC_DECAY + h + 1, n * GDN_C:(n + 1) * GDN_C]
        g_last = gcol[GDN_C - 1:GDN_C, :]
        decay = jnp.exp(jnp.where(incl, gcol - grow, -jnp.inf))
        kb = k * beta
        k16 = k.astype(BF16)
        q16 = q.astype(BF16)
        a_mat = jnp.where(strict, _dot_nt(kb.astype(BF16), k16) * decay, 0.0)
        np_ref[idx, :, 0:GDN_C] = -a_mat
        np_ref[idx, :, GDN_C:2 * GDN_C] = _dot_solve(a_mat, a_mat)
        rhs_ref[idx, :, 0:C_DIM] = v * beta
        rhs_ref[idx, :, C_DIM:2 * C_DIM] = kb * jnp.exp(gcol)
        qk_ref[idx] = jnp.where(incl, _dot_nt(q16, k16) * decay, 0.0).astype(BF16)
        qg_ref[idx] = (q * jnp.exp(gcol)).astype(BF16)
        kdt_ref[idx] = (k * jnp.exp(g_last - gcol)).T.astype(BF16)
        dec_ref[idx] = jnp.broadcast_to(jnp.exp(g_last), (1, C_DIM))

    for rnd in range(1, NEUMANN_ROUNDS + 1):
        last = rnd == NEUMANN_ROUNDS
        for idx in range(len(chains)):
            n_old = np_ref[idx, :, 0:GDN_C]
            p_old = np_ref[idx, :, GDN_C:2 * GDN_C]
            if last:
                np_ref[idx, :, 0:GDN_C] = n_old + p_old + _dot_solve(p_old, n_old)
            else:
                prod = _dot_solve(p_old, np_ref[idx])
                np_ref[idx, :, 0:GDN_C] = n_old + p_old + prod[:, 0:GDN_C]
                np_ref[idx, :, GDN_C:2 * GDN_C] = prod[:, GDN_C:2 * GDN_C]

    for idx in range(len(chains)):
        rhs = rhs_ref[idx]
        rhs_ref[idx] = rhs + _dot_solve(np_ref[idx, :, 0:GDN_C], rhs)

    for idx, (h, n) in enumerate(chains):
        rows = slice(n * GDN_C, (n + 1) * GDN_C)
        state = state_ref[h]
        s16 = state.astype(BF16)
        v_new = rhs_ref[idx, :, 0:C_DIM] - _dot(rhs_ref[idx, :, C_DIM:2 * C_DIM].astype(BF16), s16)
        vn16 = v_new.astype(BF16)
        o = _dot(qg_ref[idx], s16) + _dot(qk_ref[idx], vn16)
        state_ref[h] = state * dec_ref[idx] + _dot(kdt_ref[idx], vn16)
        z = z_ref[0, rows, h * C_DIM:(h + 1) * C_DIM]
        y = _rms(o, dn_ref[...]) * _silu(z)
        o_ref[0, rows, h * C_DIM:(h + 1) * C_DIM] = y.astype(o_ref.dtype)


def _gdn(cqkv, cz, sc, conv_w, alog_row, dtb_row, dn_row):
    b, s, width = cqkv.shape
    ts = min(GDN_TS, s)
    nch = C_HEADS * (ts // GDN_C)
    blk = lambda bi, j: (bi, j, 0)
    return pl.pallas_call(
        functools.partial(_gdn_kernel, ts=ts),
        grid=(b, s // ts),
        in_specs=[
            pl.BlockSpec((1, ts, width), blk),
            pl.BlockSpec((1, ts, BRANCH), blk),
            pl.BlockSpec((1, ts, LANES), blk),
            _resident((CONV_WIDTH, width)),
            _resident((1, LANES)),
            _resident((1, LANES)),
            _resident((1, C_DIM)),
        ],
        out_specs=pl.BlockSpec((1, ts, BRANCH), blk),
        out_shape=jax.ShapeDtypeStruct((b, s, BRANCH), BF16),
        scratch_shapes=[
            pltpu.VMEM((C_HEADS, C_DIM, C_DIM), F32),
            pltpu.VMEM((TAIL, width), F32),
            pltpu.VMEM((TAIL + ts, width), F32),
            pltpu.VMEM((ts, width), F32),
            pltpu.VMEM((nch, GDN_C, 2 * GDN_C), F32),
            pltpu.VMEM((nch, GDN_C, 2 * C_DIM), F32),
            pltpu.VMEM((nch, GDN_C, GDN_C), BF16),
            pltpu.VMEM((nch, GDN_C, C_DIM), BF16),
            pltpu.VMEM((nch, C_DIM, GDN_C), BF16),
            pltpu.VMEM((nch, 1, C_DIM), F32),
        ],
        compiler_params=_cparams(("parallel", "arbitrary")),
        name="gated_deltanet",
    )(cqkv, cz, sc, conv_w, alog_row, dtb_row, dn_row)


MERGE_TM = 512


def _merge_kernel(x_ref, ya_ref, yb_ref, yc_ref, gl_ref, bg_ref,
                  wa_ref, wb_ref, wc_ref, wo_ref, o_ref):
    merged = None
    for n, (y_ref, w_ref) in enumerate(((ya_ref, wa_ref), (yb_ref, wb_ref), (yc_ref, wc_ref))):
        cols = slice(n * D_MODEL, (n + 1) * D_MODEL)
        gate = jax.nn.sigmoid(gl_ref[:, cols] + bg_ref[:, cols])
        term = gate * _dot(y_ref[...], w_ref[...])
        merged = term if merged is None else merged + term
    o_ref[...] = x_ref[...] + _dot(merged.astype(BF16), wo_ref[...])


def _merge(x, ya, yb, yc, gl, b_gate, wa, wb, wc, wo):
    n = x.shape[0]
    tm = min(MERGE_TM, n)
    row = lambda i: (i, 0)
    return pl.pallas_call(
        _merge_kernel,
        grid=(n // tm,),
        in_specs=[
            pl.BlockSpec((tm, D_MODEL), row),
            pl.BlockSpec((tm, BRANCH), row),
            pl.BlockSpec((tm, BRANCH), row),
            pl.BlockSpec((tm, BRANCH), row),
            pl.BlockSpec((tm, 3 * D_MODEL), row),
            _resident((1, 3 * D_MODEL)),
            _resident((BRANCH, D_MODEL)),
            _resident((BRANCH, D_MODEL)),
            _resident((BRANCH, D_MODEL)),
            _resident((D_MODEL, D_MODEL)),
        ],
        out_specs=pl.BlockSpec((tm, D_MODEL), row),
        out_shape=jax.ShapeDtypeStruct((n, D_MODEL), F32),
        compiler_params=_cparams(("parallel",)),
        name="gated_merge",
    )(x, ya, yb, yc, gl, b_gate, wa, wb, wc, wo)


def _w_in_pieces():
    offs = np.concatenate([[0], np.cumsum(IN_SIZES)])
    names = ("a_q", "a_k", "a_v", "i_q", "i_k", "i_w", "b_qkv", "b_f", "c_qkv", "c_z",
             "c_beta", "c_a", "gates")
    src = {n: (int(offs[i]), int(IN_SIZES[i])) for i, n in enumerate(names)}
    order = (("a_q", 0), ("i_q", 512), ("a_k", 768), ("i_k", 832), ("a_v", 896),
             ("b_f", W_A + SC_BF), ("i_w", W_A + SC_IW), ("c_beta", W_A + SC_BETA),
             ("c_a", W_A + SC_DECAY), ("b_qkv", W_A + W_SC), ("c_qkv", W_A + W_SC + W_B),
             ("c_z", W_A + W_SC + W_B + W_CQKV), ("gates", W_A + W_SC + W_B + W_CQKV + W_CZ))
    pieces = []
    for name, dst in order:
        s0, width = src[name]
        done = 0
        while done < width:
            step = min(width - done, LANES - (dst + done) % LANES)
            pieces.append((dst + done, s0 + done, step))
            done += step
    return pieces


RELAYOUT_TM = 256


def _relayout_kernel(w_ref, tail_ref, o_ref, *, aligned_cols):
    by_tile = {}
    for dst, src, n in _w_in_pieces():
        by_tile.setdefault(dst // LANES, []).append((dst % LANES, src, n))
    row = lax.broadcasted_iota(jnp.int32, (2 * LANES, LANES), 0)
    col = lax.broadcasted_iota(jnp.int32, (2 * LANES, LANES), 1)

    def src_tile(start):
        if start + LANES <= aligned_cols:
            return w_ref[0, :, start:start + LANES]
        return tail_ref[0, :, start - aligned_cols:start - aligned_cols + LANES]

    for tile in range(W_TOTAL // LANES):
        acc = None
        for dst_lane, src, n in by_tile.get(tile, ()):
            base = (src // LANES) * LANES
            shift = src - base - dst_lane
            select = (row == col + shift) & (col >= dst_lane) & (col < dst_lane + n)
            window = jnp.concatenate([src_tile(base), src_tile(base + LANES)], axis=1)
            part = _dot(window.astype(BF16), jnp.where(select, 1.0, 0.0).astype(BF16))
            acc = part if acc is None else acc + part
        if acc is None:
            acc = jnp.zeros((w_ref.shape[1], LANES), F32)
        o_ref[0, :, tile * LANES:(tile + 1) * LANES] = acc.astype(BF16)


def _layout_w_in(w_in):
    depth, rows, cols = w_in.shape
    aligned_cols = (cols // LANES) * LANES
    tail = jnp.pad(w_in[:, :, aligned_cols:], ((0, 0), (0, 0), (0, 2 * LANES - (cols - aligned_cols))))
    tm = min(RELAYOUT_TM, rows)
    return pl.pallas_call(
        functools.partial(_relayout_kernel, aligned_cols=aligned_cols),
        grid=(depth, rows // tm),
        in_specs=[pl.BlockSpec((1, tm, cols), lambda l, i: (l, i, 0)),
                  pl.BlockSpec((1, tm, 2 * LANES), lambda l, i: (l, i, 0))],
        out_specs=pl.BlockSpec((1, tm, W_TOTAL), lambda l, i: (l, i, 0)),
        out_shape=jax.ShapeDtypeStruct((depth, rows, W_TOTAL), BF16),
        compiler_params=_cparams(("parallel", "parallel")),
        name="w_in_relayout",
    )(w_in, tail)


def _rotary_tables(seq):
    pos = jnp.arange(seq, dtype=F32)
    inv_freq = jnp.power(ROPE_THETA, -jnp.arange(0, ROT_DIM, 2, dtype=F32) / ROT_DIM)
    ang = pos[:, None] * inv_freq[None, :]
    cos, sin = jnp.cos(ang), jnp.sin(ang)
    half = ROT_DIM // 2
    ones = jnp.ones((seq, HEAD_DIM - ROT_DIM), F32)
    zeros_h = jnp.zeros((seq, half), F32)
    zeros_r = jnp.zeros((seq, HEAD_DIM - ROT_DIM), F32)
    c64 = jnp.concatenate([cos, cos, ones], axis=1)
    s1_64 = jnp.concatenate([zeros_h, sin, zeros_r], axis=1)
    s2_64 = jnp.concatenate([-sin, zeros_h, zeros_r], axis=1)
    twice = lambda t: jnp.concatenate([t, t], axis=1)
    return twice(c64), twice(s1_64), twice(s2_64)


def _lane_row(values, start):
    return jnp.zeros((1, LANES), F32).at[0, start:start + values.shape[0]].set(values.astype(F32))


def kernel(x, ffn1_norm, ffn1_w_in, ffn1_w_out, mix_norm, w_in, b_gate, b_forget, conv_w, a_log, dt_bias, delta_norm, w_branch_a, w_branch_b, w_branch_c, w_out, ffn2_norm, ffn2_w_in, ffn2_w_out, final_norm):
    batch, seq, _ = x.shape
    depth = w_in.shape[0]
    top_k = min(INDEX_TOPK, seq // 4)
    cos, s1, s2 = _rotary_tables(seq)
    w_mix = _layout_w_in(w_in)
    final_row = final_norm.reshape(1, D_MODEL)
    xt = x.reshape(batch * seq, D_MODEL)
    for l in range(depth):
        xt = _ffn(xt, ffn1_norm[l].reshape(1, D_MODEL), ffn1_w_in[l].astype(BF16),
                  ffn1_w_out[l].astype(BF16), final_row, False)
        a_grp, sc, b_grp, cqkv, cz, gl = _proj(
            xt, mix_norm[l].reshape(1, D_MODEL), w_mix[l], cos, s1, s2, seq)
        a_grp = a_grp.reshape(batch, seq, W_A)
        sc = sc.reshape(batch, seq, W_SC)
        y_a = _dsa_t(a_grp, sc, top_k)
        y_b = _fox_t(b_grp.reshape(batch, seq, W_B), sc, _lane_row(b_forget[l], SC_BF))
        y_c = _gdn(cqkv.reshape(batch, seq, W_CQKV), cz.reshape(batch, seq, W_CZ), sc,
                   conv_w[l], _lane_row(a_log[l], SC_DECAY), _lane_row(dt_bias[l], SC_DECAY),
                   delta_norm[l].reshape(1, C_DIM))
        flat = lambda t: t.reshape(batch * seq, BRANCH)
        xt = _merge(xt, flat(y_a), flat(y_b), flat(y_c), gl, b_gate[l].reshape(1, 3 * D_MODEL),
                    w_branch_a[l].astype(BF16), w_branch_b[l].astype(BF16),
                    w_branch_c[l].astype(BF16), w_out[l].astype(BF16))
        xt = _ffn(xt, ffn2_norm[l].reshape(1, D_MODEL), ffn2_w_in[l].astype(BF16),
                  ffn2_w_out[l].astype(BF16), final_row, l == depth - 1)
    return xt.reshape(batch, seq, D_MODEL)
```

```python
import functools

import numpy as np
import jax
import jax.numpy as jnp
from jax import lax
from jax.experimental import pallas as pl
from jax.experimental.pallas import tpu as pltpu

F32 = jnp.float32
BF16 = jnp.bfloat16

D_MODEL = 1024
BRANCH = 512
A_HEADS = 8
HEAD_DIM = 64
IDX_HEADS = 4
INDEX_TOPK = 256
B_HEADS = 8
C_HEADS = 4
C_DIM = 128
CONV_WIDTH = 4
ROPE_THETA = 500000.0
ROT_DIM = 16
FFN_DIM = 2048
NORM_EPS = 1e-6
L2_EPS = 1e-6
IN_SIZES = (512, 64, 64, 256, 64, 4, 1536, 8, 1536, 512, 4, 4, 3072)

LANES = 128
SC_BF = 0
SC_IW = 8
SC_BETA = 12
SC_DECAY = 16
W_A, W_SC, W_B, W_CQKV, W_CZ, W_G = 1024, 128, 1536, 1536, 512, 3072
W_TOTAL = W_A + W_SC + W_B + W_CQKV + W_CZ + W_G
ROT_TILES = 7

NEG_BIG = -1e30
NEG_MASK = -2e30
VMEM_LIMIT = 56 * 1024 * 1024


def _cparams(sem):
    return pltpu.CompilerParams(dimension_semantics=sem, vmem_limit_bytes=VMEM_LIMIT)


def _dot(a, b):
    return jnp.dot(a, b, preferred_element_type=F32)


def _dot_hi(a, b):
    return jnp.dot(a, b, preferred_element_type=F32, precision=lax.Precision.HIGHEST)


def _dot_solve(a, b):
    a_hi = a.astype(BF16)
    b_hi = b.astype(BF16)
    a_lo = (a - a_hi.astype(F32)).astype(BF16)
    b_lo = (b - b_hi.astype(F32)).astype(BF16)
    m = a.shape[0]
    top = _dot(jnp.concatenate([a_hi, a_lo], axis=0), b_hi)
    return top[0:m] + top[m:2 * m] + _dot(a_hi, b_lo)


def _dot_nt(a, b):
    return lax.dot_general(a, b, (((1,), (1,)), ((), ())), preferred_element_type=F32)


def _rms(x, gain):
    return x * lax.rsqrt(jnp.mean(x * x, axis=-1, keepdims=True) + NORM_EPS) * gain


def _silu(x):
    return x * jax.nn.sigmoid(x)


def _resident(shape):
    nd = len(shape)
    return pl.BlockSpec(shape, lambda *_: (0,) * nd, pipeline_mode=pl.Buffered(1))


FFN_TM = 512
FFN_CHUNK = 512


def _ffn_kernel(x_ref, g_ref, win_ref, wout_ref, fg_ref, o_ref, *, final):
    x = x_ref[...]
    h = _rms(x, g_ref[...]).astype(BF16)
    acc = jnp.zeros(x.shape, F32)
    for c in range(FFN_DIM // FFN_CHUNK):
        lo = c * FFN_CHUNK
        gate = _dot(h, win_ref[:, lo:lo + FFN_CHUNK])
        up = _dot(h, win_ref[:, FFN_DIM + lo:FFN_DIM + lo + FFN_CHUNK])
        act = (_silu(gate) * up).astype(BF16)
        acc = acc + _dot(act, wout_ref[lo:lo + FFN_CHUNK, :])
    y = x + 0.5 * acc
    if final:
        y = _rms(y, fg_ref[...])
    o_ref[...] = y


def _ffn(x, gain, w_in, w_out, final_gain, final):
    n = x.shape[0]
    tm = min(FFN_TM, n)
    return pl.pallas_call(
        functools.partial(_ffn_kernel, final=final),
        grid=(n // tm,),
        in_specs=[
            pl.BlockSpec((tm, D_MODEL), lambda i: (i, 0)),
            _resident((1, D_MODEL)),
            _resident((D_MODEL, 2 * FFN_DIM)),
            _resident((FFN_DIM, D_MODEL)),
            _resident((1, D_MODEL)),
        ],
        out_specs=pl.BlockSpec((tm, D_MODEL), lambda i: (i, 0)),
        out_shape=jax.ShapeDtypeStruct((n, D_MODEL), F32),
        compiler_params=_cparams(("parallel",)),
        name="ffn_half",
    )(x, gain, w_in, w_out, final_gain)


PROJ_TM = 256
PROJ_CHUNK = 512


def _proj_kernel(x_ref, g_ref, w_ref, cos_ref, s1_ref, s2_ref,
                 a_ref, sc_ref, b_ref, cqkv_ref, cz_ref, gl_ref):
    h = _rms(x_ref[...], g_ref[...]).astype(BF16)
    cos, s1, s2 = cos_ref[...], s1_ref[...], s2_ref[...]

    off = 0
    for j in range(W_A // LANES):
        t = _dot(h, w_ref[:, off:off + LANES])
        if j < ROT_TILES:
            t = t * cos + pltpu.roll(t, 8, 1) * s1 + pltpu.roll(t, LANES - 8, 1) * s2
        a_ref[:, off:off + LANES] = t.astype(BF16)
        off += LANES
    sc_ref[...] = _dot(h, w_ref[:, off:off + W_SC])
    off += W_SC
    for ref, width in ((b_ref, W_B), (cqkv_ref, W_CQKV), (cz_ref, W_CZ), (gl_ref, W_G)):
        for lo in range(0, width, PROJ_CHUNK):
            ref[:, lo:lo + PROJ_CHUNK] = _dot(
                h, w_ref[:, off + lo:off + lo + PROJ_CHUNK]).astype(ref.dtype)
        off += width


def _proj(x, gain, w, cos, s1, s2, seq):
    n = x.shape[0]
    tm = min(PROJ_TM, seq)
    per_seq = seq // tm
    row = lambda i: (i, 0)
    tab = lambda i: (i % per_seq, 0)
    widths = (W_A, W_SC, W_B, W_CQKV, W_CZ, W_G)
    dtypes = (BF16, F32, BF16, F32, F32, F32)
    return pl.pallas_call(
        _proj_kernel,
        grid=(n // tm,),
        in_specs=[
            pl.BlockSpec((tm, D_MODEL), row),
            _resident((1, D_MODEL)),
            _resident((D_MODEL, W_TOTAL)),
            pl.BlockSpec((tm, LANES), tab),
            pl.BlockSpec((tm, LANES), tab),
            pl.BlockSpec((tm, LANES), tab),
        ],
        out_specs=[pl.BlockSpec((tm, wd), row) for wd in widths],
        out_shape=[jax.ShapeDtypeStruct((n, wd), dt) for wd, dt in zip(widths, dtypes)],
        compiler_params=_cparams(("parallel",)),
        name="mixer_proj",
    )(x, gain, w, cos, s1, s2)


DSA_TQ = 256
DSA_CK = 512
INT_MIN = -2 ** 31
KEY_NEG_INF = int(np.array(-np.inf, np.float32).view(np.int32)) ^ 0x7FFFFFFF
SLAB = 64
SUM_ROW = HEAD_DIM
EXP_ROWS = 128


def _dsa_t_kernel(qa_ref, qi_ref, kk_ref, vv_ref, sc_ref, o_ref,
                  vt_ref, key_ref, bias_ref, qs_ref, m_ref, alpha_ref, acc_ref,
                  s_ref, p_ref, *, top_k, ck):
    i = pl.program_id(1)
    tq = qa_ref.shape[1]
    n_chunks_total = key_ref.shape[0]
    nck = (i * tq + tq + ck - 1) // ck
    lane = lax.broadcasted_iota(jnp.int32, (tq, LANES), 1)
    lo_half = lane < HEAD_DIM
    scale = HEAD_DIM ** -0.5

    @pl.when(i == 0)
    def _():
        ones_row = lax.broadcasted_iota(jnp.int32, (LANES, ck), 0) == SUM_ROW
        for c in range(n_chunks_total):
            vt = vv_ref[0, c * ck:(c + 1) * ck, :].astype(F32).T
            vt_ref[c] = jnp.where(ones_row, 1.0, vt).astype(BF16)

    def put(h, t):
        qs_ref[:, h * tq:(h + 1) * tq] = (t * scale).T.astype(BF16)

    for j in range(A_HEADS // 2):
        t = qa_ref[0, :, j * LANES:(j + 1) * LANES].astype(F32)
        put(2 * j, jnp.where(lo_half, t, 0.0))
        put(2 * j + 1, jnp.where(lo_half, pltpu.roll(t, HEAD_DIM, 1), 0.0))
    for j in range(IDX_HEADS // 2):
        t = qi_ref[0, :, j * LANES:(j + 1) * LANES].astype(F32)
        put(A_HEADS + 2 * j, jnp.where(lo_half, 0.0, pltpu.roll(t, HEAD_DIM, 1)))
        put(A_HEADS + 2 * j + 1, jnp.where(lo_half, 0.0, t))

    w_rows = (sc_ref[0] * (IDX_HEADS ** -0.5)).T
    key_pos = lax.broadcasted_iota(jnp.int32, (ck, tq), 0)
    qry_pos = i * tq + lax.broadcasted_iota(jnp.int32, (ck, tq), 1)

    def key_rows(c):
        return pl.ds(pl.multiple_of(c * ck, ck), ck)

    def score_chunk(c, _):
        kc = kk_ref[0, key_rows(c), :]
        score = None
        for h in range(IDX_HEADS):
            rel = jnp.maximum(
                _dot(kc, qs_ref[:, (A_HEADS + h) * tq:(A_HEADS + h + 1) * tq]), 0.0)
            term = rel * w_rows[SC_IW + h:SC_IW + h + 1, :]
            score = term if score is None else score + term
        score = jnp.where(score == 0.0, 0.0, score)
        score = jnp.where(key_pos + c * ck <= qry_pos, score, -jnp.inf)
        bits = lax.bitcast_convert_type(score, jnp.int32)
        key_ref[c] = jnp.where(bits < 0, bits ^ 0x7FFFFFFF, bits)
        return 0

    lax.fori_loop(0, nck, score_chunk, 0)

    def count(pred):
        def body(c, acc):
            for r0 in range(0, ck, SLAB):
                acc = acc + jnp.where(pred(key_ref[c, r0:r0 + SLAB, :]), 1.0, 0.0)
            return acc
        acc = lax.fori_loop(0, nck, body, jnp.zeros((SLAB, tq), F32))
        return jnp.sum(acc, axis=0, keepdims=True)

    kf = float(top_k)

    def bit_step(b, thr):
        cand = thr + lax.shift_left(jnp.int32(1), 31 - b)
        cand_slab = jnp.broadcast_to(cand, (SLAB, tq))
        cnt = count(lambda kc: kc >= cand_slab)
        return jnp.where(cnt >= kf, cand, thr)

    thr = lax.fori_loop(0, 32, bit_step, jnp.full((1, tq), INT_MIN, jnp.int32))
    thr = jnp.maximum(thr, KEY_NEG_INF + 1)
    thr_slab = jnp.broadcast_to(thr, (SLAB, tq))
    cnt_ge = count(lambda kc: kc >= thr_slab)

    def write_bias(c, _):
        bias_ref[c] = jnp.where(key_ref[c] >= thr, 0.0, NEG_MASK)
        return 0

    lax.fori_loop(0, nck, write_bias, 0)

    @pl.when(jnp.max(cnt_ge) > kf)
    def _():
        need = kf - count(lambda kc: kc > thr_slab)
        r = lax.broadcasted_iota(jnp.int32, (ck, ck), 0)
        cidx = lax.broadcasted_iota(jnp.int32, (ck, ck), 1)
        lower = jnp.where(cidx <= r, 1.0, 0.0).astype(BF16)

        def tie_chunk(c, before):
            kc = key_ref[c]
            tie = kc == thr
            rank = before + _dot(lower, jnp.where(tie, 1.0, 0.0).astype(BF16))
            keep = (kc > thr) | (tie & (rank <= need))
            bias_ref[c] = jnp.where(keep, 0.0, NEG_MASK)
            return rank[ck - 1:ck, :]

        lax.fori_loop(0, nck, tie_chunk, jnp.zeros((1, tq), F32))

    m_ref[...] = jnp.full(m_ref.shape, NEG_BIG, F32)
    acc_ref[...] = jnp.zeros(acc_ref.shape, F32)
    reps = ck // 8

    n_att = A_HEADS * tq

    def attend(c, _):
        s_ref[...] = _dot(kk_ref[0, key_rows(c), :], qs_ref[:, 0:n_att])
        for tile in range(n_att // LANES):
            cols = slice(tile * LANES, (tile + 1) * LANES)
            qcols = slice((tile * LANES) % tq, (tile * LANES) % tq + LANES)
            part = None
            for r0 in range(0, ck, EXP_ROWS):
                rows = slice(r0, r0 + EXP_ROWS)
                s = s_ref[rows, cols] + bias_ref[c, rows, qcols]
                s_ref[rows, cols] = s
                unit = jnp.max(s.reshape(EXP_ROWS // 8, 8, LANES), axis=0)
                part = unit if part is None else jnp.maximum(part, unit)
            m_old = m_ref[:, cols]
            m_new = jnp.maximum(m_old, jnp.max(part, axis=0, keepdims=True))
            alpha_ref[:, cols] = jnp.exp(m_old - m_new)
            m_ref[:, cols] = m_new
        for tile in range(n_att // LANES):
            cols = slice(tile * LANES, (tile + 1) * LANES)
            for r0 in range(0, ck, EXP_ROWS):
                rows = slice(r0, r0 + EXP_ROWS)
                p = jnp.exp(s_ref[rows, cols] - jnp.tile(m_ref[:, cols], (EXP_ROWS // 8, 1)))
                p_ref[rows, cols] = p.astype(BF16)
        acc_ref[...] = (jnp.tile(alpha_ref[...], (LANES // 8, 1)) * acc_ref[...]
                        + _dot(vt_ref[c], p_ref[...]))
        return 0

    lax.fori_loop(0, nck, attend, 0)

    value_rows = lax.broadcasted_iota(jnp.int32, (LANES, tq), 0) < HEAD_DIM
    for j in range(A_HEADS // 2):
        outs = []
        for h in (2 * j, 2 * j + 1):
            cols = slice(h * tq, (h + 1) * tq)
            out_t = jnp.where(value_rows,
                              acc_ref[:, cols] / acc_ref[SUM_ROW:SUM_ROW + 1, cols], 0.0)
            outs.append(out_t.T)
        o_ref[0, :, j * LANES:(j + 1) * LANES] = (
            outs[0] + pltpu.roll(outs[1], HEAD_DIM, 1)).astype(o_ref.dtype)


def _dsa_t(a_grp, sc, top_k):
    b, s, _ = a_grp.shape
    tq = min(DSA_TQ, s)
    ck = min(DSA_CK, s)
    return pl.pallas_call(
        functools.partial(_dsa_t_kernel, top_k=top_k, ck=ck),
        grid=(b, s // tq),
        in_specs=[
            pl.BlockSpec((1, tq, 512), lambda bi, i: (bi, i, 0)),
            pl.BlockSpec((1, tq, 256), lambda bi, i: (bi, i, 2)),
            pl.BlockSpec((1, s, LANES), lambda bi, i: (bi, 0, 6)),
            pl.BlockSpec((1, s, LANES), lambda bi, i: (bi, 0, 7)),
            pl.BlockSpec((1, tq, LANES), lambda bi, i: (bi, i, 0)),
        ],
        out_specs=pl.BlockSpec((1, tq, BRANCH), lambda bi, i: (bi, i, 0)),
        out_shape=jax.ShapeDtypeStruct((b, s, BRANCH), BF16),
        scratch_shapes=[
            pltpu.VMEM((s // ck, LANES, ck), BF16),
            pltpu.VMEM((s // ck, ck, tq), jnp.int32),
            pltpu.VMEM((s // ck, ck, tq), F32),
            pltpu.VMEM((LANES, (A_HEADS + IDX_HEADS) * tq), BF16),
            pltpu.VMEM((8, A_HEADS * tq), F32),
            pltpu.VMEM((8, A_HEADS * tq), F32),
            pltpu.VMEM((LANES, A_HEADS * tq), F32),
            pltpu.VMEM((ck, A_HEADS * tq), F32),
            pltpu.VMEM((ck, A_HEADS * tq), BF16),
        ],
        compiler_params=_cparams(("parallel", "arbitrary")),
        name="dsa_attention",
    )(a_grp, a_grp, a_grp, a_grp, sc)


FOX_T = 256
BIAS_LANE = HEAD_DIM
BIAS_TERMS = 3


def _fox_t_kernel(q_ref, k_ref, v_ref, sc_ref, bf_ref, o_ref,
                  kaug_ref, vt_ref, qz_ref, m_ref, l_ref, alpha_ref, acc_ref, s_ref, p_ref, *, t):
    i = pl.program_id(1)
    seq = k_ref.shape[1]
    n_chunks = seq // t
    pairs = B_HEADS // 2
    lane = lax.broadcasted_iota(jnp.int32, (t, LANES), 1)
    lo_half = lane < HEAD_DIM

    def bias_lanes(h):
        first = BIAS_LANE + BIAS_TERMS * h
        return (lane >= first) & (lane < first + BIAS_TERMS)

    @pl.when(i == 0)
    def _():
        r = lax.broadcasted_iota(jnp.int32, (t, t), 0)
        c = lax.broadcasted_iota(jnp.int32, (t, t), 1)
        lower = jnp.where(c <= r, 1.0, 0.0)
        src = lax.broadcasted_iota(jnp.int32, (LANES, LANES), 0)
        dst = lax.broadcasted_iota(jnp.int32, (LANES, LANES), 1)

        def build(n, carry):
            rows = pl.ds(pl.multiple_of(n * t, t), t)
            x = sc_ref[0, rows, :] + bf_ref[...]
            log_f = jnp.minimum(x, 0.0) - jnp.log1p(jnp.exp(-jnp.abs(x)))
            cum = _dot_hi(lower, log_f) + carry
            nc = -cum
            hi = nc.astype(BF16)
            rest = nc - hi.astype(F32)
            mid = rest.astype(BF16)
            lo = (rest - mid.astype(F32)).astype(BF16)
            placed = None
            for j, term in enumerate((hi, mid, lo)):
                place = jnp.where((src >= SC_BF) & (src < SC_BF + B_HEADS)
                                  & (dst == BIAS_LANE + BIAS_TERMS * (src - SC_BF) + j), 1.0, 0.0)
                part = _dot(term, place.astype(BF16))
                placed = part if placed is None else placed + part
            for h in range(B_HEADS):
                hp = h // 2
                kt = k_ref[0, rows, hp * LANES:(hp + 1) * LANES].astype(F32)
                if h % 2:
                    kt = pltpu.roll(kt, HEAD_DIM, 1)
                tile = jnp.where(lo_half, kt, jnp.where(bias_lanes(h), placed, 0.0))
                kaug_ref[h, rows, :] = tile.astype(BF16)
            for hp in range(pairs):
                vt_ref[hp, n] = v_ref[0, rows, hp * LANES:(hp + 1) * LANES].astype(F32).T.astype(BF16)
            return cum[t - 1:t, :]

        lax.fori_loop(0, n_chunks, build, jnp.zeros((1, LANES), F32))

    for hp in range(pairs):
        q = q_ref[0, :, hp * LANES:(hp + 1) * LANES].astype(F32) * (HEAD_DIM ** -0.5)
        qz_ref[:, (2 * hp) * t:(2 * hp + 1) * t] = jnp.where(
            lo_half, q, jnp.where(bias_lanes(2 * hp), 1.0, 0.0)).T.astype(BF16)
        qz_ref[:, (2 * hp + 1) * t:(2 * hp + 2) * t] = jnp.where(
            lo_half, pltpu.roll(q, HEAD_DIM, 1),
            jnp.where(bias_lanes(2 * hp + 1), 1.0, 0.0)).T.astype(BF16)
    causal = (lax.broadcasted_iota(jnp.int32, (t, t), 0)
              <= lax.broadcasted_iota(jnp.int32, (t, t), 1))
    m_ref[...] = jnp.full(m_ref.shape, NEG_BIG, F32)
    l_ref[...] = jnp.zeros(l_ref.shape, F32)
    acc_ref[...] = jnp.zeros(acc_ref.shape, F32)

    def logits(c, slot):
        rows = pl.ds(pl.multiple_of(c * t, t), t)
        for h in range(B_HEADS):
            cols = slice(h * t, (h + 1) * t)
            s_ref[slot, :, cols] = _dot(kaug_ref[h, rows, :], qz_ref[:, cols])

    def attend(c, slot, masked):
        if not masked:
            logits(c + 1, 1 - slot)
        for h in range(B_HEADS):
            cols = slice(h * t, (h + 1) * t)
            s = s_ref[slot, :, cols]
            if masked:
                s = jnp.where(causal, s, -jnp.inf)
            m_old = m_ref[:, cols]
            m_new = jnp.maximum(m_old, jnp.max(s, axis=0, keepdims=True))
            alpha = jnp.exp(m_old - m_new)
            p = jnp.exp(s - jnp.tile(m_new, (t // 8, 1)))
            l_ref[:, cols] = alpha * l_ref[:, cols] + jnp.sum(p, axis=0, keepdims=True)
            m_ref[:, cols] = m_new
            alpha_ref[:, cols] = alpha
            p_ref[:, cols] = p.astype(BF16)
        for h in range(B_HEADS):
            cols = slice(h * t, (h + 1) * t)
            acc_ref[:, cols] = (jnp.tile(alpha_ref[:, cols], (LANES // 8, 1)) * acc_ref[:, cols]
                                + _dot(vt_ref[h // 2, c], p_ref[:, cols]))

    def attend_pair(k, _):
        attend(2 * k, 0, masked=False)
        attend(2 * k + 1, 1, masked=False)
        return 0

    logits(0, 0)
    lax.fori_loop(0, i // 2, attend_pair, 0)

    @pl.when(i % 2 == 0)
    def _():
        attend(i, 0, masked=True)

    @pl.when(i % 2 == 1)
    def _():
        attend(i - 1, 0, masked=False)
        attend(i, 1, masked=True)

    for hp in range(pairs):
        outs = []
        for h in (2 * hp, 2 * hp + 1):
            cols = slice(h * t, (h + 1) * t)
            out_t = acc_ref[:, cols] / jnp.tile(l_ref[:, cols], (LANES // 8, 1))
            outs.append(out_t.T)
        o_ref[0, :, hp * LANES:(hp + 1) * LANES] = jnp.where(
            lo_half, outs[0], outs[1]).astype(o_ref.dtype)


def _fox_t(b_grp, sc, bf_row):
    b, s, _ = b_grp.shape
    t = min(FOX_T, s)
    pairs = B_HEADS // 2
    return pl.pallas_call(
        functools.partial(_fox_t_kernel, t=t),
        grid=(b, s // t),
        in_specs=[
            pl.BlockSpec((1, t, BRANCH), lambda bi, i: (bi, i, 0)),
            pl.BlockSpec((1, s, BRANCH), lambda bi, i: (bi, 0, 1)),
            pl.BlockSpec((1, s, BRANCH), lambda bi, i: (bi, 0, 2)),
            pl.BlockSpec((1, s, LANES), lambda bi, i: (bi, 0, 0)),
            _resident((1, LANES)),
        ],
        out_specs=pl.BlockSpec((1, t, BRANCH), lambda bi, i: (bi, i, 0)),
        out_shape=jax.ShapeDtypeStruct((b, s, BRANCH), BF16),
        scratch_shapes=[
            pltpu.VMEM((B_HEADS, s, LANES), BF16),
            pltpu.VMEM((pairs, s // t, LANES, t), BF16),
            pltpu.VMEM((LANES, B_HEADS * t), BF16),
            pltpu.VMEM((8, B_HEADS * t), F32),
            pltpu.VMEM((8, B_HEADS * t), F32),
            pltpu.VMEM((8, B_HEADS * t), F32),
            pltpu.VMEM((LANES, B_HEADS * t), F32),
            pltpu.VMEM((2, t, B_HEADS * t), F32),
            pltpu.VMEM((t, B_HEADS * t), BF16),
        ],
        compiler_params=_cparams(("parallel", "arbitrary")),
        name="fox_attention",
    )(b_grp, b_grp, b_grp, sc, bf_row)


GDN_TS = 512
GDN_C = 128
TAIL = 8
NEUMANN_ROUNDS = 6


def _gdn_kernel(x_ref, z_ref, sc_ref, cw_ref, alog_ref, dtb_ref, dn_ref, o_ref,
                state_ref, tail_ref, ext_ref, act_ref, np_ref, rhs_ref,
                qk_ref, qg_ref, kdt_ref, dec_ref, *, ts):
    j = pl.program_id(1)
    width = x_ref.shape[2]
    n_chunks = ts // GDN_C
    chains = [(h, n) for n in range(n_chunks) for h in range(C_HEADS)]

    @pl.when(j == 0)
    def _():
        state_ref[...] = jnp.zeros_like(state_ref)
        tail_ref[...] = jnp.zeros_like(tail_ref)

    x = x_ref[0]
    ext_ref[0:TAIL, :] = tail_ref[...]
    ext_ref[TAIL:TAIL + ts, :] = x
    tail_ref[...] = x[ts - TAIL:ts, :]
    conv = jnp.zeros((ts, width), F32)
    for tap in range(CONV_WIDTH):
        start = TAIL - (CONV_WIDTH - 1) + tap
        conv = conv + cw_ref[tap:tap + 1, :] * ext_ref[start:start + ts, :]
    act_ref[...] = _silu(conv)

    sc = sc_ref[0]
    beta_all = jax.nn.sigmoid(sc)
    xg = sc + dtb_ref[...]
    softplus = jnp.maximum(xg, 0.0) + jnp.log1p(jnp.exp(-jnp.abs(xg)))
    g_all = -jnp.exp(alog_ref[...]) * softplus
    r = lax.broadcasted_iota(jnp.int32, (ts, ts), 0)
    c = lax.broadcasted_iota(jnp.int32, (ts, ts), 1)
    same_chunk_lower = jnp.where((c <= r) & (r // GDN_C == c // GDN_C), 1.0, 0.0)
    gc_all = _dot_hi(same_chunk_lower, g_all)
    gc_t = gc_all.T

    ri = lax.broadcasted_iota(jnp.int32, (GDN_C, GDN_C), 0)
    ci = lax.broadcasted_iota(jnp.int32, (GDN_C, GDN_C), 1)
    incl = ri >= ci
    strict = ri > ci

    def l2n(t):
        return t * lax.rsqrt(jnp.sum(t * t, axis=-1, keepdims=True) + L2_EPS)

    for idx, (h, n) in enumerate(chains):
        rows = slice(n * GDN_C, (n + 1) * GDN_C)
        q = l2n(act_ref[rows, h * C_DIM:(h + 1) * C_DIM]) * (C_DIM ** -0.5)
        k = l2n(act_ref[rows, BRANCH + h * C_DIM:BRANCH + (h + 1) * C_DIM])
        v = act_ref[rows, 2 * BRANCH + h * C_DIM:2 * BRANCH + (h + 1) * C_DIM]
        beta = beta_all[rows, SC_BETA + h:SC_BETA + h + 1]
        gcol = gc_all[rows, SC_DECAY + h:SC_DECAY + h + 1]
        grow = gc_t[SC_DECAY + h:SC_DECAY + h + 1, n * GDN_C:(n + 1) * GDN_C]
        g_last = gcol[GDN_C - 1:GDN_C, :]
        decay = jnp.exp(jnp.where(incl, gcol - grow, -jnp.inf))
        kb = k * beta
        k16 = k.astype(BF16)
        q16 = q.astype(BF16)
        a_mat = jnp.where(strict, _dot_nt(kb.astype(BF16), k16) * decay, 0.0)
        np_ref[idx, :, 0:GDN_C] = -a_mat
        np_ref[idx, :, GDN_C:2 * GDN_C] = _dot_solve(a_mat, a_mat)
        rhs_ref[idx, :, 0:C_DIM] = v * beta
        rhs_ref[idx, :, C_DIM:2 * C_DIM] = kb * jnp.exp(gcol)
        qk_ref[idx] = jnp.where(incl, _dot_nt(q16, k16) * decay, 0.0).astype(BF16)
        qg_ref[idx] = (q * jnp.exp(gcol)).astype(BF16)
        kdt_ref[idx] = (k * jnp.exp(g_last - gcol)).T.astype(BF16)
        dec_ref[idx] = jnp.broadcast_to(jnp.exp(g_last), (1, C_DIM))

    for rnd in range(1, NEUMANN_ROUNDS + 1):
        last = rnd == NEUMANN_ROUNDS
        for idx in range(len(chains)):
            n_old = np_ref[idx, :, 0:GDN_C]
            p_old = np_ref[idx, :, GDN_C:2 * GDN_C]
            if last:
                np_ref[idx, :, 0:GDN_C] = n_old + p_old + _dot_solve(p_old, n_old)
            else:
                prod = _dot_solve(p_old, np_ref[idx])
                np_ref[idx, :, 0:GDN_C] = n_old + p_old + prod[:, 0:GDN_C]
                np_ref[idx, :, GDN_C:2 * GDN_C] = prod[:, GDN_C:2 * GDN_C]

    for idx in range(len(chains)):
        rhs = rhs_ref[idx]
        rhs_ref[idx] = rhs + _dot_solve(np_ref[idx, :, 0:GDN_C], rhs)

    for idx, (h, n) in enumerate(chains):
        rows = slice(n * GDN_C, (n + 1) * GDN_C)
        state = state_ref[h]
        s16 = state.astype(BF16)
        v_new = rhs_ref[idx, :, 0:C_DIM] - _dot(rhs_ref[idx, :, C_DIM:2 * C_DIM].astype(BF16), s16)
        vn16 = v_new.astype(BF16)
        o = _dot(qg_ref[idx], s16) + _dot(qk_ref[idx], vn16)
        state_ref[h] = state * dec_ref[idx] + _dot(kdt_ref[idx], vn16)
        z = z_ref[0, rows, h * C_DIM:(h + 1) * C_DIM]
        y = _rms(o, dn_ref[...]) * _silu(z)
        o_ref[0, rows, h * C_DIM:(h + 1) * C_DIM] = y.astype(o_ref.dtype)


def _gdn(cqkv, cz, sc, conv_w, alog_row, dtb_row, dn_row):
    b, s, width = cqkv.shape
    ts = min(GDN_TS, s)
    nch = C_HEADS * (ts // GDN_C)
    blk = lambda bi, j: (bi, j, 0)
    return pl.pallas_call(
        functools.partial(_gdn_kernel, ts=ts),
        grid=(b, s // ts),
        in_specs=[
            pl.BlockSpec((1, ts, width), blk),
            pl.BlockSpec((1, ts, BRANCH), blk),
            pl.BlockSpec((1, ts, LANES), blk),
            _resident((CONV_WIDTH, width)),
            _resident((1, LANES)),
            _resident((1, LANES)),
            _resident((1, C_DIM)),
        ],
        out_specs=pl.BlockSpec((1, ts, BRANCH), blk),
        out_shape=jax.ShapeDtypeStruct((b, s, BRANCH), BF16),
        scratch_shapes=[
            pltpu.VMEM((C_HEADS, C_DIM, C_DIM), F32),
            pltpu.VMEM((TAIL, width), F32),
            pltpu.VMEM((TAIL + ts, width), F32),
            pltpu.VMEM((ts, width), F32),
            pltpu.VMEM((nch, GDN_C, 2 * GDN_C), F32),
            pltpu.VMEM((nch, GDN_C, 2 * C_DIM), F32),
            pltpu.VMEM((nch, GDN_C, GDN_C), BF16),
            pltpu.VMEM((nch, GDN_C, C_DIM), BF16),
            pltpu.VMEM((nch, C_DIM, GDN_C), BF16),
            pltpu.VMEM((nch, 1, C_DIM), F32),
        ],
        compiler_params=_cparams(("parallel", "arbitrary")),
        name="gated_deltanet",
    )(cqkv, cz, sc, conv_w, alog_row, dtb_row, dn_row)


MERGE_TM = 512


def _merge_kernel(x_ref, ya_ref, yb_ref, yc_ref, gl_ref, bg_ref,
                  wa_ref, wb_ref, wc_ref, wo_ref, o_ref):
    merged = None
    for n, (y_ref, w_ref) in enumerate(((ya_ref, wa_ref), (yb_ref, wb_ref), (yc_ref, wc_ref))):
        cols = slice(n * D_MODEL, (n + 1) * D_MODEL)
        gate = jax.nn.sigmoid(gl_ref[:, cols] + bg_ref[:, cols])
        term = gate * _dot(y_ref[...], w_ref[...])
        merged = term if merged is None else merged + term
    o_ref[...] = x_ref[...] + _dot(merged.astype(BF16), wo_ref[...])


def _merge(x, ya, yb, yc, gl, b_gate, wa, wb, wc, wo):
    n = x.shape[0]
    tm = min(MERGE_TM, n)
    row = lambda i: (i, 0)
    return pl.pallas_call(
        _merge_kernel,
        grid=(n // tm,),
        in_specs=[
            pl.BlockSpec((tm, D_MODEL), row),
            pl.BlockSpec((tm, BRANCH), row),
            pl.BlockSpec((tm, BRANCH), row),
            pl.BlockSpec((tm, BRANCH), row),
            pl.BlockSpec((tm, 3 * D_MODEL), row),
            _resident((1, 3 * D_MODEL)),
            _resident((BRANCH, D_MODEL)),
            _resident((BRANCH, D_MODEL)),
            _resident((BRANCH, D_MODEL)),
            _resident((D_MODEL, D_MODEL)),
        ],
        out_specs=pl.BlockSpec((tm, D_MODEL), row),
        out_shape=jax.ShapeDtypeStruct((n, D_MODEL), F32),
        compiler_params=_cparams(("parallel",)),
        name="gated_merge",
    )(x, ya, yb, yc, gl, b_gate, wa, wb, wc, wo)


def _w_in_pieces():
    offs = np.concatenate([[0], np.cumsum(IN_SIZES)])
    names = ("a_q", "a_k", "a_v", "i_q", "i_k", "i_w", "b_qkv", "b_f", "c_qkv", "c_z",
             "c_beta", "c_a", "gates")
    src = {n: (int(offs[i]), int(IN_SIZES[i])) for i, n in enumerate(names)}
    order = (("a_q", 0), ("i_q", 512), ("a_k", 768), ("i_k", 832), ("a_v", 896),
             ("b_f", W_A + SC_BF), ("i_w", W_A + SC_IW), ("c_beta", W_A + SC_BETA),
             ("c_a", W_A + SC_DECAY), ("b_qkv", W_A + W_SC), ("c_qkv", W_A + W_SC + W_B),
             ("c_z", W_A + W_SC + W_B + W_CQKV), ("gates", W_A + W_SC + W_B + W_CQKV + W_CZ))
    pieces = []
    for name, dst in order:
        s0, width = src[name]
        done = 0
        while done < width:
            step = min(width - done, LANES - (dst + done) % LANES)
            pieces.append((dst + done, s0 + done, step))
            done += step
    return pieces


RELAYOUT_TM = 256


def _relayout_kernel(w_ref, tail_ref, o_ref, *, aligned_cols):
    by_tile = {}
    for dst, src, n in _w_in_pieces():
        by_tile.setdefault(dst // LANES, []).append((dst % LANES, src, n))
    row = lax.broadcasted_iota(jnp.int32, (2 * LANES, LANES), 0)
    col = lax.broadcasted_iota(jnp.int32, (2 * LANES, LANES), 1)

    def src_tile(start):
        if start + LANES <= aligned_cols:
            return w_ref[0, :, start:start + LANES]
        return tail_ref[0, :, start - aligned_cols:start - aligned_cols + LANES]

    for tile in range(W_TOTAL // LANES):
        acc = None
        for dst_lane, src, n in by_tile.get(tile, ()):
            base = (src // LANES) * LANES
            shift = src - base - dst_lane
            select = (row == col + shift) & (col >= dst_lane) & (col < dst_lane + n)
            window = jnp.concatenate([src_tile(base), src_tile(base + LANES)], axis=1)
            part = _dot(window.astype(BF16), jnp.where(select, 1.0, 0.0).astype(BF16))
            acc = part if acc is None else acc + part
        if acc is None:
            acc = jnp.zeros((w_ref.shape[1], LANES), F32)
        o_ref[0, :, tile * LANES:(tile + 1) * LANES] = acc.astype(BF16)


def _layout_w_in(w_in):
    depth, rows, cols = w_in.shape
    aligned_cols = (cols // LANES) * LANES
    tail = jnp.pad(w_in[:, :, aligned_cols:], ((0, 0), (0, 0), (0, 2 * LANES - (cols - aligned_cols))))
    tm = min(RELAYOUT_TM, rows)
    return pl.pallas_call(
        functools.partial(_relayout_kernel, aligned_cols=aligned_cols),
        grid=(depth, rows // tm),
        in_specs=[pl.BlockSpec((1, tm, cols), lambda l, i: (l, i, 0)),
                  pl.BlockSpec((1, tm, 2 * LANES), lambda l, i: (l, i, 0))],
        out_specs=pl.BlockSpec((1, tm, W_TOTAL), lambda l, i: (l, i, 0)),
        out_shape=jax.ShapeDtypeStruct((depth, rows, W_TOTAL), BF16),
        compiler_params=_cparams(("parallel", "parallel")),
        name="w_in_relayout",
    )(w_in, tail)


def _rotary_tables(seq):
    pos = jnp.arange(seq, dtype=F32)
    inv_freq = jnp.power(ROPE_THETA, -jnp.arange(0, ROT_DIM, 2, dtype=F32) / ROT_DIM)
    ang = pos[:, None] * inv_freq[None, :]
    cos, sin = jnp.cos(ang), jnp.sin(ang)
    half = ROT_DIM // 2
    ones = jnp.ones((seq, HEAD_DIM - ROT_DIM), F32)
    zeros_h = jnp.zeros((seq, half), F32)
    zeros_r = jnp.zeros((seq, HEAD_DIM - ROT_DIM), F32)
    c64 = jnp.concatenate([cos, cos, ones], axis=1)
    s1_64 = jnp.concatenate([zeros_h, sin, zeros_r], axis=1)
    s2_64 = jnp.concatenate([-sin, zeros_h, zeros_r], axis=1)
    twice = lambda t: jnp.concatenate([t, t], axis=1)
    return twice(c64), twice(s1_64), twice(s2_64)


def _lane_row(values, start):
    return jnp.zeros((1, LANES), F32).at[0, start:start + values.shape[0]].set(values.astype(F32))


def kernel(x, ffn1_norm, ffn1_w_in, ffn1_w_out, mix_norm, w_in, b_gate, b_forget, conv_w, a_log, dt_bias, delta_norm, w_branch_a, w_branch_b, w_branch_c, w_out, ffn2_norm, ffn2_w_in, ffn2_w_out, final_norm):
    batch, seq, _ = x.shape
    depth = w_in.shape[0]
    top_k = min(INDEX_TOPK, seq // 4)
    cos, s1, s2 = _rotary_tables(seq)
    w_mix = _layout_w_in(w_in)
    final_row = final_norm.reshape(1, D_MODEL)
    xt = x.reshape(batch * seq, D_MODEL)
    for l in range(depth):
        xt = _ffn(xt, ffn1_norm[l].reshape(1, D_MODEL), ffn1_w_in[l].astype(BF16),
                  ffn1_w_out[l].astype(BF16), final_row, False)
        a_grp, sc, b_grp, cqkv, cz, gl = _proj(
            xt, mix_norm[l].reshape(1, D_MODEL), w_mix[l], cos, s1, s2, seq)
        a_grp = a_grp.reshape(batch, seq, W_A)
        sc = sc.reshape(batch, seq, W_SC)
        y_a = _dsa_t(a_grp, sc, top_k)
        y_b = _fox_t(b_grp.reshape(batch, seq, W_B), sc, _lane_row(b_forget[l], SC_BF))
        y_c = _gdn(cqkv.reshape(batch, seq, W_CQKV), cz.reshape(batch, seq, W_CZ), sc,
                   conv_w[l], _lane_row(a_log[l], SC_DECAY), _lane_row(dt_bias[l], SC_DECAY),
                   delta_norm[l].reshape(1, C_DIM))
        flat = lambda t: t.reshape(batch * seq, BRANCH)
        xt = _merge(xt, flat(y_a), flat(y_b), flat(y_c), gl, b_gate[l].reshape(1, 3 * D_MODEL),
                    w_branch_a[l].astype(BF16), w_branch_b[l].astype(BF16),
                    w_branch_c[l].astype(BF16), w_out[l].astype(BF16))
        xt = _ffn(xt, ffn2_norm[l].reshape(1, D_MODEL), ffn2_w_in[l].astype(BF16),
                  ffn2_w_out[l].astype(BF16), final_row, l == depth - 1)
    return xt.reshape(batch, seq, D_MODEL)
```

```python
import functools

import numpy as np
import jax
import jax.numpy as jnp
from jax import lax
from jax.experimental import pallas as pl
from jax.experimental.pallas import tpu as pltpu

F32 = jnp.float32
BF16 = jnp.bfloat16

D_MODEL = 1024
BRANCH = 512
A_HEADS = 8
HEAD_DIM = 64
IDX_HEADS = 4
INDEX_TOPK = 256
B_HEADS = 8
C_HEADS = 4
C_DIM = 128
CONV_WIDTH = 4
ROPE_THETA = 500000.0
ROT_DIM = 16
FFN_DIM = 2048
NORM_EPS = 1e-6
L2_EPS = 1e-6
IN_SIZES = (512, 64, 64, 256, 64, 4, 1536, 8, 1536, 512, 4, 4, 3072)

LANES = 128
SC_BF = 0
SC_IW = 8
SC_BETA = 12
SC_DECAY = 16
W_A, W_SC, W_B, W_CQKV, W_CZ, W_G = 1024, 128, 1536, 1536, 512, 3072
W_TOTAL = W_A + W_SC + W_B + W_CQKV + W_CZ + W_G
ROT_TILES = 7

NEG_BIG = -1e30
NEG_MASK = -2e30
VMEM_LIMIT = 56 * 1024 * 1024


def _cparams(sem):
    return pltpu.CompilerParams(dimension_semantics=sem, vmem_limit_bytes=VMEM_LIMIT)


def _dot(a, b):
    return jnp.dot(a, b, preferred_element_type=F32)


def _dot_hi(a, b):
    return jnp.dot(a, b, preferred_element_type=F32, precision=lax.Precision.HIGHEST)


def _dot_solve(a, b):
    a_hi = a.astype(BF16)
    b_hi = b.astype(BF16)
    a_lo = (a - a_hi.astype(F32)).astype(BF16)
    b_lo = (b - b_hi.astype(F32)).astype(BF16)
    m = a.shape[0]
    top = _dot(jnp.concatenate([a_hi, a_lo], axis=0), b_hi)
    return top[0:m] + top[m:2 * m] + _dot(a_hi, b_lo)


def _dot_nt(a, b):
    return lax.dot_general(a, b, (((1,), (1,)), ((), ())), preferred_element_type=F32)


def _rms(x, gain):
    return x * lax.rsqrt(jnp.mean(x * x, axis=-1, keepdims=True) + NORM_EPS) * gain


def _silu(x):
    return x * jax.nn.sigmoid(x)


def _resident(shape):
    nd = len(shape)
    return pl.BlockSpec(shape, lambda *_: (0,) * nd, pipeline_mode=pl.Buffered(1))


FFN_TM = 512
FFN_CHUNK = 512


def _ffn_kernel(x_ref, g_ref, win_ref, wout_ref, fg_ref, o_ref, *, final):
    x = x_ref[...]
    h = _rms(x, g_ref[...]).astype(BF16)
    acc = jnp.zeros(x.shape, F32)
    for c in range(FFN_DIM // FFN_CHUNK):
        lo = c * FFN_CHUNK
        gate = _dot(h, win_ref[:, lo:lo + FFN_CHUNK])
        up = _dot(h, win_ref[:, FFN_DIM + lo:FFN_DIM + lo + FFN_CHUNK])
        act = (_silu(gate) * up).astype(BF16)
        acc = acc + _dot(act, wout_ref[lo:lo + FFN_CHUNK, :])
    y = x + 0.5 * acc
    if final:
        y = _rms(y, fg_ref[...])
    o_ref[...] = y


def _ffn(x, gain, w_in, w_out, final_gain, final):
    n = x.shape[0]
    tm = min(FFN_TM, n)
    return pl.pallas_call(
        functools.partial(_ffn_kernel, final=final),
        grid=(n // tm,),
        in_specs=[
            pl.BlockSpec((tm, D_MODEL), lambda i: (i, 0)),
            _resident((1, D_MODEL)),
            _resident((D_MODEL, 2 * FFN_DIM)),
            _resident((FFN_DIM, D_MODEL)),
            _resident((1, D_MODEL)),
        ],
        out_specs=pl.BlockSpec((tm, D_MODEL), lambda i: (i, 0)),
        out_shape=jax.ShapeDtypeStruct((n, D_MODEL), F32),
        compiler_params=_cparams(("parallel",)),
        name="ffn_half",
    )(x, gain, w_in, w_out, final_gain)


PROJ_TM = 256
PROJ_CHUNK = 512


def _proj_kernel(x_ref, g_ref, w_ref, cos_ref, s1_ref, s2_ref,
                 a_ref, sc_ref, b_ref, cqkv_ref, cz_ref, gl_ref):
    h = _rms(x_ref[...], g_ref[...]).astype(BF16)
    cos, s1, s2 = cos_ref[...], s1_ref[...], s2_ref[...]

    off = 0
    for j in range(W_A // LANES):
        t = _dot(h, w_ref[:, off:off + LANES])
        if j < ROT_TILES:
            t = t * cos + pltpu.roll(t, 8, 1) * s1 + pltpu.roll(t, LANES - 8, 1) * s2
        a_ref[:, off:off + LANES] = t.astype(BF16)
        off += LANES
    sc_ref[...] = _dot(h, w_ref[:, off:off + W_SC])
    off += W_SC
    for ref, width in ((b_ref, W_B), (cqkv_ref, W_CQKV), (cz_ref, W_CZ), (gl_ref, W_G)):
        for lo in range(0, width, PROJ_CHUNK):
            ref[:, lo:lo + PROJ_CHUNK] = _dot(
                h, w_ref[:, off + lo:off + lo + PROJ_CHUNK]).astype(ref.dtype)
        off += width


def _proj(x, gain, w, cos, s1, s2, seq):
    n = x.shape[0]
    tm = min(PROJ_TM, seq)
    per_seq = seq // tm
    row = lambda i: (i, 0)
    tab = lambda i: (i % per_seq, 0)
    widths = (W_A, W_SC, W_B, W_CQKV, W_CZ, W_G)
    dtypes = (BF16, F32, BF16, F32, F32, F32)
    return pl.pallas_call(
        _proj_kernel,
        grid=(n // tm,),
        in_specs=[
            pl.BlockSpec((tm, D_MODEL), row),
            _resident((1, D_MODEL)),
            _resident((D_MODEL, W_TOTAL)),
            pl.BlockSpec((tm, LANES), tab),
            pl.BlockSpec((tm, LANES), tab),
            pl.BlockSpec((tm, LANES), tab),
        ],
        out_specs=[pl.BlockSpec((tm, wd), row) for wd in widths],
        out_shape=[jax.ShapeDtypeStruct((n, wd), dt) for wd, dt in zip(widths, dtypes)],
        compiler_params=_cparams(("parallel",)),
        name="mixer_proj",
    )(x, gain, w, cos, s1, s2)


DSA_TQ = 512
DSA_CK = 512
INT_MIN = -2 ** 31
KEY_NEG_INF = int(np.array(-np.inf, np.float32).view(np.int32)) ^ 0x7FFFFFFF
SLAB = 64
SUM_ROW = HEAD_DIM
EXP_ROWS = 128


def _dsa_t_kernel(qa_ref, qi_ref, kk_ref, vv_ref, sc_ref, o_ref,
                  vt_ref, key_ref, bias_ref, qs_ref, m_ref, alpha_ref, acc_ref,
                  s_ref, p_ref, *, top_k, ck):
    i = pl.program_id(1)
    tq = qa_ref.shape[1]
    n_chunks_total = key_ref.shape[0]
    nck = (i * tq + tq + ck - 1) // ck
    lane = lax.broadcasted_iota(jnp.int32, (tq, LANES), 1)
    lo_half = lane < HEAD_DIM
    scale = HEAD_DIM ** -0.5

    @pl.when(i == 0)
    def _():
        ones_row = lax.broadcasted_iota(jnp.int32, (LANES, ck), 0) == SUM_ROW
        for c in range(n_chunks_total):
            vt = vv_ref[0, c * ck:(c + 1) * ck, :].astype(F32).T
            vt_ref[c] = jnp.where(ones_row, 1.0, vt).astype(BF16)

    def put(h, t):
        qs_ref[h * tq:(h + 1) * tq, :] = (t * scale).astype(BF16)

    for j in range(A_HEADS // 2):
        t = qa_ref[0, :, j * LANES:(j + 1) * LANES].astype(F32)
        put(2 * j, jnp.where(lo_half, t, 0.0))
        put(2 * j + 1, jnp.where(lo_half, pltpu.roll(t, HEAD_DIM, 1), 0.0))
    for j in range(IDX_HEADS // 2):
        t = qi_ref[0, :, j * LANES:(j + 1) * LANES].astype(F32)
        put(A_HEADS + 2 * j, jnp.where(lo_half, 0.0, pltpu.roll(t, HEAD_DIM, 1)))
        put(A_HEADS + 2 * j + 1, jnp.where(lo_half, 0.0, t))

    w_rows = (sc_ref[0] * (IDX_HEADS ** -0.5)).T
    key_pos = lax.broadcasted_iota(jnp.int32, (ck, tq), 0)
    qry_pos = i * tq + lax.broadcasted_iota(jnp.int32, (ck, tq), 1)

    def key_rows(c):
        return pl.ds(pl.multiple_of(c * ck, ck), ck)

    def score_chunk(c, _):
        kc = kk_ref[0, key_rows(c), :]
        score = None
        for h in range(IDX_HEADS):
            rel = jnp.maximum(
                _dot_nt(kc, qs_ref[(A_HEADS + h) * tq:(A_HEADS + h + 1) * tq, :]), 0.0)
            term = rel * w_rows[SC_IW + h:SC_IW + h + 1, :]
            score = term if score is None else score + term
        score = jnp.where(score == 0.0, 0.0, score)
        score = jnp.where(key_pos + c * ck <= qry_pos, score, -jnp.inf)
        bits = lax.bitcast_convert_type(score, jnp.int32)
        key_ref[c] = jnp.where(bits < 0, bits ^ 0x7FFFFFFF, bits)
        return 0

    lax.fori_loop(0, nck, score_chunk, 0)

    def count(pred):
        def body(c, acc):
            for r0 in range(0, ck, SLAB):
                acc = acc + jnp.where(pred(key_ref[c, r0:r0 + SLAB, :]), 1.0, 0.0)
            return acc
        acc = lax.fori_loop(0, nck, body, jnp.zeros((SLAB, tq), F32))
        return jnp.sum(acc, axis=0, keepdims=True)

    kf = float(top_k)

    def bit_step(b, thr):
        cand = thr + lax.shift_left(jnp.int32(1), 31 - b)
        cand_slab = jnp.broadcast_to(cand, (SLAB, tq))
        cnt = count(lambda kc: kc >= cand_slab)
        return jnp.where(cnt >= kf, cand, thr)

    thr = lax.fori_loop(0, 32, bit_step, jnp.full((1, tq), INT_MIN, jnp.int32))
    thr = jnp.maximum(thr, KEY_NEG_INF + 1)
    thr_slab = jnp.broadcast_to(thr, (SLAB, tq))
    cnt_ge = count(lambda kc: kc >= thr_slab)

    def write_bias(c, _):
        bias_ref[c] = jnp.where(key_ref[c] >= thr, 0.0, NEG_MASK)
        return 0

    lax.fori_loop(0, nck, write_bias, 0)

    @pl.when(jnp.max(cnt_ge) > kf)
    def _():
        need = kf - count(lambda kc: kc > thr_slab)
        r = lax.broadcasted_iota(jnp.int32, (ck, ck), 0)
        cidx = lax.broadcasted_iota(jnp.int32, (ck, ck), 1)
        lower = jnp.where(cidx <= r, 1.0, 0.0).astype(BF16)

        def tie_chunk(c, before):
            kc = key_ref[c]
            tie = kc == thr
            rank = before + _dot(lower, jnp.where(tie, 1.0, 0.0).astype(BF16))
            keep = (kc > thr) | (tie & (rank <= need))
            bias_ref[c] = jnp.where(keep, 0.0, NEG_MASK)
            return rank[ck - 1:ck, :]

        lax.fori_loop(0, nck, tie_chunk, jnp.zeros((1, tq), F32))

    m_ref[...] = jnp.full(m_ref.shape, NEG_BIG, F32)
    acc_ref[...] = jnp.zeros(acc_ref.shape, F32)
    reps = ck // 8

    n_att = A_HEADS * tq

    def attend(c, _):
        s_ref[...] = _dot_nt(kk_ref[0, key_rows(c), :], qs_ref[0:n_att, :])
        for tile in range(n_att // LANES):
            cols = slice(tile * LANES, (tile + 1) * LANES)
            qcols = slice((tile * LANES) % tq, (tile * LANES) % tq + LANES)
            part = None
            for r0 in range(0, ck, EXP_ROWS):
                rows = slice(r0, r0 + EXP_ROWS)
                s = s_ref[rows, cols] + bias_ref[c, rows, qcols]
                s_ref[rows, cols] = s
                unit = jnp.max(s.reshape(EXP_ROWS // 8, 8, LANES), axis=0)
                part = unit if part is None else jnp.maximum(part, unit)
            m_old = m_ref[:, cols]
            m_new = jnp.maximum(m_old, jnp.max(part, axis=0, keepdims=True))
            alpha_ref[:, cols] = jnp.exp(m_old - m_new)
            m_ref[:, cols] = m_new
        for tile in range(n_att // LANES):
            cols = slice(tile * LANES, (tile + 1) * LANES)
            for r0 in range(0, ck, EXP_ROWS):
                rows = slice(r0, r0 + EXP_ROWS)
                p = jnp.exp(s_ref[rows, cols] - jnp.tile(m_ref[:, cols], (EXP_ROWS // 8, 1)))
                p_ref[rows, cols] = p.astype(BF16)
        acc_ref[...] = (jnp.tile(alpha_ref[...], (LANES // 8, 1)) * acc_ref[...]
                        + _dot(vt_ref[c], p_ref[...]))
        return 0

    lax.fori_loop(0, nck, attend, 0)

    value_rows = lax.broadcasted_iota(jnp.int32, (LANES, tq), 0) < HEAD_DIM
    for j in range(A_HEADS // 2):
        outs = []
        for h in (2 * j, 2 * j + 1):
            cols = slice(h * tq, (h + 1) * tq)
            out_t = jnp.where(value_rows,
                              acc_ref[:, cols] / acc_ref[SUM_ROW:SUM_ROW + 1, cols], 0.0)
            outs.append(out_t.T)
        o_ref[0, :, j * LANES:(j + 1) * LANES] = (
            outs[0] + pltpu.roll(outs[1], HEAD_DIM, 1)).astype(o_ref.dtype)


def _dsa_t(a_grp, sc, top_k):
    b, s, _ = a_grp.shape
    tq = min(DSA_TQ, s)
    ck = min(DSA_CK, s)
    return pl.pallas_call(
        functools.partial(_dsa_t_kernel, top_k=top_k, ck=ck),
        grid=(b, s // tq),
        in_specs=[
            pl.BlockSpec((1, tq, 512), lambda bi, i: (bi, i, 0)),
            pl.BlockSpec((1, tq, 256), lambda bi, i: (bi, i, 2)),
            pl.BlockSpec((1, s, LANES), lambda bi, i: (bi, 0, 6)),
            pl.BlockSpec((1, s, LANES), lambda bi, i: (bi, 0, 7)),
            pl.BlockSpec((1, tq, LANES), lambda bi, i: (bi, i, 0)),
        ],
        out_specs=pl.BlockSpec((1, tq, BRANCH), lambda bi, i: (bi, i, 0)),
        out_shape=jax.ShapeDtypeStruct((b, s, BRANCH), BF16),
        scratch_shapes=[
            pltpu.VMEM((s // ck, LANES, ck), BF16),
            pltpu.VMEM((s // ck, ck, tq), jnp.int32),
            pltpu.VMEM((s // ck, ck, tq), F32),
            pltpu.VMEM(((A_HEADS + IDX_HEADS) * tq, LANES), BF16),
            pltpu.VMEM((8, A_HEADS * tq), F32),
            pltpu.VMEM((8, A_HEADS * tq), F32),
            pltpu.VMEM((LANES, A_HEADS * tq), F32),
            pltpu.VMEM((ck, A_HEADS * tq), F32),
            pltpu.VMEM((ck, A_HEADS * tq), BF16),
        ],
        compiler_params=_cparams(("parallel", "arbitrary")),
        name="dsa_attention",
    )(a_grp, a_grp, a_grp, a_grp, sc)


FOX_T = 256
BIAS_LANE = HEAD_DIM
BIAS_TERMS = 3


def _fox_t_kernel(q_ref, k_ref, v_ref, sc_ref, bf_ref, o_ref,
                  kaug_ref, vt_ref, qz_ref, m_ref, l_ref, alpha_ref, acc_ref, s_ref, p_ref, *, t):
    i = pl.program_id(1)
    seq = k_ref.shape[1]
    n_chunks = seq // t
    pairs = B_HEADS // 2
    lane = lax.broadcasted_iota(jnp.int32, (t, LANES), 1)
    lo_half = lane < HEAD_DIM

    def bias_lanes(h):
        first = BIAS_LANE + BIAS_TERMS * h
        return (lane >= first) & (lane < first + BIAS_TERMS)

    @pl.when(i == 0)
    def _():
        r = lax.broadcasted_iota(jnp.int32, (t, t), 0)
        c = lax.broadcasted_iota(jnp.int32, (t, t), 1)
        lower = jnp.where(c <= r, 1.0, 0.0)
        src = lax.broadcasted_iota(jnp.int32, (LANES, LANES), 0)
        dst = lax.broadcasted_iota(jnp.int32, (LANES, LANES), 1)

        def build(n, carry):
            rows = pl.ds(pl.multiple_of(n * t, t), t)
            x = sc_ref[0, rows, :] + bf_ref[...]
            log_f = jnp.minimum(x, 0.0) - jnp.log1p(jnp.exp(-jnp.abs(x)))
            cum = _dot_hi(lower, log_f) + carry
            nc = -cum
            hi = nc.astype(BF16)
            rest = nc - hi.astype(F32)
            mid = rest.astype(BF16)
            lo = (rest - mid.astype(F32)).astype(BF16)
            placed = None
            for j, term in enumerate((hi, mid, lo)):
                place = jnp.where((src >= SC_BF) & (src < SC_BF + B_HEADS)
                                  & (dst == BIAS_LANE + BIAS_TERMS * (src - SC_BF) + j), 1.0, 0.0)
                part = _dot(term, place.astype(BF16))
                placed = part if placed is None else placed + part
            for h in range(B_HEADS):
                hp = h // 2
                kt = k_ref[0, rows, hp * LANES:(hp + 1) * LANES].astype(F32)
                if h % 2:
                    kt = pltpu.roll(kt, HEAD_DIM, 1)
                tile = jnp.where(lo_half, kt, jnp.where(bias_lanes(h), placed, 0.0))
                kaug_ref[h, rows, :] = tile.astype(BF16)
            for hp in range(pairs):
                vt_ref[hp, n] = v_ref[0, rows, hp * LANES:(hp + 1) * LANES].astype(F32).T.astype(BF16)
            return cum[t - 1:t, :]

        lax.fori_loop(0, n_chunks, build, jnp.zeros((1, LANES), F32))

    for hp in range(pairs):
        q = q_ref[0, :, hp * LANES:(hp + 1) * LANES].astype(F32) * (HEAD_DIM ** -0.5)
        qz_ref[(2 * hp) * t:(2 * hp + 1) * t, :] = jnp.where(
            lo_half, q, jnp.where(bias_lanes(2 * hp), 1.0, 0.0)).astype(BF16)
        qz_ref[(2 * hp + 1) * t:(2 * hp + 2) * t, :] = jnp.where(
            lo_half, pltpu.roll(q, HEAD_DIM, 1), jnp.where(bias_lanes(2 * hp + 1), 1.0, 0.0)).astype(BF16)
    causal = (lax.broadcasted_iota(jnp.int32, (t, t), 0)
              <= lax.broadcasted_iota(jnp.int32, (t, t), 1))
    m_ref[...] = jnp.full(m_ref.shape, NEG_BIG, F32)
    l_ref[...] = jnp.zeros(l_ref.shape, F32)
    acc_ref[...] = jnp.zeros(acc_ref.shape, F32)

    def logits(c, slot):
        rows = pl.ds(pl.multiple_of(c * t, t), t)
        for h in range(B_HEADS):
            cols = slice(h * t, (h + 1) * t)
            s_ref[slot, :, cols] = _dot_nt(kaug_ref[h, rows, :], qz_ref[cols, :])

    def attend(c, slot, masked):
        if not masked:
            logits(c + 1, 1 - slot)
        for h in range(B_HEADS):
            cols = slice(h * t, (h + 1) * t)
            s = s_ref[slot, :, cols]
            if masked:
                s = jnp.where(causal, s, -jnp.inf)
            m_old = m_ref[:, cols]
            m_new = jnp.maximum(m_old, jnp.max(s, axis=0, keepdims=True))
            alpha = jnp.exp(m_old - m_new)
            p = jnp.exp(s - jnp.tile(m_new, (t // 8, 1)))
            l_ref[:, cols] = alpha * l_ref[:, cols] + jnp.sum(p, axis=0, keepdims=True)
            m_ref[:, cols] = m_new
            alpha_ref[:, cols] = alpha
            p_ref[:, cols] = p.astype(BF16)
        for h in range(B_HEADS):
            cols = slice(h * t, (h + 1) * t)
            acc_ref[:, cols] = (jnp.tile(alpha_ref[:, cols], (LANES // 8, 1)) * acc_ref[:, cols]
                                + _dot(vt_ref[h // 2, c], p_ref[:, cols]))

    def attend_pair(k, _):
        attend(2 * k, 0, masked=False)
        attend(2 * k + 1, 1, masked=False)
        return 0

    logits(0, 0)
    lax.fori_loop(0, i // 2, attend_pair, 0)

    @pl.when(i % 2 == 0)
    def _():
        attend(i, 0, masked=True)

    @pl.when(i % 2 == 1)
    def _():
        attend(i - 1, 0, masked=False)
        attend(i, 1, masked=True)

    for hp in range(pairs):
        outs = []
        for h in (2 * hp, 2 * hp + 1):
            cols = slice(h * t, (h + 1) * t)
            out_t = acc_ref[:, cols] / jnp.tile(l_ref[:, cols], (LANES // 8, 1))
            outs.append(out_t.T)
        o_ref[0, :, hp * LANES:(hp + 1) * LANES] = jnp.where(
            lo_half, outs[0], outs[1]).astype(o_ref.dtype)


def _fox_t(b_grp, sc, bf_row):
    b, s, _ = b_grp.shape
    t = min(FOX_T, s)
    pairs = B_HEADS // 2
    return pl.pallas_call(
        functools.partial(_fox_t_kernel, t=t),
        grid=(b, s // t),
        in_specs=[
            pl.BlockSpec((1, t, BRANCH), lambda bi, i: (bi, i, 0)),
            pl.BlockSpec((1, s, BRANCH), lambda bi, i: (bi, 0, 1)),
            pl.BlockSpec((1, s, BRANCH), lambda bi, i: (bi, 0, 2)),
            pl.BlockSpec((1, s, LANES), lambda bi, i: (bi, 0, 0)),
            _resident((1, LANES)),
        ],
        out_specs=pl.BlockSpec((1, t, BRANCH), lambda bi, i: (bi, i, 0)),
        out_shape=jax.ShapeDtypeStruct((b, s, BRANCH), BF16),
        scratch_shapes=[
            pltpu.VMEM((B_HEADS, s, LANES), BF16),
            pltpu.VMEM((pairs, s // t, LANES, t), BF16),
            pltpu.VMEM((B_HEADS * t, LANES), BF16),
            pltpu.VMEM((8, B_HEADS * t), F32),
            pltpu.VMEM((8, B_HEADS * t), F32),
            pltpu.VMEM((8, B_HEADS * t), F32),
            pltpu.VMEM((LANES, B_HEADS * t), F32),
            pltpu.VMEM((2, t, B_HEADS * t), F32),
            pltpu.VMEM((t, B_HEADS * t), BF16),
        ],
        compiler_params=_cparams(("parallel", "arbitrary")),
        name="fox_attention",
    )(b_grp, b_grp, b_grp, sc, bf_row)


GDN_TS = 512
GDN_C = 128
TAIL = 8
NEUMANN_ROUNDS = 6


def _gdn_kernel(x_ref, z_ref, sc_ref, cw_ref, alog_ref, dtb_ref, dn_ref, o_ref,
                state_ref, tail_ref, ext_ref, act_ref, np_ref, rhs_ref,
                qk_ref, qg_ref, kdt_ref, dec_ref, *, ts):
    j = pl.program_id(1)
    width = x_ref.shape[2]
    n_chunks = ts // GDN_C
    chains = [(h, n) for n in range(n_chunks) for h in range(C_HEADS)]

    @pl.when(j == 0)
    def _():
        state_ref[...] = jnp.zeros_like(state_ref)
        tail_ref[...] = jnp.zeros_like(tail_ref)

    x = x_ref[0]
    ext_ref[0:TAIL, :] = tail_ref[...]
    ext_ref[TAIL:TAIL + ts, :] = x
    tail_ref[...] = x[ts - TAIL:ts, :]
    conv = jnp.zeros((ts, width), F32)
    for tap in range(CONV_WIDTH):
        start = TAIL - (CONV_WIDTH - 1) + tap
        conv = conv + cw_ref[tap:tap + 1, :] * ext_ref[start:start + ts, :]
    act_ref[...] = _silu(conv)

    sc = sc_ref[0]
    beta_all = jax.nn.sigmoid(sc)
    xg = sc + dtb_ref[...]
    softplus = jnp.maximum(xg, 0.0) + jnp.log1p(jnp.exp(-jnp.abs(xg)))
    g_all = -jnp.exp(alog_ref[...]) * softplus
    r = lax.broadcasted_iota(jnp.int32, (ts, ts), 0)
    c = lax.broadcasted_iota(jnp.int32, (ts, ts), 1)
    same_chunk_lower = jnp.where((c <= r) & (r // GDN_C == c // GDN_C), 1.0, 0.0)
    gc_all = _dot_hi(same_chunk_lower, g_all)
    gc_t = gc_all.T

    ri = lax.broadcasted_iota(jnp.int32, (GDN_C, GDN_C), 0)
    ci = lax.broadcasted_iota(jnp.int32, (GDN_C, GDN_C), 1)
    incl = ri >= ci
    strict = ri > ci

    def l2n(t):
        return t * lax.rsqrt(jnp.sum(t * t, axis=-1, keepdims=True) + L2_EPS)

    for idx, (h, n) in enumerate(chains):
        rows = slice(n * GDN_C, (n + 1) * GDN_C)
        q = l2n(act_ref[rows, h * C_DIM:(h + 1) * C_DIM]) * (C_DIM ** -0.5)
        k = l2n(act_ref[rows, BRANCH + h * C_DIM:BRANCH + (h + 1) * C_DIM])
        v = act_ref[rows, 2 * BRANCH + h * C_DIM:2 * BRANCH + (h + 1) * C_DIM]
        beta = beta_all[rows, SC_BETA + h:SC_BETA + h + 1]
        gcol = gc_all[rows, SC_DECAY + h:SC_DECAY + h + 1]
        grow = gc_t[SC_DECAY + h:SC_DECAY + h + 1, n * GDN_C:(n + 1) * GDN_C]
        g_last = gcol[GDN_C - 1:GDN_C, :]
        decay = jnp.exp(jnp.where(incl, gcol - grow, -jnp.inf))
        kb = k * beta
        k16 = k.astype(BF16)
        q16 = q.astype(BF16)
        a_mat = jnp.where(strict, _dot_nt(kb.astype(BF16), k16) * decay, 0.0)
        np_ref[idx, :, 0:GDN_C] = -a_mat
        np_ref[idx, :, GDN_C:2 * GDN_C] = _dot_solve(a_mat, a_mat)
        rhs_ref[idx, :, 0:C_DIM] = v * beta
        rhs_ref[idx, :, C_DIM:2 * C_DIM] = kb * jnp.exp(gcol)
        qk_ref[idx] = jnp.where(incl, _dot_nt(q16, k16) * decay, 0.0).astype(BF16)
        qg_ref[idx] = (q * jnp.exp(gcol)).astype(BF16)
        kdt_ref[idx] = (k * jnp.exp(g_last - gcol)).T.astype(BF16)
        dec_ref[idx] = jnp.broadcast_to(jnp.exp(g_last), (1, C_DIM))

    for rnd in range(1, NEUMANN_ROUNDS + 1):
        last = rnd == NEUMANN_ROUNDS
        for idx in range(len(chains)):
            n_old = np_ref[idx, :, 0:GDN_C]
            p_old = np_ref[idx, :, GDN_C:2 * GDN_C]
            if last:
                np_ref[idx, :, 0:GDN_C] = n_old + p_old + _dot_solve(p_old, n_old)
            else:
                prod = _dot_solve(p_old, np_ref[idx])
                np_ref[idx, :, 0:GDN_C] = n_old + p_old + prod[:, 0:GDN_C]
                np_ref[idx, :, GDN_C:2 * GDN_C] = prod[:, GDN_C:2 * GDN_C]

    for idx in range(len(chains)):
        rhs = rhs_ref[idx]
        rhs_ref[idx] = rhs + _dot_solve(np_ref[idx, :, 0:GDN_C], rhs)

    for idx, (h, n) in enumerate(chains):
        rows = slice(n * GDN_C, (n + 1) * GDN_C)
        state = state_ref[h]
        s16 = state.astype(BF16)
        v_new = rhs_ref[idx, :, 0:C_DIM] - _dot(rhs_ref[idx, :, C_DIM:2 * C_DIM].astype(BF16), s16)
        vn16 = v_new.astype(BF16)
        o = _dot(qg_ref[idx], s16) + _dot(qk_ref[idx], vn16)
        state_ref[h] = state * dec_ref[idx] + _dot(kdt_ref[idx], vn16)
        z = z_ref[0, rows, h * C_DIM:(h + 1) * C_DIM]
        y = _rms(o, dn_ref[...]) * _silu(z)
        o_ref[0, rows, h * C_DIM:(h + 1) * C_DIM] = y.astype(o_ref.dtype)


def _gdn(cqkv, cz, sc, conv_w, alog_row, dtb_row, dn_row):
    b, s, width = cqkv.shape
    ts = min(GDN_TS, s)
    nch = C_HEADS * (ts // GDN_C)
    blk = lambda bi, j: (bi, j, 0)
    return pl.pallas_call(
        functools.partial(_gdn_kernel, ts=ts),
        grid=(b, s // ts),
        in_specs=[
            pl.BlockSpec((1, ts, width), blk),
            pl.BlockSpec((1, ts, BRANCH), blk),
            pl.BlockSpec((1, ts, LANES), blk),
            _resident((CONV_WIDTH, width)),
            _resident((1, LANES)),
            _resident((1, LANES)),
            _resident((1, C_DIM)),
        ],
        out_specs=pl.BlockSpec((1, ts, BRANCH), blk),
        out_shape=jax.ShapeDtypeStruct((b, s, BRANCH), BF16),
        scratch_shapes=[
            pltpu.VMEM((C_HEADS, C_DIM, C_DIM), F32),
            pltpu.VMEM((TAIL, width), F32),
            pltpu.VMEM((TAIL + ts, width), F32),
            pltpu.VMEM((ts, width), F32),
            pltpu.VMEM((nch, GDN_C, 2 * GDN_C), F32),
            pltpu.VMEM((nch, GDN_C, 2 * C_DIM), F32),
            pltpu.VMEM((nch, GDN_C, GDN_C), BF16),
            pltpu.VMEM((nch, GDN_C, C_DIM), BF16),
            pltpu.VMEM((nch, C_DIM, GDN_C), BF16),
            pltpu.VMEM((nch, 1, C_DIM), F32),
        ],
        compiler_params=_cparams(("parallel", "arbitrary")),
        name="gated_deltanet",
    )(cqkv, cz, sc, conv_w, alog_row, dtb_row, dn_row)


MERGE_TM = 512


def _merge_kernel(x_ref, ya_ref, yb_ref, yc_ref, gl_ref, bg_ref,
                  wa_ref, wb_ref, wc_ref, wo_ref, o_ref):
    merged = None
    for n, (y_ref, w_ref) in enumerate(((ya_ref, wa_ref), (yb_ref, wb_ref), (yc_ref, wc_ref))):
        cols = slice(n * D_MODEL, (n + 1) * D_MODEL)
        gate = jax.nn.sigmoid(gl_ref[:, cols] + bg_ref[:, cols])
        term = gate * _dot(y_ref[...], w_ref[...])
        merged = term if merged is None else merged + term
    o_ref[...] = x_ref[...] + _dot(merged.astype(BF16), wo_ref[...])


def _merge(x, ya, yb, yc, gl, b_gate, wa, wb, wc, wo):
    n = x.shape[0]
    tm = min(MERGE_TM, n)
    row = lambda i: (i, 0)
    return pl.pallas_call(
        _merge_kernel,
        grid=(n // tm,),
        in_specs=[
            pl.BlockSpec((tm, D_MODEL), row),
            pl.BlockSpec((tm, BRANCH), row),
            pl.BlockSpec((tm, BRANCH), row),
            pl.BlockSpec((tm, BRANCH), row),
            pl.BlockSpec((tm, 3 * D_MODEL), row),
            _resident((1, 3 * D_MODEL)),
            _resident((BRANCH, D_MODEL)),
            _resident((BRANCH, D_MODEL)),
            _resident((BRANCH, D_MODEL)),
            _resident((D_MODEL, D_MODEL)),
        ],
        out_specs=pl.BlockSpec((tm, D_MODEL), row),
        out_shape=jax.ShapeDtypeStruct((n, D_MODEL), F32),
        compiler_params=_cparams(("parallel",)),
        name="gated_merge",
    )(x, ya, yb, yc, gl, b_gate, wa, wb, wc, wo)


def _w_in_pieces():
    offs = np.concatenate([[0], np.cumsum(IN_SIZES)])
    names = ("a_q", "a_k", "a_v", "i_q", "i_k", "i_w", "b_qkv", "b_f", "c_qkv", "c_z",
             "c_beta", "c_a", "gates")
    src = {n: (int(offs[i]), int(IN_SIZES[i])) for i, n in enumerate(names)}
    order = (("a_q", 0), ("i_q", 512), ("a_k", 768), ("i_k", 832), ("a_v", 896),
             ("b_f", W_A + SC_BF), ("i_w", W_A + SC_IW), ("c_beta", W_A + SC_BETA),
             ("c_a", W_A + SC_DECAY), ("b_qkv", W_A + W_SC), ("c_qkv", W_A + W_SC + W_B),
             ("c_z", W_A + W_SC + W_B + W_CQKV), ("gates", W_A + W_SC + W_B + W_CQKV + W_CZ))
    pieces = []
    for name, dst in order:
        s0, width = src[name]
        done = 0
        while done < width:
            step = min(width - done, LANES - (dst + done) % LANES)
            pieces.append((dst + done, s0 + done, step))
            done += step
    return pieces


RELAYOUT_TM = 256


def _relayout_kernel(w_ref, tail_ref, o_ref, *, aligned_cols):
    by_tile = {}
    for dst, src, n in _w_in_pieces():
        by_tile.setdefault(dst // LANES, []).append((dst % LANES, src, n))
    row = lax.broadcasted_iota(jnp.int32, (2 * LANES, LANES), 0)
    col = lax.broadcasted_iota(jnp.int32, (2 * LANES, LANES), 1)

    def src_tile(start):
        if start + LANES <= aligned_cols:
            return w_ref[0, :, start:start + LANES]
        return tail_ref[0, :, start - aligned_cols:start - aligned_cols + LANES]

    for tile in range(W_TOTAL // LANES):
        acc = None
        for dst_lane, src, n in by_tile.get(tile, ()):
            base = (src // LANES) * LANES
            shift = src - base - dst_lane
            select = (row == col + shift) & (col >= dst_lane) & (col < dst_lane + n)
            window = jnp.concatenate([src_tile(base), src_tile(base + LANES)], axis=1)
            part = _dot(window.astype(BF16), jnp.where(select, 1.0, 0.0).astype(BF16))
            acc = part if acc is None else acc + part
        if acc is None:
            acc = jnp.zeros((w_ref.shape[1], LANES), F32)
        o_ref[0, :, tile * LANES:(tile + 1) * LANES] = acc.astype(BF16)


def _layout_w_in(w_in):
    depth, rows, cols = w_in.shape
    aligned_cols = (cols // LANES) * LANES
    tail = jnp.pad(w_in[:, :, aligned_cols:], ((0, 0), (0, 0), (0, 2 * LANES - (cols - aligned_cols))))
    tm = min(RELAYOUT_TM, rows)
    return pl.pallas_call(
        functools.partial(_relayout_kernel, aligned_cols=aligned_cols),
        grid=(depth, rows // tm),
        in_specs=[pl.BlockSpec((1, tm, cols), lambda l, i: (l, i, 0)),
                  pl.BlockSpec((1, tm, 2 * LANES), lambda l, i: (l, i, 0))],
        out_specs=pl.BlockSpec((1, tm, W_TOTAL), lambda l, i: (l, i, 0)),
        out_shape=jax.ShapeDtypeStruct((depth, rows, W_TOTAL), BF16),
        compiler_params=_cparams(("parallel", "parallel")),
        name="w_in_relayout",
    )(w_in, tail)


def _rotary_tables(seq):
    pos = jnp.arange(seq, dtype=F32)
    inv_freq = jnp.power(ROPE_THETA, -jnp.arange(0, ROT_DIM, 2, dtype=F32) / ROT_DIM)
    ang = pos[:, None] * inv_freq[None, :]
    cos, sin = jnp.cos(ang), jnp.sin(ang)
    half = ROT_DIM // 2
    ones = jnp.ones((seq, HEAD_DIM - ROT_DIM), F32)
    zeros_h = jnp.zeros((seq, half), F32)
    zeros_r = jnp.zeros((seq, HEAD_DIM - ROT_DIM), F32)
    c64 = jnp.concatenate([cos, cos, ones], axis=1)
    s1_64 = jnp.concatenate([zeros_h, sin, zeros_r], axis=1)
    s2_64 = jnp.concatenate([-sin, zeros_h, zeros_r], axis=1)
    twice = lambda t: jnp.concatenate([t, t], axis=1)
    return twice(c64), twice(s1_64), twice(s2_64)


def _lane_row(values, start):
    return jnp.zeros((1, LANES), F32).at[0, start:start + values.shape[0]].set(values.astype(F32))


def kernel(x, ffn1_norm, ffn1_w_in, ffn1_w_out, mix_norm, w_in, b_gate, b_forget, conv_w, a_log, dt_bias, delta_norm, w_branch_a, w_branch_b, w_branch_c, w_out, ffn2_norm, ffn2_w_in, ffn2_w_out, final_norm):
    batch, seq, _ = x.shape
    depth = w_in.shape[0]
    top_k = min(INDEX_TOPK, seq // 4)
    cos, s1, s2 = _rotary_tables(seq)
    w_mix = _layout_w_in(w_in)
    final_row = final_norm.reshape(1, D_MODEL)
    xt = x.reshape(batch * seq, D_MODEL)
    for l in range(depth):
        xt = _ffn(xt, ffn1_norm[l].reshape(1, D_MODEL), ffn1_w_in[l].astype(BF16),
                  ffn1_w_out[l].astype(BF16), final_row, False)
        a_grp, sc, b_grp, cqkv, cz, gl = _proj(
            xt, mix_norm[l].reshape(1, D_MODEL), w_mix[l], cos, s1, s2, seq)
        a_grp = a_grp.reshape(batch, seq, W_A)
        sc = sc.reshape(batch, seq, W_SC)
        y_a = _dsa_t(a_grp, sc, top_k)
        y_b = _fox_t(b_grp.reshape(batch, seq, W_B), sc, _lane_row(b_forget[l], SC_BF))
        y_c = _gdn(cqkv.reshape(batch, seq, W_CQKV), cz.reshape(batch, seq, W_CZ), sc,
                   conv_w[l], _lane_row(a_log[l], SC_DECAY), _lane_row(dt_bias[l], SC_DECAY),
                   delta_norm[l].reshape(1, C_DIM))
        flat = lambda t: t.reshape(batch * seq, BRANCH)
        xt = _merge(xt, flat(y_a), flat(y_b), flat(y_c), gl, b_gate[l].reshape(1, 3 * D_MODEL),
                    w_branch_a[l].astype(BF16), w_branch_b[l].astype(BF16),
                    w_branch_c[l].astype(BF16), w_out[l].astype(BF16))
        xt = _ffn(xt, ffn2_norm[l].reshape(1, D_MODEL), ffn2_w_in[l].astype(BF16),
                  ffn2_w_out[l].astype(BF16), final_row, l == depth - 1)
    return xt.reshape(batch, seq, D_MODEL)
```

```python
import functools

import numpy as np
import jax
import jax.numpy as jnp
from jax import lax
from jax.experimental import pallas as pl
from jax.experimental.pallas import tpu as pltpu

F32 = jnp.float32
BF16 = jnp.bfloat16

D_MODEL = 1024
BRANCH = 512
A_HEADS = 8
HEAD_DIM = 64
IDX_HEADS = 4
INDEX_TOPK = 256
B_HEADS = 8
C_HEADS = 4
C_DIM = 128
CONV_WIDTH = 4
ROPE_THETA = 500000.0
ROT_DIM = 16
FFN_DIM = 2048
NORM_EPS = 1e-6
L2_EPS = 1e-6
IN_SIZES = (512, 64, 64, 256, 64, 4, 1536, 8, 1536, 512, 4, 4, 3072)

LANES = 128
SC_BF = 0
SC_IW = 8
SC_BETA = 12
SC_DECAY = 16
W_A, W_SC, W_B, W_CQKV, W_CZ, W_G = 1024, 128, 1536, 1536, 512, 3072
W_TOTAL = W_A + W_SC + W_B + W_CQKV + W_CZ + W_G
ROT_TILES = 7

NEG_BIG = -1e30
NEG_MASK = -2e30
VMEM_LIMIT = 56 * 1024 * 1024


def _cparams(sem):
    return pltpu.CompilerParams(dimension_semantics=sem, vmem_limit_bytes=VMEM_LIMIT)


def _dot(a, b):
    return jnp.dot(a, b, preferred_element_type=F32)


def _dot_hi(a, b):
    return jnp.dot(a, b, preferred_element_type=F32, precision=lax.Precision.HIGHEST)


def _dot_solve(a, b):
    a_hi = a.astype(BF16)
    b_hi = b.astype(BF16)
    a_lo = (a - a_hi.astype(F32)).astype(BF16)
    b_lo = (b - b_hi.astype(F32)).astype(BF16)
    m = a.shape[0]
    top = _dot(jnp.concatenate([a_hi, a_lo], axis=0), b_hi)
    return top[0:m] + top[m:2 * m] + _dot(a_hi, b_lo)


def _dot_nt(a, b):
    return lax.dot_general(a, b, (((1,), (1,)), ((), ())), preferred_element_type=F32)


def _rms(x, gain):
    return x * lax.rsqrt(jnp.mean(x * x, axis=-1, keepdims=True) + NORM_EPS) * gain


def _silu(x):
    return x * jax.nn.sigmoid(x)


def _resident(shape):
    nd = len(shape)
    return pl.BlockSpec(shape, lambda *_: (0,) * nd, pipeline_mode=pl.Buffered(1))


FFN_TM = 1024
FFN_CHUNK = 512


def _ffn_kernel(x_ref, g_ref, win_ref, wout_ref, fg_ref, o_ref, *, final):
    x = x_ref[...]
    h = _rms(x, g_ref[...]).astype(BF16)
    acc = jnp.zeros(x.shape, F32)
    for c in range(FFN_DIM // FFN_CHUNK):
        lo = c * FFN_CHUNK
        gate = _dot(h, win_ref[:, lo:lo + FFN_CHUNK])
        up = _dot(h, win_ref[:, FFN_DIM + lo:FFN_DIM + lo + FFN_CHUNK])
        act = (_silu(gate) * up).astype(BF16)
        acc = acc + _dot(act, wout_ref[lo:lo + FFN_CHUNK, :])
    y = x + 0.5 * acc
    if final:
        y = _rms(y, fg_ref[...])
    o_ref[...] = y


def _ffn(x, gain, w_in, w_out, final_gain, final):
    n = x.shape[0]
    tm = min(FFN_TM, n)
    return pl.pallas_call(
        functools.partial(_ffn_kernel, final=final),
        grid=(n // tm,),
        in_specs=[
            pl.BlockSpec((tm, D_MODEL), lambda i: (i, 0)),
            _resident((1, D_MODEL)),
            _resident((D_MODEL, 2 * FFN_DIM)),
            _resident((FFN_DIM, D_MODEL)),
            _resident((1, D_MODEL)),
        ],
        out_specs=pl.BlockSpec((tm, D_MODEL), lambda i: (i, 0)),
        out_shape=jax.ShapeDtypeStruct((n, D_MODEL), F32),
        compiler_params=_cparams(("parallel",)),
        name="ffn_half",
    )(x, gain, w_in, w_out, final_gain)


PROJ_TM = 512
PROJ_CHUNK = 512


def _proj_kernel(x_ref, g_ref, w_ref, cos_ref, s1_ref, s2_ref,
                 a_ref, sc_ref, b_ref, cqkv_ref, cz_ref, gl_ref):
    h = _rms(x_ref[...], g_ref[...]).astype(BF16)
    cos, s1, s2 = cos_ref[...], s1_ref[...], s2_ref[...]

    off = 0
    for j in range(W_A // LANES):
        t = _dot(h, w_ref[:, off:off + LANES])
        if j < ROT_TILES:
            t = t * cos + pltpu.roll(t, 8, 1) * s1 + pltpu.roll(t, LANES - 8, 1) * s2
        a_ref[:, off:off + LANES] = t.astype(BF16)
        off += LANES
    sc_ref[...] = _dot(h, w_ref[:, off:off + W_SC])
    off += W_SC
    for ref, width in ((b_ref, W_B), (cqkv_ref, W_CQKV), (cz_ref, W_CZ), (gl_ref, W_G)):
        for lo in range(0, width, PROJ_CHUNK):
            ref[:, lo:lo + PROJ_CHUNK] = _dot(
                h, w_ref[:, off + lo:off + lo + PROJ_CHUNK]).astype(ref.dtype)
        off += width


def _proj(x, gain, w, cos, s1, s2, seq):
    n = x.shape[0]
    tm = min(PROJ_TM, seq)
    per_seq = seq // tm
    row = lambda i: (i, 0)
    tab = lambda i: (i % per_seq, 0)
    widths = (W_A, W_SC, W_B, W_CQKV, W_CZ, W_G)
    dtypes = (BF16, F32, BF16, F32, F32, F32)
    return pl.pallas_call(
        _proj_kernel,
        grid=(n // tm,),
        in_specs=[
            pl.BlockSpec((tm, D_MODEL), row),
            _resident((1, D_MODEL)),
            _resident((D_MODEL, W_TOTAL)),
            pl.BlockSpec((tm, LANES), tab),
            pl.BlockSpec((tm, LANES), tab),
            pl.BlockSpec((tm, LANES), tab),
        ],
        out_specs=[pl.BlockSpec((tm, wd), row) for wd in widths],
        out_shape=[jax.ShapeDtypeStruct((n, wd), dt) for wd, dt in zip(widths, dtypes)],
        compiler_params=_cparams(("parallel",)),
        name="mixer_proj",
    )(x, gain, w, cos, s1, s2)


DSA_TQ = 512
DSA_CK = 512
INT_MIN = -2 ** 31
KEY_NEG_INF = int(np.array(-np.inf, np.float32).view(np.int32)) ^ 0x7FFFFFFF
SLAB = 64
SUM_ROW = HEAD_DIM
EXP_ROWS = 128


def _dsa_t_kernel(qa_ref, qi_ref, kk_ref, vv_ref, sc_ref, o_ref,
                  vt_ref, key_ref, bias_ref, qs_ref, m_ref, alpha_ref, acc_ref,
                  s_ref, p_ref, *, top_k, ck):
    i = pl.program_id(1)
    tq = qa_ref.shape[1]
    n_chunks_total = key_ref.shape[0]
    nck = (i * tq + tq + ck - 1) // ck
    lane = lax.broadcasted_iota(jnp.int32, (tq, LANES), 1)
    lo_half = lane < HEAD_DIM
    scale = HEAD_DIM ** -0.5

    @pl.when(i == 0)
    def _():
        ones_row = lax.broadcasted_iota(jnp.int32, (LANES, ck), 0) == SUM_ROW
        for c in range(n_chunks_total):
            vt = vv_ref[0, c * ck:(c + 1) * ck, :].astype(F32).T
            vt_ref[c] = jnp.where(ones_row, 1.0, vt).astype(BF16)

    def put(h, t):
        qs_ref[h * tq:(h + 1) * tq, :] = (t * scale).astype(BF16)

    for j in range(A_HEADS // 2):
        t = qa_ref[0, :, j * LANES:(j + 1) * LANES].astype(F32)
        put(2 * j, jnp.where(lo_half, t, 0.0))
        put(2 * j + 1, jnp.where(lo_half, pltpu.roll(t, HEAD_DIM, 1), 0.0))
    for j in range(IDX_HEADS // 2):
        t = qi_ref[0, :, j * LANES:(j + 1) * LANES].astype(F32)
        put(A_HEADS + 2 * j, jnp.where(lo_half, 0.0, pltpu.roll(t, HEAD_DIM, 1)))
        put(A_HEADS + 2 * j + 1, jnp.where(lo_half, 0.0, t))

    w_rows = (sc_ref[0] * (IDX_HEADS ** -0.5)).T
    key_pos = lax.broadcasted_iota(jnp.int32, (ck, tq), 0)
    qry_pos = i * tq + lax.broadcasted_iota(jnp.int32, (ck, tq), 1)

    def key_rows(c):
        return pl.ds(pl.multiple_of(c * ck, ck), ck)

    def score_chunk(c, _):
        kc = kk_ref[0, key_rows(c), :]
        score = None
        for h in range(IDX_HEADS):
            rel = jnp.maximum(
                _dot_nt(kc, qs_ref[(A_HEADS + h) * tq:(A_HEADS + h + 1) * tq, :]), 0.0)
            term = rel * w_rows[SC_IW + h:SC_IW + h + 1, :]
            score = term if score is None else score + term
        score = jnp.where(score == 0.0, 0.0, score)
        score = jnp.where(key_pos + c * ck <= qry_pos, score, -jnp.inf)
        bits = lax.bitcast_convert_type(score, jnp.int32)
        key_ref[c] = jnp.where(bits < 0, bits ^ 0x7FFFFFFF, bits)
        return 0

    lax.fori_loop(0, nck, score_chunk, 0)

    def count(pred):
        def body(c, acc):
            for r0 in range(0, ck, SLAB):
                acc = acc + jnp.where(pred(key_ref[c, r0:r0 + SLAB, :]), 1.0, 0.0)
            return acc
        acc = lax.fori_loop(0, nck, body, jnp.zeros((SLAB, tq), F32))
        return jnp.sum(acc, axis=0, keepdims=True)

    kf = float(top_k)

    def bit_step(b, thr):
        cand = thr + lax.shift_left(jnp.int32(1), 31 - b)
        cand_slab = jnp.broadcast_to(cand, (SLAB, tq))
        cnt = count(lambda kc: kc >= cand_slab)
        return jnp.where(cnt >= kf, cand, thr)

    thr = lax.fori_loop(0, 32, bit_step, jnp.full((1, tq), INT_MIN, jnp.int32))
    thr = jnp.maximum(thr, KEY_NEG_INF + 1)
    thr_slab = jnp.broadcast_to(thr, (SLAB, tq))
    cnt_ge = count(lambda kc: kc >= thr_slab)

    def write_bias(c, _):
        bias_ref[c] = jnp.where(key_ref[c] >= thr, 0.0, NEG_MASK)
        return 0

    lax.fori_loop(0, nck, write_bias, 0)

    @pl.when(jnp.max(cnt_ge) > kf)
    def _():
        need = kf - count(lambda kc: kc > thr_slab)
        r = lax.broadcasted_iota(jnp.int32, (ck, ck), 0)
        cidx = lax.broadcasted_iota(jnp.int32, (ck, ck), 1)
        lower = jnp.where(cidx <= r, 1.0, 0.0).astype(BF16)

        def tie_chunk(c, before):
            kc = key_ref[c]
            tie = kc == thr
            rank = before + _dot(lower, jnp.where(tie, 1.0, 0.0).astype(BF16))
            keep = (kc > thr) | (tie & (rank <= need))
            bias_ref[c] = jnp.where(keep, 0.0, NEG_MASK)
            return rank[ck - 1:ck, :]

        lax.fori_loop(0, nck, tie_chunk, jnp.zeros((1, tq), F32))

    m_ref[...] = jnp.full(m_ref.shape, NEG_BIG, F32)
    acc_ref[...] = jnp.zeros(acc_ref.shape, F32)
    reps = ck // 8

    n_att = A_HEADS * tq

    def attend(c, _):
        s_ref[...] = _dot_nt(kk_ref[0, key_rows(c), :], qs_ref[0:n_att, :])
        for tile in range(n_att // LANES):
            cols = slice(tile * LANES, (tile + 1) * LANES)
            qcols = slice((tile * LANES) % tq, (tile * LANES) % tq + LANES)
            part = None
            for r0 in range(0, ck, EXP_ROWS):
                rows = slice(r0, r0 + EXP_ROWS)
                s = s_ref[rows, cols] + bias_ref[c, rows, qcols]
                s_ref[rows, cols] = s
                unit = jnp.max(s.reshape(EXP_ROWS // 8, 8, LANES), axis=0)
                part = unit if part is None else jnp.maximum(part, unit)
            m_old = m_ref[:, cols]
            m_new = jnp.maximum(m_old, jnp.max(part, axis=0, keepdims=True))
            alpha_ref[:, cols] = jnp.exp(m_old - m_new)
            m_ref[:, cols] = m_new
        for tile in range(n_att // LANES):
            cols = slice(tile * LANES, (tile + 1) * LANES)
            for r0 in range(0, ck, EXP_ROWS):
                rows = slice(r0, r0 + EXP_ROWS)
                p = jnp.exp(s_ref[rows, cols] - jnp.tile(m_ref[:, cols], (EXP_ROWS // 8, 1)))
                p_ref[rows, cols] = p.astype(BF16)
        acc_ref[...] = (jnp.tile(alpha_ref[...], (LANES // 8, 1)) * acc_ref[...]
                        + _dot(vt_ref[c], p_ref[...]))
        return 0

    lax.fori_loop(0, nck, attend, 0)

    value_rows = lax.broadcasted_iota(jnp.int32, (LANES, tq), 0) < HEAD_DIM
    for j in range(A_HEADS // 2):
        outs = []
        for h in (2 * j, 2 * j + 1):
            cols = slice(h * tq, (h + 1) * tq)
            out_t = jnp.where(value_rows,
                              acc_ref[:, cols] / acc_ref[SUM_ROW:SUM_ROW + 1, cols], 0.0)
            outs.append(out_t.T)
        o_ref[0, :, j * LANES:(j + 1) * LANES] = (
            outs[0] + pltpu.roll(outs[1], HEAD_DIM, 1)).astype(o_ref.dtype)


def _dsa_t(a_grp, sc, top_k):
    b, s, _ = a_grp.shape
    tq = min(DSA_TQ, s)
    ck = min(DSA_CK, s)
    return pl.pallas_call(
        functools.partial(_dsa_t_kernel, top_k=top_k, ck=ck),
        grid=(b, s // tq),
        in_specs=[
            pl.BlockSpec((1, tq, 512), lambda bi, i: (bi, i, 0)),
            pl.BlockSpec((1, tq, 256), lambda bi, i: (bi, i, 2)),
            pl.BlockSpec((1, s, LANES), lambda bi, i: (bi, 0, 6)),
            pl.BlockSpec((1, s, LANES), lambda bi, i: (bi, 0, 7)),
            pl.BlockSpec((1, tq, LANES), lambda bi, i: (bi, i, 0)),
        ],
        out_specs=pl.BlockSpec((1, tq, BRANCH), lambda bi, i: (bi, i, 0)),
        out_shape=jax.ShapeDtypeStruct((b, s, BRANCH), BF16),
        scratch_shapes=[
            pltpu.VMEM((s // ck, LANES, ck), BF16),
            pltpu.VMEM((s // ck, ck, tq), jnp.int32),
            pltpu.VMEM((s // ck, ck, tq), F32),
            pltpu.VMEM(((A_HEADS + IDX_HEADS) * tq, LANES), BF16),
            pltpu.VMEM((8, A_HEADS * tq), F32),
            pltpu.VMEM((8, A_HEADS * tq), F32),
            pltpu.VMEM((LANES, A_HEADS * tq), F32),
            pltpu.VMEM((ck, A_HEADS * tq), F32),
            pltpu.VMEM((ck, A_HEADS * tq), BF16),
        ],
        compiler_params=_cparams(("parallel", "arbitrary")),
        name="dsa_attention",
    )(a_grp, a_grp, a_grp, a_grp, sc)


FOX_T = 256
BIAS_LANE = HEAD_DIM
BIAS_TERMS = 3


def _fox_t_kernel(q_ref, k_ref, v_ref, sc_ref, bf_ref, o_ref,
                  kaug_ref, vt_ref, qz_ref, m_ref, l_ref, alpha_ref, acc_ref, s_ref, p_ref, *, t):
    i = pl.program_id(1)
    seq = k_ref.shape[1]
    n_chunks = seq // t
    pairs = B_HEADS // 2
    lane = lax.broadcasted_iota(jnp.int32, (t, LANES), 1)
    lo_half = lane < HEAD_DIM

    def bias_lanes(h):
        first = BIAS_LANE + BIAS_TERMS * h
        return (lane >= first) & (lane < first + BIAS_TERMS)

    @pl.when(i == 0)
    def _():
        r = lax.broadcasted_iota(jnp.int32, (t, t), 0)
        c = lax.broadcasted_iota(jnp.int32, (t, t), 1)
        lower = jnp.where(c <= r, 1.0, 0.0)
        src = lax.broadcasted_iota(jnp.int32, (LANES, LANES), 0)
        dst = lax.broadcasted_iota(jnp.int32, (LANES, LANES), 1)

        def build(n, carry):
            rows = pl.ds(pl.multiple_of(n * t, t), t)
            x = sc_ref[0, rows, :] + bf_ref[...]
            log_f = jnp.minimum(x, 0.0) - jnp.log1p(jnp.exp(-jnp.abs(x)))
            cum = _dot_hi(lower, log_f) + carry
            nc = -cum
            hi = nc.astype(BF16)
            rest = nc - hi.astype(F32)
            mid = rest.astype(BF16)
            lo = (rest - mid.astype(F32)).astype(BF16)
            placed = None
            for j, term in enumerate((hi, mid, lo)):
                place = jnp.where((src >= SC_BF) & (src < SC_BF + B_HEADS)
                                  & (dst == BIAS_LANE + BIAS_TERMS * (src - SC_BF) + j), 1.0, 0.0)
                part = _dot(term, place.astype(BF16))
                placed = part if placed is None else placed + part
            for h in range(B_HEADS):
                hp = h // 2
                kt = k_ref[0, rows, hp * LANES:(hp + 1) * LANES].astype(F32)
                if h % 2:
                    kt = pltpu.roll(kt, HEAD_DIM, 1)
                tile = jnp.where(lo_half, kt, jnp.where(bias_lanes(h), placed, 0.0))
                kaug_ref[h, rows, :] = tile.astype(BF16)
            for hp in range(pairs):
                vt_ref[hp, n] = v_ref[0, rows, hp * LANES:(hp + 1) * LANES].astype(F32).T.astype(BF16)
            return cum[t - 1:t, :]

        lax.fori_loop(0, n_chunks, build, jnp.zeros((1, LANES), F32))

    for hp in range(pairs):
        q = q_ref[0, :, hp * LANES:(hp + 1) * LANES].astype(F32) * (HEAD_DIM ** -0.5)
        qz_ref[(2 * hp) * t:(2 * hp + 1) * t, :] = jnp.where(
            lo_half, q, jnp.where(bias_lanes(2 * hp), 1.0, 0.0)).astype(BF16)
        qz_ref[(2 * hp + 1) * t:(2 * hp + 2) * t, :] = jnp.where(
            lo_half, pltpu.roll(q, HEAD_DIM, 1), jnp.where(bias_lanes(2 * hp + 1), 1.0, 0.0)).astype(BF16)
    causal = (lax.broadcasted_iota(jnp.int32, (t, t), 0)
              <= lax.broadcasted_iota(jnp.int32, (t, t), 1))
    m_ref[...] = jnp.full(m_ref.shape, NEG_BIG, F32)
    l_ref[...] = jnp.zeros(l_ref.shape, F32)
    acc_ref[...] = jnp.zeros(acc_ref.shape, F32)

    def logits(c, slot):
        rows = pl.ds(pl.multiple_of(c * t, t), t)
        for h in range(B_HEADS):
            cols = slice(h * t, (h + 1) * t)
            s_ref[slot, :, cols] = _dot_nt(kaug_ref[h, rows, :], qz_ref[cols, :])

    def attend(c, slot, masked):
        if not masked:
            logits(c + 1, 1 - slot)
        for h in range(B_HEADS):
            cols = slice(h * t, (h + 1) * t)
            s = s_ref[slot, :, cols]
            if masked:
                s = jnp.where(causal, s, -jnp.inf)
            m_old = m_ref[:, cols]
            m_new = jnp.maximum(m_old, jnp.max(s, axis=0, keepdims=True))
            alpha = jnp.exp(m_old - m_new)
            p = jnp.exp(s - jnp.tile(m_new, (t // 8, 1)))
            l_ref[:, cols] = alpha * l_ref[:, cols] + jnp.sum(p, axis=0, keepdims=True)
            m_ref[:, cols] = m_new
            alpha_ref[:, cols] = alpha
            p_ref[:, cols] = p.astype(BF16)
        for h in range(B_HEADS):
            cols = slice(h * t, (h + 1) * t)
            acc_ref[:, cols] = (jnp.tile(alpha_ref[:, cols], (LANES // 8, 1)) * acc_ref[:, cols]
                                + _dot(vt_ref[h // 2, c], p_ref[:, cols]))

    def attend_pair(k, _):
        attend(2 * k, 0, masked=False)
        attend(2 * k + 1, 1, masked=False)
        return 0

    logits(0, 0)
    lax.fori_loop(0, i // 2, attend_pair, 0)

    @pl.when(i % 2 == 0)
    def _():
        attend(i, 0, masked=True)

    @pl.when(i % 2 == 1)
    def _():
        attend(i - 1, 0, masked=False)
        attend(i, 1, masked=True)

    for hp in range(pairs):
        outs = []
        for h in (2 * hp, 2 * hp + 1):
            cols = slice(h * t, (h + 1) * t)
            out_t = acc_ref[:, cols] / jnp.tile(l_ref[:, cols], (LANES // 8, 1))
            outs.append(out_t.T)
        o_ref[0, :, hp * LANES:(hp + 1) * LANES] = jnp.where(
            lo_half, outs[0], outs[1]).astype(o_ref.dtype)


def _fox_t(b_grp, sc, bf_row):
    b, s, _ = b_grp.shape
    t = min(FOX_T, s)
    pairs = B_HEADS // 2
    return pl.pallas_call(
        functools.partial(_fox_t_kernel, t=t),
        grid=(b, s // t),
        in_specs=[
            pl.BlockSpec((1, t, BRANCH), lambda bi, i: (bi, i, 0)),
            pl.BlockSpec((1, s, BRANCH), lambda bi, i: (bi, 0, 1)),
            pl.BlockSpec((1, s, BRANCH), lambda bi, i: (bi, 0, 2)),
            pl.BlockSpec((1, s, LANES), lambda bi, i: (bi, 0, 0)),
            _resident((1, LANES)),
        ],
        out_specs=pl.BlockSpec((1, t, BRANCH), lambda bi, i: (bi, i, 0)),
        out_shape=jax.ShapeDtypeStruct((b, s, BRANCH), BF16),
        scratch_shapes=[
            pltpu.VMEM((B_HEADS, s, LANES), BF16),
            pltpu.VMEM((pairs, s // t, LANES, t), BF16),
            pltpu.VMEM((B_HEADS * t, LANES), BF16),
            pltpu.VMEM((8, B_HEADS * t), F32),
            pltpu.VMEM((8, B_HEADS * t), F32),
            pltpu.VMEM((8, B_HEADS * t), F32),
            pltpu.VMEM((LANES, B_HEADS * t), F32),
            pltpu.VMEM((2, t, B_HEADS * t), F32),
            pltpu.VMEM((t, B_HEADS * t), BF16),
        ],
        compiler_params=_cparams(("parallel", "arbitrary")),
        name="fox_attention",
    )(b_grp, b_grp, b_grp, sc, bf_row)


GDN_TS = 512
GDN_C = 128
TAIL = 8
NEUMANN_ROUNDS = 6


def _gdn_kernel(x_ref, z_ref, sc_ref, cw_ref, alog_ref, dtb_ref, dn_ref, o_ref,
                state_ref, tail_ref, ext_ref, act_ref, np_ref, rhs_ref,
                qk_ref, qg_ref, kdt_ref, dec_ref, *, ts):
    j = pl.program_id(1)
    width = x_ref.shape[2]
    n_chunks = ts // GDN_C
    chains = [(h, n) for n in range(n_chunks) for h in range(C_HEADS)]

    @pl.when(j == 0)
    def _():
        state_ref[...] = jnp.zeros_like(state_ref)
        tail_ref[...] = jnp.zeros_like(tail_ref)

    x = x_ref[0]
    ext_ref[0:TAIL, :] = tail_ref[...]
    ext_ref[TAIL:TAIL + ts, :] = x
    tail_ref[...] = x[ts - TAIL:ts, :]
    conv = jnp.zeros((ts, width), F32)
    for tap in range(CONV_WIDTH):
        start = TAIL - (CONV_WIDTH - 1) + tap
        conv = conv + cw_ref[tap:tap + 1, :] * ext_ref[start:start + ts, :]
    act_ref[...] = _silu(conv)

    sc = sc_ref[0]
    beta_all = jax.nn.sigmoid(sc)
    xg = sc + dtb_ref[...]
    softplus = jnp.maximum(xg, 0.0) + jnp.log1p(jnp.exp(-jnp.abs(xg)))
    g_all = -jnp.exp(alog_ref[...]) * softplus
    r = lax.broadcasted_iota(jnp.int32, (ts, ts), 0)
    c = lax.broadcasted_iota(jnp.int32, (ts, ts), 1)
    same_chunk_lower = jnp.where((c <= r) & (r // GDN_C == c // GDN_C), 1.0, 0.0)
    gc_all = _dot_hi(same_chunk_lower, g_all)
    gc_t = gc_all.T

    ri = lax.broadcasted_iota(jnp.int32, (GDN_C, GDN_C), 0)
    ci = lax.broadcasted_iota(jnp.int32, (GDN_C, GDN_C), 1)
    incl = ri >= ci
    strict = ri > ci

    def l2n(t):
        return t * lax.rsqrt(jnp.sum(t * t, axis=-1, keepdims=True) + L2_EPS)

    for idx, (h, n) in enumerate(chains):
        rows = slice(n * GDN_C, (n + 1) * GDN_C)
        q = l2n(act_ref[rows, h * C_DIM:(h + 1) * C_DIM]) * (C_DIM ** -0.5)
        k = l2n(act_ref[rows, BRANCH + h * C_DIM:BRANCH + (h + 1) * C_DIM])
        v = act_ref[rows, 2 * BRANCH + h * C_DIM:2 * BRANCH + (h + 1) * C_DIM]
        beta = beta_all[rows, SC_BETA + h:SC_BETA + h + 1]
        gcol = gc_all[rows, SC_DECAY + h:SC_DECAY + h + 1]
        grow = gc_t[SC_DECAY + h:SC_DECAY + h + 1, n * GDN_C:(n + 1) * GDN_C]
        g_last = gcol[GDN_C - 1:GDN_C, :]
        decay = jnp.exp(jnp.where(incl, gcol - grow, -jnp.inf))
        kb = k * beta
        k16 = k.astype(BF16)
        q16 = q.astype(BF16)
        a_mat = jnp.where(strict, _dot_nt(kb.astype(BF16), k16) * decay, 0.0)
        np_ref[idx, :, 0:GDN_C] = -a_mat
        np_ref[idx, :, GDN_C:2 * GDN_C] = _dot_solve(a_mat, a_mat)
        rhs_ref[idx, :, 0:C_DIM] = v * beta
        rhs_ref[idx, :, C_DIM:2 * C_DIM] = kb * jnp.exp(gcol)
        qk_ref[idx] = jnp.where(incl, _dot_nt(q16, k16) * decay, 0.0).astype(BF16)
        qg_ref[idx] = (q * jnp.exp(gcol)).astype(BF16)
        kdt_ref[idx] = (k * jnp.exp(g_last - gcol)).T.astype(BF16)
        dec_ref[idx] = jnp.broadcast_to(jnp.exp(g_last), (1, C_DIM))

    for rnd in range(1, NEUMANN_ROUNDS + 1):
        last = rnd == NEUMANN_ROUNDS
        for idx in range(len(chains)):
            n_old = np_ref[idx, :, 0:GDN_C]
            p_old = np_ref[idx, :, GDN_C:2 * GDN_C]
            if last:
                np_ref[idx, :, 0:GDN_C] = n_old + p_old + _dot_solve(p_old, n_old)
            else:
                prod = _dot_solve(p_old, np_ref[idx])
                np_ref[idx, :, 0:GDN_C] = n_old + p_old + prod[:, 0:GDN_C]
                np_ref[idx, :, GDN_C:2 * GDN_C] = prod[:, GDN_C:2 * GDN_C]

    for idx in range(len(chains)):
        rhs = rhs_ref[idx]
        rhs_ref[idx] = rhs + _dot_solve(np_ref[idx, :, 0:GDN_C], rhs)

    for idx, (h, n) in enumerate(chains):
        rows = slice(n * GDN_C, (n + 1) * GDN_C)
        state = state_ref[h]
        s16 = state.astype(BF16)
        v_new = rhs_ref[idx, :, 0:C_DIM] - _dot(rhs_ref[idx, :, C_DIM:2 * C_DIM].astype(BF16), s16)
        vn16 = v_new.astype(BF16)
        o = _dot(qg_ref[idx], s16) + _dot(qk_ref[idx], vn16)
        state_ref[h] = state * dec_ref[idx] + _dot(kdt_ref[idx], vn16)
        z = z_ref[0, rows, h * C_DIM:(h + 1) * C_DIM]
        y = _rms(o, dn_ref[...]) * _silu(z)
        o_ref[0, rows, h * C_DIM:(h + 1) * C_DIM] = y.astype(o_ref.dtype)


def _gdn(cqkv, cz, sc, conv_w, alog_row, dtb_row, dn_row):
    b, s, width = cqkv.shape
    ts = min(GDN_TS, s)
    nch = C_HEADS * (ts // GDN_C)
    blk = lambda bi, j: (bi, j, 0)
    return pl.pallas_call(
        functools.partial(_gdn_kernel, ts=ts),
        grid=(b, s // ts),
        in_specs=[
            pl.BlockSpec((1, ts, width), blk),
            pl.BlockSpec((1, ts, BRANCH), blk),
            pl.BlockSpec((1, ts, LANES), blk),
            _resident((CONV_WIDTH, width)),
            _resident((1, LANES)),
            _resident((1, LANES)),
            _resident((1, C_DIM)),
        ],
        out_specs=pl.BlockSpec((1, ts, BRANCH), blk),
        out_shape=jax.ShapeDtypeStruct((b, s, BRANCH), BF16),
        scratch_shapes=[
            pltpu.VMEM((C_HEADS, C_DIM, C_DIM), F32),
            pltpu.VMEM((TAIL, width), F32),
            pltpu.VMEM((TAIL + ts, width), F32),
            pltpu.VMEM((ts, width), F32),
            pltpu.VMEM((nch, GDN_C, 2 * GDN_C), F32),
            pltpu.VMEM((nch, GDN_C, 2 * C_DIM), F32),
            pltpu.VMEM((nch, GDN_C, GDN_C), BF16),
            pltpu.VMEM((nch, GDN_C, C_DIM), BF16),
            pltpu.VMEM((nch, C_DIM, GDN_C), BF16),
            pltpu.VMEM((nch, 1, C_DIM), F32),
        ],
        compiler_params=_cparams(("parallel", "arbitrary")),
        name="gated_deltanet",
    )(cqkv, cz, sc, conv_w, alog_row, dtb_row, dn_row)


MERGE_TM = 512


def _merge_kernel(x_ref, ya_ref, yb_ref, yc_ref, gl_ref, bg_ref,
                  wa_ref, wb_ref, wc_ref, wo_ref, o_ref):
    merged = None
    for n, (y_ref, w_ref) in enumerate(((ya_ref, wa_ref), (yb_ref, wb_ref), (yc_ref, wc_ref))):
        cols = slice(n * D_MODEL, (n + 1) * D_MODEL)
        gate = jax.nn.sigmoid(gl_ref[:, cols] + bg_ref[:, cols])
        term = gate * _dot(y_ref[...], w_ref[...])
        merged = term if merged is None else merged + term
    o_ref[...] = x_ref[...] + _dot(merged.astype(BF16), wo_ref[...])


def _merge(x, ya, yb, yc, gl, b_gate, wa, wb, wc, wo):
    n = x.shape[0]
    tm = min(MERGE_TM, n)
    row = lambda i: (i, 0)
    return pl.pallas_call(
        _merge_kernel,
        grid=(n // tm,),
        in_specs=[
            pl.BlockSpec((tm, D_MODEL), row),
            pl.BlockSpec((tm, BRANCH), row),
            pl.BlockSpec((tm, BRANCH), row),
            pl.BlockSpec((tm, BRANCH), row),
            pl.BlockSpec((tm, 3 * D_MODEL), row),
            _resident((1, 3 * D_MODEL)),
            _resident((BRANCH, D_MODEL)),
            _resident((BRANCH, D_MODEL)),
            _resident((BRANCH, D_MODEL)),
            _resident((D_MODEL, D_MODEL)),
        ],
        out_specs=pl.BlockSpec((tm, D_MODEL), row),
        out_shape=jax.ShapeDtypeStruct((n, D_MODEL), F32),
        compiler_params=_cparams(("parallel",)),
        name="gated_merge",
    )(x, ya, yb, yc, gl, b_gate, wa, wb, wc, wo)


def _w_in_pieces():
    offs = np.concatenate([[0], np.cumsum(IN_SIZES)])
    names = ("a_q", "a_k", "a_v", "i_q", "i_k", "i_w", "b_qkv", "b_f", "c_qkv", "c_z",
             "c_beta", "c_a", "gates")
    src = {n: (int(offs[i]), int(IN_SIZES[i])) for i, n in enumerate(names)}
    order = (("a_q", 0), ("i_q", 512), ("a_k", 768), ("i_k", 832), ("a_v", 896),
             ("b_f", W_A + SC_BF), ("i_w", W_A + SC_IW), ("c_beta", W_A + SC_BETA),
             ("c_a", W_A + SC_DECAY), ("b_qkv", W_A + W_SC), ("c_qkv", W_A + W_SC + W_B),
             ("c_z", W_A + W_SC + W_B + W_CQKV), ("gates", W_A + W_SC + W_B + W_CQKV + W_CZ))
    pieces = []
    for name, dst in order:
        s0, width = src[name]
        done = 0
        while done < width:
            step = min(width - done, LANES - (dst + done) % LANES)
            pieces.append((dst + done, s0 + done, step))
            done += step
    return pieces


RELAYOUT_TM = 256


def _relayout_kernel(w_ref, tail_ref, o_ref, *, aligned_cols):
    by_tile = {}
    for dst, src, n in _w_in_pieces():
        by_tile.setdefault(dst // LANES, []).append((dst % LANES, src, n))
    row = lax.broadcasted_iota(jnp.int32, (2 * LANES, LANES), 0)
    col = lax.broadcasted_iota(jnp.int32, (2 * LANES, LANES), 1)

    def src_tile(start):
        if start + LANES <= aligned_cols:
            return w_ref[0, :, start:start + LANES]
        return tail_ref[0, :, start - aligned_cols:start - aligned_cols + LANES]

    for tile in range(W_TOTAL // LANES):
        acc = None
        for dst_lane, src, n in by_tile.get(tile, ()):
            base = (src // LANES) * LANES
            shift = src - base - dst_lane
            select = (row == col + shift) & (col >= dst_lane) & (col < dst_lane + n)
            window = jnp.concatenate([src_tile(base), src_tile(base + LANES)], axis=1)
            part = _dot(window.astype(BF16), jnp.where(select, 1.0, 0.0).astype(BF16))
            acc = part if acc is None else acc + part
        if acc is None:
            acc = jnp.zeros((w_ref.shape[1], LANES), F32)
        o_ref[0, :, tile * LANES:(tile + 1) * LANES] = acc.astype(BF16)


def _layout_w_in(w_in):
    depth, rows, cols = w_in.shape
    aligned_cols = (cols // LANES) * LANES
    tail = jnp.pad(w_in[:, :, aligned_cols:], ((0, 0), (0, 0), (0, 2 * LANES - (cols - aligned_cols))))
    tm = min(RELAYOUT_TM, rows)
    return pl.pallas_call(
        functools.partial(_relayout_kernel, aligned_cols=aligned_cols),
        grid=(depth, rows // tm),
        in_specs=[pl.BlockSpec((1, tm, cols), lambda l, i: (l, i, 0)),
                  pl.BlockSpec((1, tm, 2 * LANES), lambda l, i: (l, i, 0))],
        out_specs=pl.BlockSpec((1, tm, W_TOTAL), lambda l, i: (l, i, 0)),
        out_shape=jax.ShapeDtypeStruct((depth, rows, W_TOTAL), BF16),
        compiler_params=_cparams(("parallel", "parallel")),
        name="w_in_relayout",
    )(w_in, tail)


def _rotary_tables(seq):
    pos = jnp.arange(seq, dtype=F32)
    inv_freq = jnp.power(ROPE_THETA, -jnp.arange(0, ROT_DIM, 2, dtype=F32) / ROT_DIM)
    ang = pos[:, None] * inv_freq[None, :]
    cos, sin = jnp.cos(ang), jnp.sin(ang)
    half = ROT_DIM // 2
    ones = jnp.ones((seq, HEAD_DIM - ROT_DIM), F32)
    zeros_h = jnp.zeros((seq, half), F32)
    zeros_r = jnp.zeros((seq, HEAD_DIM - ROT_DIM), F32)
    c64 = jnp.concatenate([cos, cos, ones], axis=1)
    s1_64 = jnp.concatenate([zeros_h, sin, zeros_r], axis=1)
    s2_64 = jnp.concatenate([-sin, zeros_h, zeros_r], axis=1)
    twice = lambda t: jnp.concatenate([t, t], axis=1)
    return twice(c64), twice(s1_64), twice(s2_64)


def _lane_row(values, start):
    return jnp.zeros((1, LANES), F32).at[0, start:start + values.shape[0]].set(values.astype(F32))


def kernel(x, ffn1_norm, ffn1_w_in, ffn1_w_out, mix_norm, w_in, b_gate, b_forget, conv_w, a_log, dt_bias, delta_norm, w_branch_a, w_branch_b, w_branch_c, w_out, ffn2_norm, ffn2_w_in, ffn2_w_out, final_norm):
    batch, seq, _ = x.shape
    depth = w_in.shape[0]
    top_k = min(INDEX_TOPK, seq // 4)
    cos, s1, s2 = _rotary_tables(seq)
    w_mix = _layout_w_in(w_in)
    final_row = final_norm.reshape(1, D_MODEL)
    xt = x.reshape(batch * seq, D_MODEL)
    for l in range(depth):
        xt = _ffn(xt, ffn1_norm[l].reshape(1, D_MODEL), ffn1_w_in[l].astype(BF16),
                  ffn1_w_out[l].astype(BF16), final_row, False)
        a_grp, sc, b_grp, cqkv, cz, gl = _proj(
            xt, mix_norm[l].reshape(1, D_MODEL), w_mix[l], cos, s1, s2, seq)
        a_grp = a_grp.reshape(batch, seq, W_A)
        sc = sc.reshape(batch, seq, W_SC)
        y_a = _dsa_t(a_grp, sc, top_k)
        y_b = _fox_t(b_grp.reshape(batch, seq, W_B), sc, _lane_row(b_forget[l], SC_BF))
        y_c = _gdn(cqkv.reshape(batch, seq, W_CQKV), cz.reshape(batch, seq, W_CZ), sc,
                   conv_w[l], _lane_row(a_log[l], SC_DECAY), _lane_row(dt_bias[l], SC_DECAY),
                   delta_norm[l].reshape(1, C_DIM))
        flat = lambda t: t.reshape(batch * seq, BRANCH)
        xt = _merge(xt, flat(y_a), flat(y_b), flat(y_c), gl, b_gate[l].reshape(1, 3 * D_MODEL),
                    w_branch_a[l].astype(BF16), w_branch_b[l].astype(BF16),
                    w_branch_c[l].astype(BF16), w_out[l].astype(BF16))
        xt = _ffn(xt, ffn2_norm[l].reshape(1, D_MODEL), ffn2_w_in[l].astype(BF16),
                  ffn2_w_out[l].astype(BF16), final_row, l == depth - 1)
    return xt.reshape(batch, seq, D_MODEL)
```

```python
import functools

import numpy as np
import jax
import jax.numpy as jnp
from jax import lax
from jax.experimental import pallas as pl
from jax.experimental.pallas import tpu as pltpu

F32 = jnp.float32
BF16 = jnp.bfloat16

D_MODEL = 1024
BRANCH = 512
A_HEADS = 8
HEAD_DIM = 64
IDX_HEADS = 4
INDEX_TOPK = 256
B_HEADS = 8
C_HEADS = 4
C_DIM = 128
CONV_WIDTH = 4
ROPE_THETA = 500000.0
ROT_DIM = 16
FFN_DIM = 2048
NORM_EPS = 1e-6
L2_EPS = 1e-6
IN_SIZES = (512, 64, 64, 256, 64, 4, 1536, 8, 1536, 512, 4, 4, 3072)

LANES = 128
SC_BF = 0
SC_IW = 8
SC_BETA = 12
SC_DECAY = 16
W_A, W_SC, W_B, W_CQKV, W_CZ, W_G = 1024, 128, 1536, 1536, 512, 3072
W_TOTAL = W_A + W_SC + W_B + W_CQKV + W_CZ + W_G
A_Q_W = A_HEADS * HEAD_DIM
I_Q_W = IDX_HEADS * HEAD_DIM
I_Q_BLOCK = A_Q_W // I_Q_W
KEY_TILE = (A_Q_W + I_Q_W) // LANES
VALUE_TILE = KEY_TILE + 1
ROT_TILES = KEY_TILE + 1

NEG_BIG = -1e30
NEG_MASK = -2e30
VMEM_LIMIT = 56 * 1024 * 1024


def _cparams(sem):
    return pltpu.CompilerParams(dimension_semantics=sem, vmem_limit_bytes=VMEM_LIMIT)


def _dot(a, b):
    return jnp.dot(a, b, preferred_element_type=F32)


def _dot_hi(a, b):
    return jnp.dot(a, b, preferred_element_type=F32, precision=lax.Precision.HIGHEST)


def _dot_solve(a, b):
    a_hi = a.astype(BF16)
    b_hi = b.astype(BF16)
    a_lo = (a - a_hi.astype(F32)).astype(BF16)
    b_lo = (b - b_hi.astype(F32)).astype(BF16)
    m = a.shape[0]
    top = _dot(jnp.concatenate([a_hi, a_lo], axis=0), b_hi)
    return top[0:m] + top[m:2 * m] + _dot(a_hi, b_lo)


def _dot_nt(a, b):
    return lax.dot_general(a, b, (((1,), (1,)), ((), ())), preferred_element_type=F32)


def _rms(x, gain):
    return x * lax.rsqrt(jnp.mean(x * x, axis=-1, keepdims=True) + NORM_EPS) * gain


def _silu(x):
    return x * jax.nn.sigmoid(x)


def _resident(shape):
    nd = len(shape)
    return pl.BlockSpec(shape, lambda *_: (0,) * nd, pipeline_mode=pl.Buffered(1))


FFN_TM = 1024
FFN_CHUNK = 512


def _ffn_kernel(x_ref, g_ref, win_ref, wout_ref, fg_ref, o_ref, *, final):
    x = x_ref[...]
    h = _rms(x, g_ref[...]).astype(BF16)
    acc = jnp.zeros(x.shape, F32)
    for c in range(FFN_DIM // FFN_CHUNK):
        lo = c * FFN_CHUNK
        gate = _dot(h, win_ref[:, lo:lo + FFN_CHUNK])
        up = _dot(h, win_ref[:, FFN_DIM + lo:FFN_DIM + lo + FFN_CHUNK])
        act = (_silu(gate) * up).astype(BF16)
        acc = acc + _dot(act, wout_ref[lo:lo + FFN_CHUNK, :])
    y = x + 0.5 * acc
    if final:
        y = _rms(y, fg_ref[...])
    o_ref[...] = y


def _ffn(x, gain, w_in, w_out, final_gain, final):
    n = x.shape[0]
    tm = min(FFN_TM, n)
    return pl.pallas_call(
        functools.partial(_ffn_kernel, final=final),
        grid=(n // tm,),
        in_specs=[
            pl.BlockSpec((tm, D_MODEL), lambda i: (i, 0)),
            _resident((1, D_MODEL)),
            _resident((D_MODEL, 2 * FFN_DIM)),
            _resident((FFN_DIM, D_MODEL)),
            _resident((1, D_MODEL)),
        ],
        out_specs=pl.BlockSpec((tm, D_MODEL), lambda i: (i, 0)),
        out_shape=jax.ShapeDtypeStruct((n, D_MODEL), F32),
        compiler_params=_cparams(("parallel",)),
        name="ffn_half",
    )(x, gain, w_in, w_out, final_gain)


PROJ_TM = 512
PROJ_CHUNK = 512


def _proj_kernel(x_ref, g_ref, w_ref, cos_ref, s1_ref, s2_ref,
                 a_ref, sc_ref, b_ref, cqkv_ref, cz_ref, gl_ref):
    h = _rms(x_ref[...], g_ref[...]).astype(BF16)
    cos, s1, s2 = cos_ref[...], s1_ref[...], s2_ref[...]

    off = 0
    for j in range(W_A // LANES):
        t = _dot(h, w_ref[:, off:off + LANES])
        if j < ROT_TILES:
            t = t * cos + pltpu.roll(t, 8, 1) * s1 + pltpu.roll(t, LANES - 8, 1) * s2
        a_ref[:, off:off + LANES] = t.astype(BF16)
        off += LANES
    sc_ref[...] = _dot(h, w_ref[:, off:off + W_SC])
    off += W_SC
    for ref, width in ((b_ref, W_B), (cqkv_ref, W_CQKV), (cz_ref, W_CZ), (gl_ref, W_G)):
        for lo in range(0, width, PROJ_CHUNK):
            ref[:, lo:lo + PROJ_CHUNK] = _dot(
                h, w_ref[:, off + lo:off + lo + PROJ_CHUNK]).astype(ref.dtype)
        off += width


def _proj(x, gain, w, cos, s1, s2, seq):
    n = x.shape[0]
    tm = min(PROJ_TM, seq)
    per_seq = seq // tm
    row = lambda i: (i, 0)
    tab = lambda i: (i % per_seq, 0)
    widths = (W_A, W_SC, W_B, W_CQKV, W_CZ, W_G)
    dtypes = (BF16, F32, BF16, F32, F32, F32)
    return pl.pallas_call(
        _proj_kernel,
        grid=(n // tm,),
        in_specs=[
            pl.BlockSpec((tm, D_MODEL), row),
            _resident((1, D_MODEL)),
            _resident((D_MODEL, W_TOTAL)),
            pl.BlockSpec((tm, LANES), tab),
            pl.BlockSpec((tm, LANES), tab),
            pl.BlockSpec((tm, LANES), tab),
        ],
        out_specs=[pl.BlockSpec((tm, wd), row) for wd in widths],
        out_shape=[jax.ShapeDtypeStruct((n, wd), dt) for wd, dt in zip(widths, dtypes)],
        compiler_params=_cparams(("parallel",)),
        name="mixer_proj",
    )(x, gain, w, cos, s1, s2)


DSA_TQ = 512
DSA_CK = 512
INT_MIN = -2 ** 31
KEY_NEG_INF = int(np.array(-np.inf, np.float32).view(np.int32)) ^ 0x7FFFFFFF
SLAB = 64
SUM_ROW = HEAD_DIM
EXP_ROWS = 128


def _dsa_t_kernel(qa_ref, qi_ref, kk_ref, vv_ref, sc_ref, o_ref,
                  vt_ref, key_ref, bias_ref, qs_ref, m_ref, alpha_ref, acc_ref,
                  s_ref, p_ref, *, top_k, ck):
    i = pl.program_id(1)
    tq = qa_ref.shape[1]
    n_chunks_total = key_ref.shape[0]
    nck = (i * tq + tq + ck - 1) // ck
    lane = lax.broadcasted_iota(jnp.int32, (tq, LANES), 1)
    lo_half = lane < HEAD_DIM
    scale = HEAD_DIM ** -0.5

    @pl.when(i == 0)
    def _():
        ones_row = lax.broadcasted_iota(jnp.int32, (LANES, ck), 0) == SUM_ROW
        for c in range(n_chunks_total):
            vt = vv_ref[0, c * ck:(c + 1) * ck, :].astype(F32).T
            vt_ref[c] = jnp.where(ones_row, 1.0, vt).astype(BF16)

    def put(h, t):
        qs_ref[h * tq:(h + 1) * tq, :] = (t * scale).astype(BF16)

    for j in range(A_HEADS // 2):
        t = qa_ref[0, :, j * LANES:(j + 1) * LANES].astype(F32)
        put(2 * j, jnp.where(lo_half, t, 0.0))
        put(2 * j + 1, jnp.where(lo_half, pltpu.roll(t, HEAD_DIM, 1), 0.0))
    for j in range(IDX_HEADS // 2):
        t = qi_ref[0, :, j * LANES:(j + 1) * LANES].astype(F32)
        put(A_HEADS + 2 * j, jnp.where(lo_half, 0.0, pltpu.roll(t, HEAD_DIM, 1)))
        put(A_HEADS + 2 * j + 1, jnp.where(lo_half, 0.0, t))

    w_rows = (sc_ref[0] * (IDX_HEADS ** -0.5)).T
    key_pos = lax.broadcasted_iota(jnp.int32, (ck, tq), 0)
    qry_pos = i * tq + lax.broadcasted_iota(jnp.int32, (ck, tq), 1)

    def key_rows(c):
        return pl.ds(pl.multiple_of(c * ck, ck), ck)

    def score_chunk(c, _):
        kc = kk_ref[0, key_rows(c), :]
        score = None
        for h in range(IDX_HEADS):
            rel = jnp.maximum(
                _dot_nt(kc, qs_ref[(A_HEADS + h) * tq:(A_HEADS + h + 1) * tq, :]), 0.0)
            term = rel * w_rows[SC_IW + h:SC_IW + h + 1, :]
            score = term if score is None else score + term
        score = jnp.where(score == 0.0, 0.0, score)
        score = jnp.where(key_pos + c * ck <= qry_pos, score, -jnp.inf)
        bits = lax.bitcast_convert_type(score, jnp.int32)
        key_ref[c] = jnp.where(bits < 0, bits ^ 0x7FFFFFFF, bits)
        return 0

    lax.fori_loop(0, nck, score_chunk, 0)

    def count(pred):
        def body(c, acc):
            for r0 in range(0, ck, SLAB):
                acc = acc + jnp.where(pred(key_ref[c, r0:r0 + SLAB, :]), 1.0, 0.0)
            return acc
        acc = lax.fori_loop(0, nck, body, jnp.zeros((SLAB, tq), F32))
        return jnp.sum(acc, axis=0, keepdims=True)

    kf = float(top_k)

    def bit_step(b, thr):
        cand = thr + lax.shift_left(jnp.int32(1), 31 - b)
        cand_slab = jnp.broadcast_to(cand, (SLAB, tq))
        cnt = count(lambda kc: kc >= cand_slab)
        return jnp.where(cnt >= kf, cand, thr)

    thr = lax.fori_loop(0, 32, bit_step, jnp.full((1, tq), INT_MIN, jnp.int32))
    thr = jnp.maximum(thr, KEY_NEG_INF + 1)
    thr_slab = jnp.broadcast_to(thr, (SLAB, tq))
    cnt_ge = count(lambda kc: kc >= thr_slab)

    def write_bias(c, _):
        bias_ref[c] = jnp.where(key_ref[c] >= thr, 0.0, NEG_MASK)
        return 0

    lax.fori_loop(0, nck, write_bias, 0)

    @pl.when(jnp.max(cnt_ge) > kf)
    def _():
        need = kf - count(lambda kc: kc > thr_slab)
        r = lax.broadcasted_iota(jnp.int32, (ck, ck), 0)
        cidx = lax.broadcasted_iota(jnp.int32, (ck, ck), 1)
        lower = jnp.where(cidx <= r, 1.0, 0.0).astype(BF16)

        def tie_chunk(c, before):
            kc = key_ref[c]
            tie = kc == thr
            rank = before + _dot(lower, jnp.where(tie, 1.0, 0.0).astype(BF16))
            keep = (kc > thr) | (tie & (rank <= need))
            bias_ref[c] = jnp.where(keep, 0.0, NEG_MASK)
            return rank[ck - 1:ck, :]

        lax.fori_loop(0, nck, tie_chunk, jnp.zeros((1, tq), F32))

    m_ref[...] = jnp.full(m_ref.shape, NEG_BIG, F32)
    acc_ref[...] = jnp.zeros(acc_ref.shape, F32)
    reps = ck // 8

    n_att = A_HEADS * tq

    def attend(c, _):
        s_ref[...] = _dot_nt(kk_ref[0, key_rows(c), :], qs_ref[0:n_att, :])
        for tile in range(n_att // LANES):
            cols = slice(tile * LANES, (tile + 1) * LANES)
            qcols = slice((tile * LANES) % tq, (tile * LANES) % tq + LANES)
            part = None
            for r0 in range(0, ck, EXP_ROWS):
                rows = slice(r0, r0 + EXP_ROWS)
                s = s_ref[rows, cols] + bias_ref[c, rows, qcols]
                s_ref[rows, cols] = s
                unit = jnp.max(s.reshape(EXP_ROWS // 8, 8, LANES), axis=0)
                part = unit if part is None else jnp.maximum(part, unit)
            m_old = m_ref[:, cols]
            m_new = jnp.maximum(m_old, jnp.max(part, axis=0, keepdims=True))
            alpha_ref[:, cols] = jnp.exp(m_old - m_new)
            m_ref[:, cols] = m_new
        for tile in range(n_att // LANES):
            cols = slice(tile * LANES, (tile + 1) * LANES)
            for r0 in range(0, ck, EXP_ROWS):
                rows = slice(r0, r0 + EXP_ROWS)
                p = jnp.exp(s_ref[rows, cols] - jnp.tile(m_ref[:, cols], (EXP_ROWS // 8, 1)))
                p_ref[rows, cols] = p.astype(BF16)
        acc_ref[...] = (jnp.tile(alpha_ref[...], (LANES // 8, 1)) * acc_ref[...]
                        + _dot(vt_ref[c], p_ref[...]))
        return 0

    lax.fori_loop(0, nck, attend, 0)

    value_rows = lax.broadcasted_iota(jnp.int32, (LANES, tq), 0) < HEAD_DIM
    for j in range(A_HEADS // 2):
        outs = []
        for h in (2 * j, 2 * j + 1):
            cols = slice(h * tq, (h + 1) * tq)
            out_t = jnp.where(value_rows,
                              acc_ref[:, cols] / acc_ref[SUM_ROW:SUM_ROW + 1, cols], 0.0)
            outs.append(out_t.T)
        o_ref[0, :, j * LANES:(j + 1) * LANES] = (
            outs[0] + pltpu.roll(outs[1], HEAD_DIM, 1)).astype(o_ref.dtype)


def _dsa_t(a_grp, sc, top_k):
    b, s, _ = a_grp.shape
    tq = min(DSA_TQ, s)
    ck = min(DSA_CK, s)
    return pl.pallas_call(
        functools.partial(_dsa_t_kernel, top_k=top_k, ck=ck),
        grid=(b, s // tq),
        in_specs=[
            pl.BlockSpec((1, tq, A_Q_W), lambda bi, i: (bi, i, 0)),
            pl.BlockSpec((1, tq, I_Q_W), lambda bi, i: (bi, i, I_Q_BLOCK)),
            pl.BlockSpec((1, s, LANES), lambda bi, i: (bi, 0, KEY_TILE)),
            pl.BlockSpec((1, s, LANES), lambda bi, i: (bi, 0, VALUE_TILE)),
            pl.BlockSpec((1, tq, LANES), lambda bi, i: (bi, i, 0)),
        ],
        out_specs=pl.BlockSpec((1, tq, BRANCH), lambda bi, i: (bi, i, 0)),
        out_shape=jax.ShapeDtypeStruct((b, s, BRANCH), BF16),
        scratch_shapes=[
            pltpu.VMEM((s // ck, LANES, ck), BF16),
            pltpu.VMEM((s // ck, ck, tq), jnp.int32),
            pltpu.VMEM((s // ck, ck, tq), F32),
            pltpu.VMEM(((A_HEADS + IDX_HEADS) * tq, LANES), BF16),
            pltpu.VMEM((8, A_HEADS * tq), F32),
            pltpu.VMEM((8, A_HEADS * tq), F32),
            pltpu.VMEM((LANES, A_HEADS * tq), F32),
            pltpu.VMEM((ck, A_HEADS * tq), F32),
            pltpu.VMEM((ck, A_HEADS * tq), BF16),
        ],
        compiler_params=_cparams(("parallel", "arbitrary")),
        name="dsa_attention",
    )(a_grp, a_grp, a_grp, a_grp, sc)


FOX_T = 256
BIAS_LANE = HEAD_DIM
BIAS_TERMS = 3


def _fox_t_kernel(q_ref, k_ref, v_ref, sc_ref, bf_ref, o_ref,
                  kaug_ref, vt_ref, qz_ref, m_ref, l_ref, alpha_ref, acc_ref, s_ref, p_ref, *, t):
    i = pl.program_id(1)
    seq = k_ref.shape[1]
    n_chunks = seq // t
    pairs = B_HEADS // 2
    lane = lax.broadcasted_iota(jnp.int32, (t, LANES), 1)
    lo_half = lane < HEAD_DIM

    def bias_lanes(h):
        first = BIAS_LANE + BIAS_TERMS * h
        return (lane >= first) & (lane < first + BIAS_TERMS)

    @pl.when(i == 0)
    def _():
        r = lax.broadcasted_iota(jnp.int32, (t, t), 0)
        c = lax.broadcasted_iota(jnp.int32, (t, t), 1)
        lower = jnp.where(c <= r, 1.0, 0.0)
        src = lax.broadcasted_iota(jnp.int32, (LANES, LANES), 0)
        dst = lax.broadcasted_iota(jnp.int32, (LANES, LANES), 1)

        def build(n, carry):
            rows = pl.ds(pl.multiple_of(n * t, t), t)
            x = sc_ref[0, rows, :] + bf_ref[...]
            log_f = jnp.minimum(x, 0.0) - jnp.log1p(jnp.exp(-jnp.abs(x)))
            cum = _dot_hi(lower, log_f) + carry
            nc = -cum
            hi = nc.astype(BF16)
            rest = nc - hi.astype(F32)
            mid = rest.astype(BF16)
            lo = (rest - mid.astype(F32)).astype(BF16)
            placed = None
            for j, term in enumerate((hi, mid, lo)):
                place = jnp.where((src >= SC_BF) & (src < SC_BF + B_HEADS)
                                  & (dst == BIAS_LANE + BIAS_TERMS * (src - SC_BF) + j), 1.0, 0.0)
                part = _dot(term, place.astype(BF16))
                placed = part if placed is None else placed + part
            for h in range(B_HEADS):
                hp = h // 2
                kt = k_ref[0, rows, hp * LANES:(hp + 1) * LANES].astype(F32)
                if h % 2:
                    kt = pltpu.roll(kt, HEAD_DIM, 1)
                tile = jnp.where(lo_half, kt, jnp.where(bias_lanes(h), placed, 0.0))
                kaug_ref[h, rows, :] = tile.astype(BF16)
            for hp in range(pairs):
                vt_ref[hp, n] = v_ref[0, rows, hp * LANES:(hp + 1) * LANES].astype(F32).T.astype(BF16)
            return cum[t - 1:t, :]

        lax.fori_loop(0, n_chunks, build, jnp.zeros((1, LANES), F32))

    for hp in range(pairs):
        q = q_ref[0, :, hp * LANES:(hp + 1) * LANES].astype(F32) * (HEAD_DIM ** -0.5)
        qz_ref[(2 * hp) * t:(2 * hp + 1) * t, :] = jnp.where(
            lo_half, q, jnp.where(bias_lanes(2 * hp), 1.0, 0.0)).astype(BF16)
        qz_ref[(2 * hp + 1) * t:(2 * hp + 2) * t, :] = jnp.where(
            lo_half, pltpu.roll(q, HEAD_DIM, 1), jnp.where(bias_lanes(2 * hp + 1), 1.0, 0.0)).astype(BF16)
    causal = (lax.broadcasted_iota(jnp.int32, (t, t), 0)
              <= lax.broadcasted_iota(jnp.int32, (t, t), 1))
    m_ref[...] = jnp.full(m_ref.shape, NEG_BIG, F32)
    l_ref[...] = jnp.zeros(l_ref.shape, F32)
    acc_ref[...] = jnp.zeros(acc_ref.shape, F32)

    def logits(c, slot):
        rows = pl.ds(pl.multiple_of(c * t, t), t)
        for h in range(B_HEADS):
            cols = slice(h * t, (h + 1) * t)
            s_ref[slot, :, cols] = _dot_nt(kaug_ref[h, rows, :], qz_ref[cols, :])

    def attend(c, slot, masked):
        if not masked:
            logits(c + 1, 1 - slot)
        for h in range(B_HEADS):
            cols = slice(h * t, (h + 1) * t)
            s = s_ref[slot, :, cols]
            if masked:
                s = jnp.where(causal, s, -jnp.inf)
            m_old = m_ref[:, cols]
            m_new = jnp.maximum(m_old, jnp.max(s, axis=0, keepdims=True))
            alpha = jnp.exp(m_old - m_new)
            p = jnp.exp(s - jnp.tile(m_new, (t // 8, 1)))
            l_ref[:, cols] = alpha * l_ref[:, cols] + jnp.sum(p, axis=0, keepdims=True)
            m_ref[:, cols] = m_new
            alpha_ref[:, cols] = alpha
            p_ref[:, cols] = p.astype(BF16)
        for h in range(B_HEADS):
            cols = slice(h * t, (h + 1) * t)
            acc_ref[:, cols] = (jnp.tile(alpha_ref[:, cols], (LANES // 8, 1)) * acc_ref[:, cols]
                                + _dot(vt_ref[h // 2, c], p_ref[:, cols]))

    def attend_pair(k, _):
        attend(2 * k, 0, masked=False)
        attend(2 * k + 1, 1, masked=False)
        return 0

    logits(0, 0)
    lax.fori_loop(0, i // 2, attend_pair, 0)

    @pl.when(i % 2 == 0)
    def _():
        attend(i, 0, masked=True)

    @pl.when(i % 2 == 1)
    def _():
        attend(i - 1, 0, masked=False)
        attend(i, 1, masked=True)

    for hp in range(pairs):
        outs = []
        for h in (2 * hp, 2 * hp + 1):
            cols = slice(h * t, (h + 1) * t)
            out_t = acc_ref[:, cols] / jnp.tile(l_ref[:, cols], (LANES // 8, 1))
            outs.append(out_t.T)
        o_ref[0, :, hp * LANES:(hp + 1) * LANES] = jnp.where(
            lo_half, outs[0], outs[1]).astype(o_ref.dtype)


def _fox_t(b_grp, sc, bf_row):
    b, s, _ = b_grp.shape
    t = min(FOX_T, s)
    pairs = B_HEADS // 2
    return pl.pallas_call(
        functools.partial(_fox_t_kernel, t=t),
        grid=(b, s // t),
        in_specs=[
            pl.BlockSpec((1, t, BRANCH), lambda bi, i: (bi, i, 0)),
            pl.BlockSpec((1, s, BRANCH), lambda bi, i: (bi, 0, 1)),
            pl.BlockSpec((1, s, BRANCH), lambda bi, i: (bi, 0, 2)),
            pl.BlockSpec((1, s, LANES), lambda bi, i: (bi, 0, 0)),
            _resident((1, LANES)),
        ],
        out_specs=pl.BlockSpec((1, t, BRANCH), lambda bi, i: (bi, i, 0)),
        out_shape=jax.ShapeDtypeStruct((b, s, BRANCH), BF16),
        scratch_shapes=[
            pltpu.VMEM((B_HEADS, s, LANES), BF16),
            pltpu.VMEM((pairs, s // t, LANES, t), BF16),
            pltpu.VMEM((B_HEADS * t, LANES), BF16),
            pltpu.VMEM((8, B_HEADS * t), F32),
            pltpu.VMEM((8, B_HEADS * t), F32),
            pltpu.VMEM((8, B_HEADS * t), F32),
            pltpu.VMEM((LANES, B_HEADS * t), F32),
            pltpu.VMEM((2, t, B_HEADS * t), F32),
            pltpu.VMEM((t, B_HEADS * t), BF16),
        ],
        compiler_params=_cparams(("parallel", "arbitrary")),
        name="fox_attention",
    )(b_grp, b_grp, b_grp, sc, bf_row)


GDN_TS = 512
GDN_C = 128
TAIL = 8
NEUMANN_ROUNDS = 6


def _gdn_kernel(x_ref, z_ref, sc_ref, cw_ref, alog_ref, dtb_ref, dn_ref, o_ref,
                state_ref, tail_ref, ext_ref, act_ref, np_ref, rhs_ref,
                qk_ref, qg_ref, kdt_ref, dec_ref, *, ts):
    j = pl.program_id(1)
    width = x_ref.shape[2]
    n_chunks = ts // GDN_C
    chains = [(h, n) for n in range(n_chunks) for h in range(C_HEADS)]

    @pl.when(j == 0)
    def _():
        state_ref[...] = jnp.zeros_like(state_ref)
        tail_ref[...] = jnp.zeros_like(tail_ref)

    x = x_ref[0]
    ext_ref[0:TAIL, :] = tail_ref[...]
    ext_ref[TAIL:TAIL + ts, :] = x
    tail_ref[...] = x[ts - TAIL:ts, :]
    conv = jnp.zeros((ts, width), F32)
    for tap in range(CONV_WIDTH):
        start = TAIL - (CONV_WIDTH - 1) + tap
        conv = conv + cw_ref[tap:tap + 1, :] * ext_ref[start:start + ts, :]
    act_ref[...] = _silu(conv)

    sc = sc_ref[0]
    beta_all = jax.nn.sigmoid(sc)
    xg = sc + dtb_ref[...]
    softplus = jnp.maximum(xg, 0.0) + jnp.log1p(jnp.exp(-jnp.abs(xg)))
    g_all = -jnp.exp(alog_ref[...]) * softplus
    r = lax.broadcasted_iota(jnp.int32, (ts, ts), 0)
    c = lax.broadcasted_iota(jnp.int32, (ts, ts), 1)
    same_chunk_lower = jnp.where((c <= r) & (r // GDN_C == c // GDN_C), 1.0, 0.0)
    gc_all = _dot_hi(same_chunk_lower, g_all)
    gc_t = gc_all.T

    ri = lax.broadcasted_iota(jnp.int32, (GDN_C, GDN_C), 0)
    ci = lax.broadcasted_iota(jnp.int32, (GDN_C, GDN_C), 1)
    incl = ri >= ci
    strict = ri > ci

    def l2n(t):
        return t * lax.rsqrt(jnp.sum(t * t, axis=-1, keepdims=True) + L2_EPS)

    for idx, (h, n) in enumerate(chains):
        rows = slice(n * GDN_C, (n + 1) * GDN_C)
        q = l2n(act_ref[rows, h * C_DIM:(h + 1) * C_DIM]) * (C_DIM ** -0.5)
        k = l2n(act_ref[rows, BRANCH + h * C_DIM:BRANCH + (h + 1) * C_DIM])
        v = act_ref[rows, 2 * BRANCH + h * C_DIM:2 * BRANCH + (h + 1) * C_DIM]
        beta = beta_all[rows, SC_BETA + h:SC_BETA + h + 1]
        gcol = gc_all[rows, SC_DECAY + h:SC_DECAY + h + 1]
        grow = gc_t[SC_DECAY + h:SC_DECAY + h + 1, n * GDN_C:(n + 1) * GDN_C]
        g_last = gcol[GDN_C - 1:GDN_C, :]
        decay = jnp.exp(jnp.where(incl, gcol - grow, -jnp.inf))
        kb = k * beta
        k16 = k.astype(BF16)
        q16 = q.astype(BF16)
        a_mat = jnp.where(strict, _dot_nt(kb.astype(BF16), k16) * decay, 0.0)
        np_ref[idx, :, 0:GDN_C] = -a_mat
        np_ref[idx, :, GDN_C:2 * GDN_C] = _dot_solve(a_mat, a_mat)
        rhs_ref[idx, :, 0:C_DIM] = v * beta
        rhs_ref[idx, :, C_DIM:2 * C_DIM] = kb * jnp.exp(gcol)
        qk_ref[idx] = jnp.where(incl, _dot_nt(q16, k16) * decay, 0.0).astype(BF16)
        qg_ref[idx] = (q * jnp.exp(gcol)).astype(BF16)
        kdt_ref[idx] = (k * jnp.exp(g_last - gcol)).T.astype(BF16)
        dec_ref[idx] = jnp.broadcast_to(jnp.exp(g_last), (1, C_DIM))

    for rnd in range(1, NEUMANN_ROUNDS + 1):
        last = rnd == NEUMANN_ROUNDS
        for idx in range(len(chains)):
            n_old = np_ref[idx, :, 0:GDN_C]
            p_old = np_ref[idx, :, GDN_C:2 * GDN_C]
            if last:
                np_ref[idx, :, 0:GDN_C] = n_old + p_old + _dot_solve(p_old, n_old)
            else:
                prod = _dot_solve(p_old, np_ref[idx])
                np_ref[idx, :, 0:GDN_C] = n_old + p_old + prod[:, 0:GDN_C]
                np_ref[idx, :, GDN_C:2 * GDN_C] = prod[:, GDN_C:2 * GDN_C]

    for idx in range(len(chains)):
        rhs = rhs_ref[idx]
        rhs_ref[idx] = rhs + _dot_solve(np_ref[idx, :, 0:GDN_C], rhs)

    for idx, (h, n) in enumerate(chains):
        rows = slice(n * GDN_C, (n + 1) * GDN_C)
        state = state_ref[h]
        s16 = state.astype(BF16)
        v_new = rhs_ref[idx, :, 0:C_DIM] - _dot(rhs_ref[idx, :, C_DIM:2 * C_DIM].astype(BF16), s16)
        vn16 = v_new.astype(BF16)
        o = _dot(qg_ref[idx], s16) + _dot(qk_ref[idx], vn16)
        state_ref[h] = state * dec_ref[idx] + _dot(kdt_ref[idx], vn16)
        z = z_ref[0, rows, h * C_DIM:(h + 1) * C_DIM]
        y = _rms(o, dn_ref[...]) * _silu(z)
        o_ref[0, rows, h * C_DIM:(h + 1) * C_DIM] = y.astype(o_ref.dtype)


def _gdn(cqkv, cz, sc, conv_w, alog_row, dtb_row, dn_row):
    b, s, width = cqkv.shape
    ts = min(GDN_TS, s)
    nch = C_HEADS * (ts // GDN_C)
    blk = lambda bi, j: (bi, j, 0)
    return pl.pallas_call(
        functools.partial(_gdn_kernel, ts=ts),
        grid=(b, s // ts),
        in_specs=[
            pl.BlockSpec((1, ts, width), blk),
            pl.BlockSpec((1, ts, BRANCH), blk),
            pl.BlockSpec((1, ts, LANES), blk),
            _resident((CONV_WIDTH, width)),
            _resident((1, LANES)),
            _resident((1, LANES)),
            _resident((1, C_DIM)),
        ],
        out_specs=pl.BlockSpec((1, ts, BRANCH), blk),
        out_shape=jax.ShapeDtypeStruct((b, s, BRANCH), BF16),
        scratch_shapes=[
            pltpu.VMEM((C_HEADS, C_DIM, C_DIM), F32),
            pltpu.VMEM((TAIL, width), F32),
            pltpu.VMEM((TAIL + ts, width), F32),
            pltpu.VMEM((ts, width), F32),
            pltpu.VMEM((nch, GDN_C, 2 * GDN_C), F32),
            pltpu.VMEM((nch, GDN_C, 2 * C_DIM), F32),
            pltpu.VMEM((nch, GDN_C, GDN_C), BF16),
            pltpu.VMEM((nch, GDN_C, C_DIM), BF16),
            pltpu.VMEM((nch, C_DIM, GDN_C), BF16),
            pltpu.VMEM((nch, 1, C_DIM), F32),
        ],
        compiler_params=_cparams(("parallel", "arbitrary")),
        name="gated_deltanet",
    )(cqkv, cz, sc, conv_w, alog_row, dtb_row, dn_row)


MERGE_TM = 512


def _merge_kernel(x_ref, ya_ref, yb_ref, yc_ref, gl_ref, bg_ref,
                  wa_ref, wb_ref, wc_ref, wo_ref, o_ref):
    merged = None
    for n, (y_ref, w_ref) in enumerate(((ya_ref, wa_ref), (yb_ref, wb_ref), (yc_ref, wc_ref))):
        cols = slice(n * D_MODEL, (n + 1) * D_MODEL)
        gate = jax.nn.sigmoid(gl_ref[:, cols] + bg_ref[:, cols])
        term = gate * _dot(y_ref[...], w_ref[...])
        merged = term if merged is None else merged + term
    o_ref[...] = x_ref[...] + _dot(merged.astype(BF16), wo_ref[...])


def _merge(x, ya, yb, yc, gl, b_gate, wa, wb, wc, wo):
    n = x.shape[0]
    tm = min(MERGE_TM, n)
    row = lambda i: (i, 0)
    return pl.pallas_call(
        _merge_kernel,
        grid=(n // tm,),
        in_specs=[
            pl.BlockSpec((tm, D_MODEL), row),
            pl.BlockSpec((tm, BRANCH), row),
            pl.BlockSpec((tm, BRANCH), row),
            pl.BlockSpec((tm, BRANCH), row),
            pl.BlockSpec((tm, 3 * D_MODEL), row),
            _resident((1, 3 * D_MODEL)),
            _resident((BRANCH, D_MODEL)),
            _resident((BRANCH, D_MODEL)),
            _resident((BRANCH, D_MODEL)),
            _resident((D_MODEL, D_MODEL)),
        ],
        out_specs=pl.BlockSpec((tm, D_MODEL), row),
        out_shape=jax.ShapeDtypeStruct((n, D_MODEL), F32),
        compiler_params=_cparams(("parallel",)),
        name="gated_merge",
    )(x, ya, yb, yc, gl, b_gate, wa, wb, wc, wo)


def _w_in_pieces():
    offs = np.concatenate([[0], np.cumsum(IN_SIZES)])
    names = ("a_q", "a_k", "a_v", "i_q", "i_k", "i_w", "b_qkv", "b_f", "c_qkv", "c_z",
             "c_beta", "c_a", "gates")
    src = {n: (int(offs[i]), int(IN_SIZES[i])) for i, n in enumerate(names)}
    order = (("a_q", 0), ("i_q", A_Q_W), ("a_k", KEY_TILE * LANES),
             ("i_k", KEY_TILE * LANES + HEAD_DIM), ("a_v", VALUE_TILE * LANES),
             ("b_f", W_A + SC_BF), ("i_w", W_A + SC_IW), ("c_beta", W_A + SC_BETA),
             ("c_a", W_A + SC_DECAY), ("b_qkv", W_A + W_SC), ("c_qkv", W_A + W_SC + W_B),
             ("c_z", W_A + W_SC + W_B + W_CQKV), ("gates", W_A + W_SC + W_B + W_CQKV + W_CZ))
    pieces = []
    for name, dst in order:
        s0, width = src[name]
        done = 0
        while done < width:
            step = min(width - done, LANES - (dst + done) % LANES)
            pieces.append((dst + done, s0 + done, step))
            done += step
    return pieces


RELAYOUT_TM = 256


def _relayout_kernel(w_ref, tail_ref, o_ref, *, aligned_cols):
    by_tile = {}
    for dst, src, n in _w_in_pieces():
        by_tile.setdefault(dst // LANES, []).append((dst % LANES, src, n))
    row = lax.broadcasted_iota(jnp.int32, (2 * LANES, LANES), 0)
    col = lax.broadcasted_iota(jnp.int32, (2 * LANES, LANES), 1)

    def src_tile(start):
        if start + LANES <= aligned_cols:
            return w_ref[0, :, start:start + LANES]
        return tail_ref[0, :, start - aligned_cols:start - aligned_cols + LANES]

    for tile in range(W_TOTAL // LANES):
        acc = None
        for dst_lane, src, n in by_tile.get(tile, ()):
            base = (src // LANES) * LANES
            shift = src - base - dst_lane
            select = (row == col + shift) & (col >= dst_lane) & (col < dst_lane + n)
            window = jnp.concatenate([src_tile(base), src_tile(base + LANES)], axis=1)
            part = _dot(window.astype(BF16), jnp.where(select, 1.0, 0.0).astype(BF16))
            acc = part if acc is None else acc + part
        if acc is None:
            acc = jnp.zeros((w_ref.shape[1], LANES), F32)
        o_ref[0, :, tile * LANES:(tile + 1) * LANES] = acc.astype(BF16)


def _layout_w_in(w_in):
    depth, rows, cols = w_in.shape
    aligned_cols = (cols // LANES) * LANES
    tail = jnp.pad(w_in[:, :, aligned_cols:], ((0, 0), (0, 0), (0, 2 * LANES - (cols - aligned_cols))))
    tm = min(RELAYOUT_TM, rows)
    return pl.pallas_call(
        functools.partial(_relayout_kernel, aligned_cols=aligned_cols),
        grid=(depth, rows // tm),
        in_specs=[pl.BlockSpec((1, tm, cols), lambda l, i: (l, i, 0)),
                  pl.BlockSpec((1, tm, 2 * LANES), lambda l, i: (l, i, 0))],
        out_specs=pl.BlockSpec((1, tm, W_TOTAL), lambda l, i: (l, i, 0)),
        out_shape=jax.ShapeDtypeStruct((depth, rows, W_TOTAL), BF16),
        compiler_params=_cparams(("parallel", "parallel")),
        name="w_in_relayout",
    )(w_in, tail)


def _rotary_tables(seq):
    pos = jnp.arange(seq, dtype=F32)
    inv_freq = jnp.power(ROPE_THETA, -jnp.arange(0, ROT_DIM, 2, dtype=F32) / ROT_DIM)
    ang = pos[:, None] * inv_freq[None, :]
    cos, sin = jnp.cos(ang), jnp.sin(ang)
    half = ROT_DIM // 2
    ones = jnp.ones((seq, HEAD_DIM - ROT_DIM), F32)
    zeros_h = jnp.zeros((seq, half), F32)
    zeros_r = jnp.zeros((seq, HEAD_DIM - ROT_DIM), F32)
    c64 = jnp.concatenate([cos, cos, ones], axis=1)
    s1_64 = jnp.concatenate([zeros_h, sin, zeros_r], axis=1)
    s2_64 = jnp.concatenate([-sin, zeros_h, zeros_r], axis=1)
    twice = lambda t: jnp.concatenate([t, t], axis=1)
    return twice(c64), twice(s1_64), twice(s2_64)


def _lane_row(values, start):
    return jnp.zeros((1, LANES), F32).at[0, start:start + values.shape[0]].set(values.astype(F32))


def kernel(x, ffn1_norm, ffn1_w_in, ffn1_w_out, mix_norm, w_in, b_gate, b_forget, conv_w, a_log, dt_bias, delta_norm, w_branch_a, w_branch_b, w_branch_c, w_out, ffn2_norm, ffn2_w_in, ffn2_w_out, final_norm):
    batch, seq, _ = x.shape
    depth = w_in.shape[0]
    top_k = min(INDEX_TOPK, seq // 4)
    cos, s1, s2 = _rotary_tables(seq)
    w_mix = _layout_w_in(w_in)
    final_row = final_norm.reshape(1, D_MODEL)
    xt = x.reshape(batch * seq, D_MODEL)
    for l in range(depth):
        xt = _ffn(xt, ffn1_norm[l].reshape(1, D_MODEL), ffn1_w_in[l].astype(BF16),
                  ffn1_w_out[l].astype(BF16), final_row, False)
        a_grp, sc, b_grp, cqkv, cz, gl = _proj(
            xt, mix_norm[l].reshape(1, D_MODEL), w_mix[l], cos, s1, s2, seq)
        a_grp = a_grp.reshape(batch, seq, W_A)
        sc = sc.reshape(batch, seq, W_SC)
        y_a = _dsa_t(a_grp, sc, top_k)
        y_b = _fox_t(b_grp.reshape(batch, seq, W_B), sc, _lane_row(b_forget[l], SC_BF))
        y_c = _gdn(cqkv.reshape(batch, seq, W_CQKV), cz.reshape(batch, seq, W_CZ), sc,
                   conv_w[l], _lane_row(a_log[l], SC_DECAY), _lane_row(dt_bias[l], SC_DECAY),
                   delta_norm[l].reshape(1, C_DIM))
        flat = lambda t: t.reshape(batch * seq, BRANCH)
        xt = _merge(xt, flat(y_a), flat(y_b), flat(y_c), gl, b_gate[l].reshape(1, 3 * D_MODEL),
                    w_branch_a[l].astype(BF16), w_branch_b[l].astype(BF16),
                    w_branch_c[l].astype(BF16), w_out[l].astype(BF16))
        xt = _ffn(xt, ffn2_norm[l].reshape(1, D_MODEL), ffn2_w_in[l].astype(BF16),
                  ffn2_w_out[l].astype(BF16), final_row, l == depth - 1)
    return xt.reshape(batch, seq, D_MODEL)
```

```python
import functools

import numpy as np
import jax
import jax.numpy as jnp
from jax import lax
from jax.experimental import pallas as pl
from jax.experimental.pallas import tpu as pltpu

F32 = jnp.float32
BF16 = jnp.bfloat16

D_MODEL = 1024
BRANCH = 512
A_HEADS = 8
HEAD_DIM = 64
IDX_HEADS = 4
INDEX_TOPK = 256
B_HEADS = 8
C_HEADS = 4
C_DIM = 128
CONV_WIDTH = 4
ROPE_THETA = 500000.0
ROT_DIM = 16
FFN_DIM = 2048
NORM_EPS = 1e-6
L2_EPS = 1e-6
IN_SIZES = (512, 64, 64, 256, 64, 4, 1536, 8, 1536, 512, 4, 4, 3072)

LANES = 128
SC_BF = 0
SC_IW = 8
SC_BETA = 12
SC_DECAY = 16
W_A, W_SC, W_B, W_CQKV, W_CZ, W_G = 1024, 128, 1536, 1536, 512, 3072
W_TOTAL = W_A + W_SC + W_B + W_CQKV + W_CZ + W_G
A_Q_W = A_HEADS * HEAD_DIM
I_Q_W = IDX_HEADS * HEAD_DIM
I_Q_BLOCK = A_Q_W // I_Q_W
KEY_TILE = (A_Q_W + I_Q_W) // LANES
VALUE_TILE = KEY_TILE + 1
ROT_TILES = KEY_TILE + 1

NEG_BIG = -1e30
NEG_MASK = -2e30
VMEM_LIMIT = 56 * 1024 * 1024


def _cparams(sem):
    return pltpu.CompilerParams(dimension_semantics=sem, vmem_limit_bytes=VMEM_LIMIT)


def _dot(a, b):
    return jnp.dot(a, b, preferred_element_type=F32)


def _dot_hi(a, b):
    return jnp.dot(a, b, preferred_element_type=F32, precision=lax.Precision.HIGHEST)


def _dot_solve(a, b):
    a_hi = a.astype(BF16)
    b_hi = b.astype(BF16)
    a_lo = (a - a_hi.astype(F32)).astype(BF16)
    b_lo = (b - b_hi.astype(F32)).astype(BF16)
    m = a.shape[0]
    top = _dot(jnp.concatenate([a_hi, a_lo], axis=0), b_hi)
    return top[0:m] + top[m:2 * m] + _dot(a_hi, b_lo)


def _dot_nt(a, b):
    return lax.dot_general(a, b, (((1,), (1,)), ((), ())), preferred_element_type=F32)


def _rms(x, gain):
    return x * lax.rsqrt(jnp.mean(x * x, axis=-1, keepdims=True) + NORM_EPS) * gain


def _silu(x):
    return x * jax.nn.sigmoid(x)


def _resident(shape):
    nd = len(shape)
    return pl.BlockSpec(shape, lambda *_: (0,) * nd, pipeline_mode=pl.Buffered(1))


FFN_TM = 1024
FFN_CHUNK = 512


def _ffn_kernel(x_ref, g_ref, win_ref, wout_ref, fg_ref, o_ref, *, final):
    x = x_ref[...]
    h = _rms(x, g_ref[...]).astype(BF16)
    acc = jnp.zeros(x.shape, F32)
    for c in range(FFN_DIM // FFN_CHUNK):
        lo = c * FFN_CHUNK
        gate = _dot(h, win_ref[:, lo:lo + FFN_CHUNK])
        up = _dot(h, win_ref[:, FFN_DIM + lo:FFN_DIM + lo + FFN_CHUNK])
        act = (_silu(gate) * up).astype(BF16)
        acc = acc + _dot(act, wout_ref[lo:lo + FFN_CHUNK, :])
    y = x + 0.5 * acc
    if final:
        y = _rms(y, fg_ref[...])
    o_ref[...] = y


def _ffn(x, gain, w_in, w_out, final_gain, final):
    n = x.shape[0]
    tm = min(FFN_TM, n)
    return pl.pallas_call(
        functools.partial(_ffn_kernel, final=final),
        grid=(n // tm,),
        in_specs=[
            pl.BlockSpec((tm, D_MODEL), lambda i: (i, 0)),
            _resident((1, D_MODEL)),
            _resident((D_MODEL, 2 * FFN_DIM)),
            _resident((FFN_DIM, D_MODEL)),
            _resident((1, D_MODEL)),
        ],
        out_specs=pl.BlockSpec((tm, D_MODEL), lambda i: (i, 0)),
        out_shape=jax.ShapeDtypeStruct((n, D_MODEL), F32),
        compiler_params=_cparams(("parallel",)),
        name="ffn_half",
    )(x, gain, w_in, w_out, final_gain)


PROJ_TM = 512
PROJ_CHUNK = 512


def _proj_kernel(x_ref, g_ref, w_ref, cos_ref, s1_ref, s2_ref,
                 a_ref, sc_ref, b_ref, cqkv_ref, cz_ref, gl_ref):
    h = _rms(x_ref[...], g_ref[...]).astype(BF16)
    cos, s1, s2 = cos_ref[...], s1_ref[...], s2_ref[...]

    off = 0
    for j in range(W_A // LANES):
        t = _dot(h, w_ref[:, off:off + LANES])
        if j < ROT_TILES:
            t = t * cos + pltpu.roll(t, 8, 1) * s1 + pltpu.roll(t, LANES - 8, 1) * s2
        a_ref[:, off:off + LANES] = t.astype(BF16)
        off += LANES
    sc_ref[...] = _dot(h, w_ref[:, off:off + W_SC])
    off += W_SC
    for ref, width in ((b_ref, W_B), (cqkv_ref, W_CQKV), (cz_ref, W_CZ), (gl_ref, W_G)):
        for lo in range(0, width, PROJ_CHUNK):
            ref[:, lo:lo + PROJ_CHUNK] = _dot(
                h, w_ref[:, off + lo:off + lo + PROJ_CHUNK]).astype(ref.dtype)
        off += width


def _proj(x, gain, w, cos, s1, s2, seq):
    n = x.shape[0]
    tm = min(PROJ_TM, seq)
    per_seq = seq // tm
    row = lambda i: (i, 0)
    tab = lambda i: (i % per_seq, 0)
    widths = (W_A, W_SC, W_B, W_CQKV, W_CZ, W_G)
    dtypes = (BF16, F32, BF16, F32, F32, F32)
    return pl.pallas_call(
        _proj_kernel,
        grid=(n // tm,),
        in_specs=[
            pl.BlockSpec((tm, D_MODEL), row),
            _resident((1, D_MODEL)),
            _resident((D_MODEL, W_TOTAL)),
            pl.BlockSpec((tm, LANES), tab),
            pl.BlockSpec((tm, LANES), tab),
            pl.BlockSpec((tm, LANES), tab),
        ],
        out_specs=[pl.BlockSpec((tm, wd), row) for wd in widths],
        out_shape=[jax.ShapeDtypeStruct((n, wd), dt) for wd, dt in zip(widths, dtypes)],
        compiler_params=_cparams(("parallel",)),
        name="mixer_proj",
    )(x, gain, w, cos, s1, s2)


DSA_TQ = 512
DSA_CK = 512
INT_MIN = -2 ** 31
KEY_NEG_INF = int(np.array(-np.inf, np.float32).view(np.int32)) ^ 0x7FFFFFFF
SLAB = 64
SUM_ROW = HEAD_DIM
EXP_ROWS = 128


def _dsa_t_kernel(qa_ref, qi_ref, kk_ref, vv_ref, sc_ref, o_ref,
                  vt_ref, key_ref, bias_ref, qs_ref, m_ref, alpha_ref, acc_ref,
                  s_ref, p_ref, *, top_k, ck):
    i = pl.program_id(1)
    tq = qa_ref.shape[1]
    n_chunks_total = key_ref.shape[0]
    nck = (i * tq + tq + ck - 1) // ck
    lane = lax.broadcasted_iota(jnp.int32, (tq, LANES), 1)
    lo_half = lane < HEAD_DIM
    scale = HEAD_DIM ** -0.5

    @pl.when(i == 0)
    def _():
        ones_row = lax.broadcasted_iota(jnp.int32, (LANES, ck), 0) == SUM_ROW
        for c in range(n_chunks_total):
            vt = vv_ref[0, c * ck:(c + 1) * ck, :].astype(F32).T
            vt_ref[c] = jnp.where(ones_row, 1.0, vt).astype(BF16)

    def put(h, t):
        qs_ref[h * tq:(h + 1) * tq, :] = (t * scale).astype(BF16)

    for j in range(A_HEADS // 2):
        t = qa_ref[0, :, j * LANES:(j + 1) * LANES].astype(F32)
        put(2 * j, jnp.where(lo_half, t, 0.0))
        put(2 * j + 1, jnp.where(lo_half, pltpu.roll(t, HEAD_DIM, 1), 0.0))
    for j in range(IDX_HEADS // 2):
        t = qi_ref[0, :, j * LANES:(j + 1) * LANES].astype(F32)
        put(A_HEADS + 2 * j, jnp.where(lo_half, 0.0, pltpu.roll(t, HEAD_DIM, 1)))
        put(A_HEADS + 2 * j + 1, jnp.where(lo_half, 0.0, t))

    w_rows = (sc_ref[0] * (IDX_HEADS ** -0.5)).T
    key_pos = lax.broadcasted_iota(jnp.int32, (ck, tq), 0)
    qry_pos = i * tq + lax.broadcasted_iota(jnp.int32, (ck, tq), 1)

    def key_rows(c):
        return pl.ds(pl.multiple_of(c * ck, ck), ck)

    def score_chunk(c, _):
        kc = kk_ref[0, key_rows(c), :]
        score = None
        for h in range(IDX_HEADS):
            rel = jnp.maximum(
                _dot_nt(kc, qs_ref[(A_HEADS + h) * tq:(A_HEADS + h + 1) * tq, :]), 0.0)
            term = rel * w_rows[SC_IW + h:SC_IW + h + 1, :]
            score = term if score is None else score + term
        score = jnp.where(score == 0.0, 0.0, score)
        score = jnp.where(key_pos + c * ck <= qry_pos, score, -jnp.inf)
        bits = lax.bitcast_convert_type(score, jnp.int32)
        key_ref[c] = jnp.where(bits < 0, bits ^ 0x7FFFFFFF, bits)
        return 0

    lax.fori_loop(0, nck, score_chunk, 0)

    def count(pred):
        def body(c, acc):
            for r0 in range(0, ck, SLAB):
                acc = acc + jnp.where(pred(key_ref[c, r0:r0 + SLAB, :]), 1.0, 0.0)
            return acc
        acc = lax.fori_loop(0, nck, body, jnp.zeros((SLAB, tq), F32))
        return jnp.sum(acc, axis=0, keepdims=True)

    kf = float(top_k)

    def bit_step(b, thr):
        cand = thr + lax.shift_left(jnp.int32(1), 31 - b)
        cand_slab = jnp.broadcast_to(cand, (SLAB, tq))
        cnt = count(lambda kc: kc >= cand_slab)
        return jnp.where(cnt >= kf, cand, thr)

    thr = lax.fori_loop(0, 32, bit_step, jnp.full((1, tq), INT_MIN, jnp.int32))
    thr = jnp.maximum(thr, KEY_NEG_INF + 1)
    thr_slab = jnp.broadcast_to(thr, (SLAB, tq))
    cnt_ge = count(lambda kc: kc >= thr_slab)

    def write_bias(c, _):
        bias_ref[c] = jnp.where(key_ref[c] >= thr, 0.0, NEG_MASK)
        return 0

    lax.fori_loop(0, nck, write_bias, 0)

    @pl.when(jnp.max(cnt_ge) > kf)
    def _():
        need = kf - count(lambda kc: kc > thr_slab)
        r = lax.broadcasted_iota(jnp.int32, (ck, ck), 0)
        cidx = lax.broadcasted_iota(jnp.int32, (ck, ck), 1)
        lower = jnp.where(cidx <= r, 1.0, 0.0).astype(BF16)

        def tie_chunk(c, before):
            kc = key_ref[c]
            tie = kc == thr
            rank = before + _dot(lower, jnp.where(tie, 1.0, 0.0).astype(BF16))
            keep = (kc > thr) | (tie & (rank <= need))
            bias_ref[c] = jnp.where(keep, 0.0, NEG_MASK)
            return rank[ck - 1:ck, :]

        lax.fori_loop(0, nck, tie_chunk, jnp.zeros((1, tq), F32))

    m_ref[...] = jnp.full(m_ref.shape, NEG_BIG, F32)
    acc_ref[...] = jnp.zeros(acc_ref.shape, F32)
    reps = ck // 8

    n_att = A_HEADS * tq

    def attend(c, _):
        s_ref[...] = _dot_nt(kk_ref[0, key_rows(c), :], qs_ref[0:n_att, :])
        for tile in range(n_att // LANES):
            cols = slice(tile * LANES, (tile + 1) * LANES)
            qcols = slice((tile * LANES) % tq, (tile * LANES) % tq + LANES)
            part = None
            for r0 in range(0, ck, EXP_ROWS):
                rows = slice(r0, r0 + EXP_ROWS)
                s = s_ref[rows, cols] + bias_ref[c, rows, qcols]
                s_ref[rows, cols] = s
                unit = jnp.max(s.reshape(EXP_ROWS // 8, 8, LANES), axis=0)
                part = unit if part is None else jnp.maximum(part, unit)
            m_old = m_ref[:, cols]
            m_new = jnp.maximum(m_old, jnp.max(part, axis=0, keepdims=True))
            alpha_ref[:, cols] = jnp.exp(m_old - m_new)
            m_ref[:, cols] = m_new
        for tile in range(n_att // LANES):
            cols = slice(tile * LANES, (tile + 1) * LANES)
            for r0 in range(0, ck, EXP_ROWS):
                rows = slice(r0, r0 + EXP_ROWS)
                p = jnp.exp(s_ref[rows, cols] - jnp.tile(m_ref[:, cols], (EXP_ROWS // 8, 1)))
                p_ref[rows, cols] = p.astype(BF16)
        acc_ref[...] = (jnp.tile(alpha_ref[...], (LANES // 8, 1)) * acc_ref[...]
                        + _dot(vt_ref[c], p_ref[...]))
        return 0

    lax.fori_loop(0, nck, attend, 0)

    value_rows = lax.broadcasted_iota(jnp.int32, (LANES, tq), 0) < HEAD_DIM
    for j in range(A_HEADS // 2):
        outs = []
        for h in (2 * j, 2 * j + 1):
            cols = slice(h * tq, (h + 1) * tq)
            out_t = jnp.where(value_rows,
                              acc_ref[:, cols] / acc_ref[SUM_ROW:SUM_ROW + 1, cols], 0.0)
            outs.append(out_t.T)
        o_ref[0, :, j * LANES:(j + 1) * LANES] = (
            outs[0] + pltpu.roll(outs[1], HEAD_DIM, 1)).astype(o_ref.dtype)


def _dsa_t(a_grp, sc, top_k):
    b, s, _ = a_grp.shape
    tq = min(DSA_TQ, s)
    ck = min(DSA_CK, s)
    return pl.pallas_call(
        functools.partial(_dsa_t_kernel, top_k=top_k, ck=ck),
        grid=(b, s // tq),
        in_specs=[
            pl.BlockSpec((1, tq, A_Q_W), lambda bi, i: (bi, i, 0)),
            pl.BlockSpec((1, tq, I_Q_W), lambda bi, i: (bi, i, I_Q_BLOCK)),
            pl.BlockSpec((1, s, LANES), lambda bi, i: (bi, 0, KEY_TILE)),
            pl.BlockSpec((1, s, LANES), lambda bi, i: (bi, 0, VALUE_TILE)),
            pl.BlockSpec((1, tq, LANES), lambda bi, i: (bi, i, 0)),
        ],
        out_specs=pl.BlockSpec((1, tq, BRANCH), lambda bi, i: (bi, i, 0)),
        out_shape=jax.ShapeDtypeStruct((b, s, BRANCH), BF16),
        scratch_shapes=[
            pltpu.VMEM((s // ck, LANES, ck), BF16),
            pltpu.VMEM((s // ck, ck, tq), jnp.int32),
            pltpu.VMEM((s // ck, ck, tq), F32),
            pltpu.VMEM(((A_HEADS + IDX_HEADS) * tq, LANES), BF16),
            pltpu.VMEM((8, A_HEADS * tq), F32),
            pltpu.VMEM((8, A_HEADS * tq), F32),
            pltpu.VMEM((LANES, A_HEADS * tq), F32),
            pltpu.VMEM((ck, A_HEADS * tq), F32),
            pltpu.VMEM((ck, A_HEADS * tq), BF16),
        ],
        compiler_params=_cparams(("parallel", "arbitrary")),
        name="dsa_attention",
    )(a_grp, a_grp, a_grp, a_grp, sc)


FOX_T = 256
BIAS_LANE = HEAD_DIM
BIAS_TERMS = 3


def _fox_t_kernel(q_ref, k_ref, v_ref, sc_ref, bf_ref, o_ref,
                  kaug_ref, vt_ref, qz_ref, m_ref, alpha_ref, acc_ref, s_ref, p_ref, *, t):
    i = pl.program_id(1)
    seq = k_ref.shape[1]
    n_chunks = seq // t
    pairs = B_HEADS // 2
    lane = lax.broadcasted_iota(jnp.int32, (t, LANES), 1)
    lo_half = lane < HEAD_DIM

    def bias_lanes(h):
        first = BIAS_LANE + BIAS_TERMS * h
        return (lane >= first) & (lane < first + BIAS_TERMS)

    @pl.when(i == 0)
    def _():
        r = lax.broadcasted_iota(jnp.int32, (t, t), 0)
        c = lax.broadcasted_iota(jnp.int32, (t, t), 1)
        lower = jnp.where(c <= r, 1.0, 0.0)
        src = lax.broadcasted_iota(jnp.int32, (LANES, LANES), 0)
        dst = lax.broadcasted_iota(jnp.int32, (LANES, LANES), 1)

        def build(n, carry):
            rows = pl.ds(pl.multiple_of(n * t, t), t)
            x = sc_ref[0, rows, :] + bf_ref[...]
            log_f = jnp.minimum(x, 0.0) - jnp.log1p(jnp.exp(-jnp.abs(x)))
            cum = _dot_hi(lower, log_f) + carry
            nc = -cum
            hi = nc.astype(BF16)
            rest = nc - hi.astype(F32)
            mid = rest.astype(BF16)
            lo = (rest - mid.astype(F32)).astype(BF16)
            placed = None
            for j, term in enumerate((hi, mid, lo)):
                place = jnp.where((src >= SC_BF) & (src < SC_BF + B_HEADS)
                                  & (dst == BIAS_LANE + BIAS_TERMS * (src - SC_BF) + j), 1.0, 0.0)
                part = _dot(term, place.astype(BF16))
                placed = part if placed is None else placed + part
            for h in range(B_HEADS):
                hp = h // 2
                kt = k_ref[0, rows, hp * LANES:(hp + 1) * LANES].astype(F32)
                if h % 2:
                    kt = pltpu.roll(kt, HEAD_DIM, 1)
                tile = jnp.where(lo_half, kt, jnp.where(bias_lanes(h), placed, 0.0))
                kaug_ref[h, rows, :] = tile.astype(BF16)
            vrow = lax.broadcasted_iota(jnp.int32, (LANES, t), 0)
            for h in range(B_HEADS):
                hp = h // 2
                vt = v_ref[0, rows, hp * LANES:(hp + 1) * LANES].astype(F32)
                if h % 2:
                    vt = pltpu.roll(vt, HEAD_DIM, 1)
                vt_ref[h, n] = jnp.where(vrow < HEAD_DIM, vt.T,
                                         jnp.where(vrow == SUM_ROW, 1.0, 0.0)).astype(BF16)
            return cum[t - 1:t, :]

        lax.fori_loop(0, n_chunks, build, jnp.zeros((1, LANES), F32))

    for hp in range(pairs):
        q = q_ref[0, :, hp * LANES:(hp + 1) * LANES].astype(F32) * (HEAD_DIM ** -0.5)
        qz_ref[(2 * hp) * t:(2 * hp + 1) * t, :] = jnp.where(
            lo_half, q, jnp.where(bias_lanes(2 * hp), 1.0, 0.0)).astype(BF16)
        qz_ref[(2 * hp + 1) * t:(2 * hp + 2) * t, :] = jnp.where(
            lo_half, pltpu.roll(q, HEAD_DIM, 1), jnp.where(bias_lanes(2 * hp + 1), 1.0, 0.0)).astype(BF16)
    causal = (lax.broadcasted_iota(jnp.int32, (t, t), 0)
              <= lax.broadcasted_iota(jnp.int32, (t, t), 1))
    m_ref[...] = jnp.full(m_ref.shape, NEG_BIG, F32)
    acc_ref[...] = jnp.zeros(acc_ref.shape, F32)

    def logits(c, slot):
        rows = pl.ds(pl.multiple_of(c * t, t), t)
        for h in range(B_HEADS):
            cols = slice(h * t, (h + 1) * t)
            s_ref[slot, :, cols] = _dot_nt(kaug_ref[h, rows, :], qz_ref[cols, :])

    def attend(c, slot, masked):
        if not masked:
            logits(c + 1, 1 - slot)
        for h in range(B_HEADS):
            cols = slice(h * t, (h + 1) * t)
            s = s_ref[slot, :, cols]
            if masked:
                s = jnp.where(causal, s, -jnp.inf)
            m_old = m_ref[:, cols]
            m_new = jnp.maximum(m_old, jnp.max(s, axis=0, keepdims=True))
            alpha = jnp.exp(m_old - m_new)
            p = jnp.exp(s - jnp.tile(m_new, (t // 8, 1)))
            m_ref[:, cols] = m_new
            alpha_ref[:, cols] = alpha
            p_ref[:, cols] = p.astype(BF16)
        for h in range(B_HEADS):
            cols = slice(h * t, (h + 1) * t)
            acc_ref[:, cols] = (jnp.tile(alpha_ref[:, cols], (LANES // 8, 1)) * acc_ref[:, cols]
                                + _dot(vt_ref[h, c], p_ref[:, cols]))

    def attend_pair(k, _):
        attend(2 * k, 0, masked=False)
        attend(2 * k + 1, 1, masked=False)
        return 0

    logits(0, 0)
    lax.fori_loop(0, i // 2, attend_pair, 0)

    @pl.when(i % 2 == 0)
    def _():
        attend(i, 0, masked=True)

    @pl.when(i % 2 == 1)
    def _():
        attend(i - 1, 0, masked=False)
        attend(i, 1, masked=True)

    value_rows = lax.broadcasted_iota(jnp.int32, (LANES, t), 0) < HEAD_DIM
    for hp in range(pairs):
        outs = []
        for h in (2 * hp, 2 * hp + 1):
            cols = slice(h * t, (h + 1) * t)
            out_t = jnp.where(value_rows,
                              acc_ref[:, cols] / acc_ref[SUM_ROW:SUM_ROW + 1, cols], 0.0)
            outs.append(out_t.T)
        o_ref[0, :, hp * LANES:(hp + 1) * LANES] = (
            outs[0] + pltpu.roll(outs[1], HEAD_DIM, 1)).astype(o_ref.dtype)


def _fox_t(b_grp, sc, bf_row):
    b, s, _ = b_grp.shape
    t = min(FOX_T, s)
    return pl.pallas_call(
        functools.partial(_fox_t_kernel, t=t),
        grid=(b, s // t),
        in_specs=[
            pl.BlockSpec((1, t, BRANCH), lambda bi, i: (bi, i, 0)),
            pl.BlockSpec((1, s, BRANCH), lambda bi, i: (bi, 0, 1)),
            pl.BlockSpec((1, s, BRANCH), lambda bi, i: (bi, 0, 2)),
            pl.BlockSpec((1, s, LANES), lambda bi, i: (bi, 0, 0)),
            _resident((1, LANES)),
        ],
        out_specs=pl.BlockSpec((1, t, BRANCH), lambda bi, i: (bi, i, 0)),
        out_shape=jax.ShapeDtypeStruct((b, s, BRANCH), BF16),
        scratch_shapes=[
            pltpu.VMEM((B_HEADS, s, LANES), BF16),
            pltpu.VMEM((B_HEADS, s // t, LANES, t), BF16),
            pltpu.VMEM((B_HEADS * t, LANES), BF16),
            pltpu.VMEM((8, B_HEADS * t), F32),
            pltpu.VMEM((8, B_HEADS * t), F32),
            pltpu.VMEM((LANES, B_HEADS * t), F32),
            pltpu.VMEM((2, t, B_HEADS * t), F32),
            pltpu.VMEM((t, B_HEADS * t), BF16),
        ],
        compiler_params=_cparams(("parallel", "arbitrary")),
        name="fox_attention",
    )(b_grp, b_grp, b_grp, sc, bf_row)


GDN_TS = 512
GDN_C = 128
TAIL = 8
NEUMANN_ROUNDS = 6


def _gdn_kernel(x_ref, z_ref, sc_ref, cw_ref, alog_ref, dtb_ref, dn_ref, o_ref,
                state_ref, tail_ref, ext_ref, act_ref, np_ref, rhs_ref,
                qk_ref, qg_ref, kdt_ref, dec_ref, *, ts):
    j = pl.program_id(1)
    width = x_ref.shape[2]
    n_chunks = ts // GDN_C
    chains = [(h, n) for n in range(n_chunks) for h in range(C_HEADS)]

    @pl.when(j == 0)
    def _():
        state_ref[...] = jnp.zeros_like(state_ref)
        tail_ref[...] = jnp.zeros_like(tail_ref)

    x = x_ref[0]
    ext_ref[0:TAIL, :] = tail_ref[...]
    ext_ref[TAIL:TAIL + ts, :] = x
    tail_ref[...] = x[ts - TAIL:ts, :]
    conv = jnp.zeros((ts, width), F32)
    for tap in range(CONV_WIDTH):
        start = TAIL - (CONV_WIDTH - 1) + tap
        conv = conv + cw_ref[tap:tap + 1, :] * ext_ref[start:start + ts, :]
    act_ref[...] = _silu(conv)

    sc = sc_ref[0]
    beta_all = jax.nn.sigmoid(sc)
    xg = sc + dtb_ref[...]
    softplus = jnp.maximum(xg, 0.0) + jnp.log1p(jnp.exp(-jnp.abs(xg)))
    g_all = -jnp.exp(alog_ref[...]) * softplus
    r = lax.broadcasted_iota(jnp.int32, (ts, ts), 0)
    c = lax.broadcasted_iota(jnp.int32, (ts, ts), 1)
    same_chunk_lower = jnp.where((c <= r) & (r // GDN_C == c // GDN_C), 1.0, 0.0)
    gc_all = _dot_hi(same_chunk_lower, g_all)
    gc_t = gc_all.T

    ri = lax.broadcasted_iota(jnp.int32, (GDN_C, GDN_C), 0)
    ci = lax.broadcasted_iota(jnp.int32, (GDN_C, GDN_C), 1)
    incl = ri >= ci
    strict = ri > ci

    def l2n(t):
        return t * lax.rsqrt(jnp.sum(t * t, axis=-1, keepdims=True) + L2_EPS)

    for idx, (h, n) in enumerate(chains):
        rows = slice(n * GDN_C, (n + 1) * GDN_C)
        q = l2n(act_ref[rows, h * C_DIM:(h + 1) * C_DIM]) * (C_DIM ** -0.5)
        k = l2n(act_ref[rows, BRANCH + h * C_DIM:BRANCH + (h + 1) * C_DIM])
        v = act_ref[rows, 2 * BRANCH + h * C_DIM:2 * BRANCH + (h + 1) * C_DIM]
        beta = beta_all[rows, SC_BETA + h:SC_BETA + h + 1]
        gcol = gc_all[rows, SC_DECAY + h:SC_DECAY + h + 1]
        grow = gc_t[SC_DECAY + h:SC_DECAY + h + 1, n * GDN_C:(n + 1) * GDN_C]
        g_last = gcol[GDN_C - 1:GDN_C, :]
        decay = jnp.exp(jnp.where(incl, gcol - grow, -jnp.inf))
        kb = k * beta
        k16 = k.astype(BF16)
        q16 = q.astype(BF16)
        a_mat = jnp.where(strict, _dot_nt(kb.astype(BF16), k16) * decay, 0.0)
        np_ref[idx, :, 0:GDN_C] = -a_mat
        np_ref[idx, :, GDN_C:2 * GDN_C] = _dot_solve(a_mat, a_mat)
        rhs_ref[idx, :, 0:C_DIM] = v * beta
        rhs_ref[idx, :, C_DIM:2 * C_DIM] = kb * jnp.exp(gcol)
        qk_ref[idx] = jnp.where(incl, _dot_nt(q16, k16) * decay, 0.0).astype(BF16)
        qg_ref[idx] = (q * jnp.exp(gcol)).astype(BF16)
        kdt_ref[idx] = (k * jnp.exp(g_last - gcol)).T.astype(BF16)
        dec_ref[idx] = jnp.broadcast_to(jnp.exp(g_last), (1, C_DIM))

    for rnd in range(1, NEUMANN_ROUNDS + 1):
        last = rnd == NEUMANN_ROUNDS
        for idx in range(len(chains)):
            n_old = np_ref[idx, :, 0:GDN_C]
            p_old = np_ref[idx, :, GDN_C:2 * GDN_C]
            if last:
                np_ref[idx, :, 0:GDN_C] = n_old + p_old + _dot_solve(p_old, n_old)
            else:
                prod = _dot_solve(p_old, np_ref[idx])
                np_ref[idx, :, 0:GDN_C] = n_old + p_old + prod[:, 0:GDN_C]
                np_ref[idx, :, GDN_C:2 * GDN_C] = prod[:, GDN_C:2 * GDN_C]

    for idx in range(len(chains)):
        rhs = rhs_ref[idx]
        rhs_ref[idx] = rhs + _dot_solve(np_ref[idx, :, 0:GDN_C], rhs)

    for idx, (h, n) in enumerate(chains):
        rows = slice(n * GDN_C, (n + 1) * GDN_C)
        state = state_ref[h]
        s16 = state.astype(BF16)
        v_new = rhs_ref[idx, :, 0:C_DIM] - _dot(rhs_ref[idx, :, C_DIM:2 * C_DIM].astype(BF16), s16)
        vn16 = v_new.astype(BF16)
        o = _dot(qg_ref[idx], s16) + _dot(qk_ref[idx], vn16)
        state_ref[h] = state * dec_ref[idx] + _dot(kdt_ref[idx], vn16)
        z = z_ref[0, rows, h * C_DIM:(h + 1) * C_DIM]
        y = _rms(o, dn_ref[...]) * _silu(z)
        o_ref[0, rows, h * C_DIM:(h + 1) * C_DIM] = y.astype(o_ref.dtype)


def _gdn(cqkv, cz, sc, conv_w, alog_row, dtb_row, dn_row):
    b, s, width = cqkv.shape
    ts = min(GDN_TS, s)
    nch = C_HEADS * (ts // GDN_C)
    blk = lambda bi, j: (bi, j, 0)
    return pl.pallas_call(
        functools.partial(_gdn_kernel, ts=ts),
        grid=(b, s // ts),
        in_specs=[
            pl.BlockSpec((1, ts, width), blk),
            pl.BlockSpec((1, ts, BRANCH), blk),
            pl.BlockSpec((1, ts, LANES), blk),
            _resident((CONV_WIDTH, width)),
            _resident((1, LANES)),
            _resident((1, LANES)),
            _resident((1, C_DIM)),
        ],
        out_specs=pl.BlockSpec((1, ts, BRANCH), blk),
        out_shape=jax.ShapeDtypeStruct((b, s, BRANCH), BF16),
        scratch_shapes=[
            pltpu.VMEM((C_HEADS, C_DIM, C_DIM), F32),
            pltpu.VMEM((TAIL, width), F32),
            pltpu.VMEM((TAIL + ts, width), F32),
            pltpu.VMEM((ts, width), F32),
            pltpu.VMEM((nch, GDN_C, 2 * GDN_C), F32),
            pltpu.VMEM((nch, GDN_C, 2 * C_DIM), F32),
            pltpu.VMEM((nch, GDN_C, GDN_C), BF16),
            pltpu.VMEM((nch, GDN_C, C_DIM), BF16),
            pltpu.VMEM((nch, C_DIM, GDN_C), BF16),
            pltpu.VMEM((nch, 1, C_DIM), F32),
        ],
        compiler_params=_cparams(("parallel", "arbitrary")),
        name="gated_deltanet",
    )(cqkv, cz, sc, conv_w, alog_row, dtb_row, dn_row)


MERGE_TM = 512


def _merge_kernel(x_ref, ya_ref, yb_ref, yc_ref, gl_ref, bg_ref,
                  wa_ref, wb_ref, wc_ref, wo_ref, o_ref):
    merged = None
    for n, (y_ref, w_ref) in enumerate(((ya_ref, wa_ref), (yb_ref, wb_ref), (yc_ref, wc_ref))):
        cols = slice(n * D_MODEL, (n + 1) * D_MODEL)
        gate = jax.nn.sigmoid(gl_ref[:, cols] + bg_ref[:, cols])
        term = gate * _dot(y_ref[...], w_ref[...])
        merged = term if merged is None else merged + term
    o_ref[...] = x_ref[...] + _dot(merged.astype(BF16), wo_ref[...])


def _merge(x, ya, yb, yc, gl, b_gate, wa, wb, wc, wo):
    n = x.shape[0]
    tm = min(MERGE_TM, n)
    row = lambda i: (i, 0)
    return pl.pallas_call(
        _merge_kernel,
        grid=(n // tm,),
        in_specs=[
            pl.BlockSpec((tm, D_MODEL), row),
            pl.BlockSpec((tm, BRANCH), row),
            pl.BlockSpec((tm, BRANCH), row),
            pl.BlockSpec((tm, BRANCH), row),
            pl.BlockSpec((tm, 3 * D_MODEL), row),
            _resident((1, 3 * D_MODEL)),
            _resident((BRANCH, D_MODEL)),
            _resident((BRANCH, D_MODEL)),
            _resident((BRANCH, D_MODEL)),
            _resident((D_MODEL, D_MODEL)),
        ],
        out_specs=pl.BlockSpec((tm, D_MODEL), row),
        out_shape=jax.ShapeDtypeStruct((n, D_MODEL), F32),
        compiler_params=_cparams(("parallel",)),
        name="gated_merge",
    )(x, ya, yb, yc, gl, b_gate, wa, wb, wc, wo)


def _w_in_pieces():
    offs = np.concatenate([[0], np.cumsum(IN_SIZES)])
    names = ("a_q", "a_k", "a_v", "i_q", "i_k", "i_w", "b_qkv", "b_f", "c_qkv", "c_z",
             "c_beta", "c_a", "gates")
    src = {n: (int(offs[i]), int(IN_SIZES[i])) for i, n in enumerate(names)}
    order = (("a_q", 0), ("i_q", A_Q_W), ("a_k", KEY_TILE * LANES),
             ("i_k", KEY_TILE * LANES + HEAD_DIM), ("a_v", VALUE_TILE * LANES),
             ("b_f", W_A + SC_BF), ("i_w", W_A + SC_IW), ("c_beta", W_A + SC_BETA),
             ("c_a", W_A + SC_DECAY), ("b_qkv", W_A + W_SC), ("c_qkv", W_A + W_SC + W_B),
             ("c_z", W_A + W_SC + W_B + W_CQKV), ("gates", W_A + W_SC + W_B + W_CQKV + W_CZ))
    pieces = []
    for name, dst in order:
        s0, width = src[name]
        done = 0
        while done < width:
            step = min(width - done, LANES - (dst + done) % LANES)
            pieces.append((dst + done, s0 + done, step))
            done += step
    return pieces


RELAYOUT_TM = 256


def _relayout_kernel(w_ref, tail_ref, o_ref, *, aligned_cols):
    by_tile = {}
    for dst, src, n in _w_in_pieces():
        by_tile.setdefault(dst // LANES, []).append((dst % LANES, src, n))
    row = lax.broadcasted_iota(jnp.int32, (2 * LANES, LANES), 0)
    col = lax.broadcasted_iota(jnp.int32, (2 * LANES, LANES), 1)

    def src_tile(start):
        if start + LANES <= aligned_cols:
            return w_ref[0, :, start:start + LANES]
        return tail_ref[0, :, start - aligned_cols:start - aligned_cols + LANES]

    for tile in range(W_TOTAL // LANES):
        acc = None
        for dst_lane, src, n in by_tile.get(tile, ()):
            base = (src // LANES) * LANES
            shift = src - base - dst_lane
            select = (row == col + shift) & (col >= dst_lane) & (col < dst_lane + n)
            window = jnp.concatenate([src_tile(base), src_tile(base + LANES)], axis=1)
            part = _dot(window.astype(BF16), jnp.where(select, 1.0, 0.0).astype(BF16))
            acc = part if acc is None else acc + part
        if acc is None:
            acc = jnp.zeros((w_ref.shape[1], LANES), F32)
        o_ref[0, :, tile * LANES:(tile + 1) * LANES] = acc.astype(BF16)


def _layout_w_in(w_in):
    depth, rows, cols = w_in.shape
    aligned_cols = (cols // LANES) * LANES
    tail = jnp.pad(w_in[:, :, aligned_cols:], ((0, 0), (0, 0), (0, 2 * LANES - (cols - aligned_cols))))
    tm = min(RELAYOUT_TM, rows)
    return pl.pallas_call(
        functools.partial(_relayout_kernel, aligned_cols=aligned_cols),
        grid=(depth, rows // tm),
        in_specs=[pl.BlockSpec((1, tm, cols), lambda l, i: (l, i, 0)),
                  pl.BlockSpec((1, tm, 2 * LANES), lambda l, i: (l, i, 0))],
        out_specs=pl.BlockSpec((1, tm, W_TOTAL), lambda l, i: (l, i, 0)),
        out_shape=jax.ShapeDtypeStruct((depth, rows, W_TOTAL), BF16),
        compiler_params=_cparams(("parallel", "parallel")),
        name="w_in_relayout",
    )(w_in, tail)


def _rotary_tables(seq):
    pos = jnp.arange(seq, dtype=F32)
    inv_freq = jnp.power(ROPE_THETA, -jnp.arange(0, ROT_DIM, 2, dtype=F32) / ROT_DIM)
    ang = pos[:, None] * inv_freq[None, :]
    cos, sin = jnp.cos(ang), jnp.sin(ang)
    half = ROT_DIM // 2
    ones = jnp.ones((seq, HEAD_DIM - ROT_DIM), F32)
    zeros_h = jnp.zeros((seq, half), F32)
    zeros_r = jnp.zeros((seq, HEAD_DIM - ROT_DIM), F32)
    c64 = jnp.concatenate([cos, cos, ones], axis=1)
    s1_64 = jnp.concatenate([zeros_h, sin, zeros_r], axis=1)
    s2_64 = jnp.concatenate([-sin, zeros_h, zeros_r], axis=1)
    twice = lambda t: jnp.concatenate([t, t], axis=1)
    return twice(c64), twice(s1_64), twice(s2_64)


def _lane_row(values, start):
    return jnp.zeros((1, LANES), F32).at[0, start:start + values.shape[0]].set(values.astype(F32))


def kernel(x, ffn1_norm, ffn1_w_in, ffn1_w_out, mix_norm, w_in, b_gate, b_forget, conv_w, a_log, dt_bias, delta_norm, w_branch_a, w_branch_b, w_branch_c, w_out, ffn2_norm, ffn2_w_in, ffn2_w_out, final_norm):
    batch, seq, _ = x.shape
    depth = w_in.shape[0]
    top_k = min(INDEX_TOPK, seq // 4)
    cos, s1, s2 = _rotary_tables(seq)
    w_mix = _layout_w_in(w_in)
    final_row = final_norm.reshape(1, D_MODEL)
    xt = x.reshape(batch * seq, D_MODEL)
    for l in range(depth):
        xt = _ffn(xt, ffn1_norm[l].reshape(1, D_MODEL), ffn1_w_in[l].astype(BF16),
                  ffn1_w_out[l].astype(BF16), final_row, False)
        a_grp, sc, b_grp, cqkv, cz, gl = _proj(
            xt, mix_norm[l].reshape(1, D_MODEL), w_mix[l], cos, s1, s2, seq)
        a_grp = a_grp.reshape(batch, seq, W_A)
        sc = sc.reshape(batch, seq, W_SC)
        y_a = _dsa_t(a_grp, sc, top_k)
        y_b = _fox_t(b_grp.reshape(batch, seq, W_B), sc, _lane_row(b_forget[l], SC_BF))
        y_c = _gdn(cqkv.reshape(batch, seq, W_CQKV), cz.reshape(batch, seq, W_CZ), sc,
                   conv_w[l], _lane_row(a_log[l], SC_DECAY), _lane_row(dt_bias[l], SC_DECAY),
                   delta_norm[l].reshape(1, C_DIM))
        flat = lambda t: t.reshape(batch * seq, BRANCH)
        xt = _merge(xt, flat(y_a), flat(y_b), flat(y_c), gl, b_gate[l].reshape(1, 3 * D_MODEL),
                    w_branch_a[l].astype(BF16), w_branch_b[l].astype(BF16),
                    w_branch_c[l].astype(BF16), w_out[l].astype(BF16))
        xt = _ffn(xt, ffn2_norm[l].reshape(1, D_MODEL), ffn2_w_in[l].astype(BF16),
                  ffn2_w_out[l].astype(BF16), final_row, l == depth - 1)
    return xt.reshape(batch, seq, D_MODEL)
```

```python
import functools

import numpy as np
import jax
import jax.numpy as jnp
from jax import lax
from jax.experimental import pallas as pl
from jax.experimental.pallas import tpu as pltpu

F32 = jnp.float32
BF16 = jnp.bfloat16

D_MODEL = 1024
BRANCH = 512
A_HEADS = 8
HEAD_DIM = 64
IDX_HEADS = 4
INDEX_TOPK = 256
B_HEADS = 8
C_HEADS = 4
C_DIM = 128
CONV_WIDTH = 4
ROPE_THETA = 500000.0
ROT_DIM = 16
FFN_DIM = 2048
NORM_EPS = 1e-6
L2_EPS = 1e-6
IN_SIZES = (512, 64, 64, 256, 64, 4, 1536, 8, 1536, 512, 4, 4, 3072)

LANES = 128
SC_BF = 0
SC_IW = 8
SC_BETA = 12
SC_DECAY = 16
W_A, W_SC, W_B, W_CQKV, W_CZ, W_G = 1024, 128, 1536, 1536, 512, 3072
W_TOTAL = W_A + W_SC + W_B + W_CQKV + W_CZ + W_G
A_Q_W = A_HEADS * HEAD_DIM
I_Q_W = IDX_HEADS * HEAD_DIM
I_Q_BLOCK = A_Q_W // I_Q_W
KEY_TILE = (A_Q_W + I_Q_W) // LANES
VALUE_TILE = KEY_TILE + 1
ROT_TILES = KEY_TILE + 1

NEG_BIG = -1e30
NEG_MASK = -2e30
VMEM_LIMIT = 56 * 1024 * 1024


def _cparams(sem):
    return pltpu.CompilerParams(dimension_semantics=sem, vmem_limit_bytes=VMEM_LIMIT)


def _dot(a, b):
    return jnp.dot(a, b, preferred_element_type=F32)


def _dot_hi(a, b):
    return jnp.dot(a, b, preferred_element_type=F32, precision=lax.Precision.HIGHEST)


def _dot_solve(a, b):
    a_hi = a.astype(BF16)
    b_hi = b.astype(BF16)
    a_lo = (a - a_hi.astype(F32)).astype(BF16)
    b_lo = (b - b_hi.astype(F32)).astype(BF16)
    m = a.shape[0]
    top = _dot(jnp.concatenate([a_hi, a_lo], axis=0), b_hi)
    return top[0:m] + top[m:2 * m] + _dot(a_hi, b_lo)


def _dot_nt(a, b):
    return lax.dot_general(a, b, (((1,), (1,)), ((), ())), preferred_element_type=F32)


def _rms(x, gain):
    return x * lax.rsqrt(jnp.mean(x * x, axis=-1, keepdims=True) + NORM_EPS) * gain


def _silu(x):
    return x * jax.nn.sigmoid(x)


def _resident(shape):
    nd = len(shape)
    return pl.BlockSpec(shape, lambda *_: (0,) * nd, pipeline_mode=pl.Buffered(1))


FFN_TM = 1024
FFN_CHUNK = 512


def _ffn_kernel(x_ref, g_ref, win_ref, wout_ref, fg_ref, o_ref, *, final):
    x = x_ref[...]
    h = _rms(x, g_ref[...]).astype(BF16)
    acc = jnp.zeros(x.shape, F32)
    for c in range(FFN_DIM // FFN_CHUNK):
        lo = c * FFN_CHUNK
        gate = _dot(h, win_ref[:, lo:lo + FFN_CHUNK])
        up = _dot(h, win_ref[:, FFN_DIM + lo:FFN_DIM + lo + FFN_CHUNK])
        act = (_silu(gate) * up).astype(BF16)
        acc = acc + _dot(act, wout_ref[lo:lo + FFN_CHUNK, :])
    y = x + 0.5 * acc
    if final:
        y = _rms(y, fg_ref[...])
    o_ref[...] = y


def _ffn(x, gain, w_in, w_out, final_gain, final):
    n = x.shape[0]
    tm = min(FFN_TM, n)
    return pl.pallas_call(
        functools.partial(_ffn_kernel, final=final),
        grid=(n // tm,),
        in_specs=[
            pl.BlockSpec((tm, D_MODEL), lambda i: (i, 0)),
            _resident((1, D_MODEL)),
            _resident((D_MODEL, 2 * FFN_DIM)),
            _resident((FFN_DIM, D_MODEL)),
            _resident((1, D_MODEL)),
        ],
        out_specs=pl.BlockSpec((tm, D_MODEL), lambda i: (i, 0)),
        out_shape=jax.ShapeDtypeStruct((n, D_MODEL), F32),
        compiler_params=_cparams(("parallel",)),
        name="ffn_half",
    )(x, gain, w_in, w_out, final_gain)


PROJ_TM = 512
PROJ_CHUNK = 512


def _proj_kernel(x_ref, g_ref, w_ref, cos_ref, s1_ref, s2_ref,
                 a_ref, sc_ref, b_ref, cqkv_ref, cz_ref, gl_ref):
    h = _rms(x_ref[...], g_ref[...]).astype(BF16)
    cos, s1, s2 = cos_ref[...], s1_ref[...], s2_ref[...]

    off = 0
    for j in range(W_A // LANES):
        t = _dot(h, w_ref[:, off:off + LANES])
        if j < ROT_TILES:
            t = t * cos + pltpu.roll(t, 8, 1) * s1 + pltpu.roll(t, LANES - 8, 1) * s2
        a_ref[:, off:off + LANES] = t.astype(BF16)
        off += LANES
    sc_ref[...] = _dot(h, w_ref[:, off:off + W_SC])
    off += W_SC
    for ref, width in ((b_ref, W_B), (cqkv_ref, W_CQKV), (cz_ref, W_CZ), (gl_ref, W_G)):
        for lo in range(0, width, PROJ_CHUNK):
            ref[:, lo:lo + PROJ_CHUNK] = _dot(
                h, w_ref[:, off + lo:off + lo + PROJ_CHUNK]).astype(ref.dtype)
        off += width


def _proj(x, gain, w, cos, s1, s2, seq):
    n = x.shape[0]
    tm = min(PROJ_TM, seq)
    per_seq = seq // tm
    row = lambda i: (i, 0)
    tab = lambda i: (i % per_seq, 0)
    widths = (W_A, W_SC, W_B, W_CQKV, W_CZ, W_G)
    dtypes = (BF16, F32, BF16, F32, F32, F32)
    return pl.pallas_call(
        _proj_kernel,
        grid=(n // tm,),
        in_specs=[
            pl.BlockSpec((tm, D_MODEL), row),
            _resident((1, D_MODEL)),
            _resident((D_MODEL, W_TOTAL)),
            pl.BlockSpec((tm, LANES), tab),
            pl.BlockSpec((tm, LANES), tab),
            pl.BlockSpec((tm, LANES), tab),
        ],
        out_specs=[pl.BlockSpec((tm, wd), row) for wd in widths],
        out_shape=[jax.ShapeDtypeStruct((n, wd), dt) for wd, dt in zip(widths, dtypes)],
        compiler_params=_cparams(("parallel",)),
        name="mixer_proj",
    )(x, gain, w, cos, s1, s2)


DSA_TQ = 512
DSA_CK = 512
INT_MIN = -2 ** 31
KEY_NEG_INF = int(np.array(-np.inf, np.float32).view(np.int32)) ^ 0x7FFFFFFF
SLAB = 64
SUM_ROW = HEAD_DIM
EXP_ROWS = 128


def _dsa_t_kernel(qa_ref, qi_ref, kk_ref, vv_ref, sc_ref, o_ref,
                  vt_ref, key_ref, bias_ref, qs_ref, m_ref, alpha_ref, acc_ref,
                  s_ref, p_ref, *, top_k, ck):
    i = pl.program_id(1)
    tq = qa_ref.shape[1]
    n_chunks_total = key_ref.shape[0]
    nck = (i * tq + tq + ck - 1) // ck
    lane = lax.broadcasted_iota(jnp.int32, (tq, LANES), 1)
    lo_half = lane < HEAD_DIM
    scale = HEAD_DIM ** -0.5

    @pl.when(i == 0)
    def _():
        ones_row = lax.broadcasted_iota(jnp.int32, (LANES, ck), 0) == SUM_ROW
        for c in range(n_chunks_total):
            vt = vv_ref[0, c * ck:(c + 1) * ck, :].astype(F32).T
            vt_ref[c] = jnp.where(ones_row, 1.0, vt).astype(BF16)

    def put(h, t):
        qs_ref[h * tq:(h + 1) * tq, :] = (t * scale).astype(BF16)

    for j in range(A_HEADS // 2):
        t = qa_ref[0, :, j * LANES:(j + 1) * LANES].astype(F32)
        put(2 * j, jnp.where(lo_half, t, 0.0))
        put(2 * j + 1, jnp.where(lo_half, pltpu.roll(t, HEAD_DIM, 1), 0.0))
    for j in range(IDX_HEADS // 2):
        t = qi_ref[0, :, j * LANES:(j + 1) * LANES].astype(F32)
        put(A_HEADS + 2 * j, jnp.where(lo_half, 0.0, pltpu.roll(t, HEAD_DIM, 1)))
        put(A_HEADS + 2 * j + 1, jnp.where(lo_half, 0.0, t))

    w_rows = (sc_ref[0] * (IDX_HEADS ** -0.5)).T
    key_pos = lax.broadcasted_iota(jnp.int32, (ck, tq), 0)
    qry_pos = i * tq + lax.broadcasted_iota(jnp.int32, (ck, tq), 1)

    def key_rows(c):
        return pl.ds(pl.multiple_of(c * ck, ck), ck)

    def score_chunk(c, _):
        kc = kk_ref[0, key_rows(c), :]
        score = None
        for h in range(IDX_HEADS):
            rel = jnp.maximum(
                _dot_nt(kc, qs_ref[(A_HEADS + h) * tq:(A_HEADS + h + 1) * tq, :]), 0.0)
            term = rel * w_rows[SC_IW + h:SC_IW + h + 1, :]
            score = term if score is None else score + term
        score = jnp.where(score == 0.0, 0.0, score)
        score = jnp.where(key_pos + c * ck <= qry_pos, score, -jnp.inf)
        bits = lax.bitcast_convert_type(score, jnp.int32)
        key_ref[c] = jnp.where(bits < 0, bits ^ 0x7FFFFFFF, bits)
        return 0

    lax.fori_loop(0, nck, score_chunk, 0)

    def count(pred):
        def body(c, acc):
            for r0 in range(0, ck, SLAB):
                acc = acc + jnp.where(pred(key_ref[c, r0:r0 + SLAB, :]), 1.0, 0.0)
            return acc
        acc = lax.fori_loop(0, nck, body, jnp.zeros((SLAB, tq), F32))
        return jnp.sum(acc, axis=0, keepdims=True)

    kf = float(top_k)

    def bit_step(b, thr):
        cand = thr + lax.shift_left(jnp.int32(1), 31 - b)
        cand_slab = jnp.broadcast_to(cand, (SLAB, tq))
        cnt = count(lambda kc: kc >= cand_slab)
        return jnp.where(cnt >= kf, cand, thr)

    thr = lax.fori_loop(0, 32, bit_step, jnp.full((1, tq), INT_MIN, jnp.int32))
    thr = jnp.maximum(thr, KEY_NEG_INF + 1)
    thr_slab = jnp.broadcast_to(thr, (SLAB, tq))
    cnt_ge = count(lambda kc: kc >= thr_slab)

    def write_bias(c, _):
        bias_ref[c] = jnp.where(key_ref[c] >= thr, 0.0, NEG_MASK)
        return 0

    lax.fori_loop(0, nck, write_bias, 0)

    @pl.when(jnp.max(cnt_ge) > kf)
    def _():
        need = kf - count(lambda kc: kc > thr_slab)
        r = lax.broadcasted_iota(jnp.int32, (ck, ck), 0)
        cidx = lax.broadcasted_iota(jnp.int32, (ck, ck), 1)
        lower = jnp.where(cidx <= r, 1.0, 0.0).astype(BF16)

        def tie_chunk(c, before):
            kc = key_ref[c]
            tie = kc == thr
            rank = before + _dot(lower, jnp.where(tie, 1.0, 0.0).astype(BF16))
            keep = (kc > thr) | (tie & (rank <= need))
            bias_ref[c] = jnp.where(keep, 0.0, NEG_MASK)
            return rank[ck - 1:ck, :]

        lax.fori_loop(0, nck, tie_chunk, jnp.zeros((1, tq), F32))

    m_ref[...] = jnp.full(m_ref.shape, NEG_BIG, F32)
    acc_ref[...] = jnp.zeros(acc_ref.shape, F32)
    reps = ck // 8

    n_att = A_HEADS * tq

    def attend(c, _):
        s_ref[...] = _dot_nt(kk_ref[0, key_rows(c), :], qs_ref[0:n_att, :])
        for tile in range(n_att // LANES):
            cols = slice(tile * LANES, (tile + 1) * LANES)
            qcols = slice((tile * LANES) % tq, (tile * LANES) % tq + LANES)
            part = None
            for r0 in range(0, ck, EXP_ROWS):
                rows = slice(r0, r0 + EXP_ROWS)
                s = s_ref[rows, cols] + bias_ref[c, rows, qcols]
                s_ref[rows, cols] = s
                unit = jnp.max(s.reshape(EXP_ROWS // 8, 8, LANES), axis=0)
                part = unit if part is None else jnp.maximum(part, unit)
            m_old = m_ref[:, cols]
            m_new = jnp.maximum(m_old, jnp.max(part, axis=0, keepdims=True))
            alpha_ref[:, cols] = jnp.exp(m_old - m_new)
            m_ref[:, cols] = m_new
        for tile in range(n_att // LANES):
            cols = slice(tile * LANES, (tile + 1) * LANES)
            for r0 in range(0, ck, EXP_ROWS):
                rows = slice(r0, r0 + EXP_ROWS)
                p = jnp.exp(s_ref[rows, cols] - jnp.tile(m_ref[:, cols], (EXP_ROWS // 8, 1)))
                p_ref[rows, cols] = p.astype(BF16)
        acc_ref[...] = (jnp.tile(alpha_ref[...], (LANES // 8, 1)) * acc_ref[...]
                        + _dot(vt_ref[c], p_ref[...]))
        return 0

    lax.fori_loop(0, nck, attend, 0)

    value_rows = lax.broadcasted_iota(jnp.int32, (LANES, tq), 0) < HEAD_DIM
    for j in range(A_HEADS // 2):
        outs = []
        for h in (2 * j, 2 * j + 1):
            cols = slice(h * tq, (h + 1) * tq)
            out_t = jnp.where(value_rows,
                              acc_ref[:, cols] / acc_ref[SUM_ROW:SUM_ROW + 1, cols], 0.0)
            outs.append(out_t.T)
        o_ref[0, :, j * LANES:(j + 1) * LANES] = (
            outs[0] + pltpu.roll(outs[1], HEAD_DIM, 1)).astype(o_ref.dtype)


def _dsa_t(a_grp, sc, top_k):
    b, s, _ = a_grp.shape
    tq = min(DSA_TQ, s)
    ck = min(DSA_CK, s)
    return pl.pallas_call(
        functools.partial(_dsa_t_kernel, top_k=top_k, ck=ck),
        grid=(b, s // tq),
        in_specs=[
            pl.BlockSpec((1, tq, A_Q_W), lambda bi, i: (bi, i, 0)),
            pl.BlockSpec((1, tq, I_Q_W), lambda bi, i: (bi, i, I_Q_BLOCK)),
            pl.BlockSpec((1, s, LANES), lambda bi, i: (bi, 0, KEY_TILE)),
            pl.BlockSpec((1, s, LANES), lambda bi, i: (bi, 0, VALUE_TILE)),
            pl.BlockSpec((1, tq, LANES), lambda bi, i: (bi, i, 0)),
        ],
        out_specs=pl.BlockSpec((1, tq, BRANCH), lambda bi, i: (bi, i, 0)),
        out_shape=jax.ShapeDtypeStruct((b, s, BRANCH), BF16),
        scratch_shapes=[
            pltpu.VMEM((s // ck, LANES, ck), BF16),
            pltpu.VMEM((s // ck, ck, tq), jnp.int32),
            pltpu.VMEM((s // ck, ck, tq), F32),
            pltpu.VMEM(((A_HEADS + IDX_HEADS) * tq, LANES), BF16),
            pltpu.VMEM((8, A_HEADS * tq), F32),
            pltpu.VMEM((8, A_HEADS * tq), F32),
            pltpu.VMEM((LANES, A_HEADS * tq), F32),
            pltpu.VMEM((ck, A_HEADS * tq), F32),
            pltpu.VMEM((ck, A_HEADS * tq), BF16),
        ],
        compiler_params=_cparams(("parallel", "arbitrary")),
        name="dsa_attention",
    )(a_grp, a_grp, a_grp, a_grp, sc)


FOX_T = 512
BIAS_LANE = HEAD_DIM
BIAS_TERMS = 3


def _fox_t_kernel(q_ref, k_ref, v_ref, sc_ref, bf_ref, o_ref,
                  kaug_ref, vt_ref, qz_ref, m_ref, alpha_ref, acc_ref, s_ref, p_ref, *, t):
    i = pl.program_id(1)
    seq = k_ref.shape[1]
    n_chunks = seq // t
    pairs = B_HEADS // 2
    lane = lax.broadcasted_iota(jnp.int32, (t, LANES), 1)
    lo_half = lane < HEAD_DIM

    def bias_lanes(h):
        first = BIAS_LANE + BIAS_TERMS * h
        return (lane >= first) & (lane < first + BIAS_TERMS)

    @pl.when(i == 0)
    def _():
        r = lax.broadcasted_iota(jnp.int32, (t, t), 0)
        c = lax.broadcasted_iota(jnp.int32, (t, t), 1)
        lower = jnp.where(c <= r, 1.0, 0.0)
        src = lax.broadcasted_iota(jnp.int32, (LANES, LANES), 0)
        dst = lax.broadcasted_iota(jnp.int32, (LANES, LANES), 1)

        def build(n, carry):
            rows = pl.ds(pl.multiple_of(n * t, t), t)
            x = sc_ref[0, rows, :] + bf_ref[...]
            log_f = jnp.minimum(x, 0.0) - jnp.log1p(jnp.exp(-jnp.abs(x)))
            cum = _dot_hi(lower, log_f) + carry
            nc = -cum
            hi = nc.astype(BF16)
            rest = nc - hi.astype(F32)
            mid = rest.astype(BF16)
            lo = (rest - mid.astype(F32)).astype(BF16)
            placed = None
            for j, term in enumerate((hi, mid, lo)):
                place = jnp.where((src >= SC_BF) & (src < SC_BF + B_HEADS)
                                  & (dst == BIAS_LANE + BIAS_TERMS * (src - SC_BF) + j), 1.0, 0.0)
                part = _dot(term, place.astype(BF16))
                placed = part if placed is None else placed + part
            for h in range(B_HEADS):
                hp = h // 2
                kt = k_ref[0, rows, hp * LANES:(hp + 1) * LANES].astype(F32)
                if h % 2:
                    kt = pltpu.roll(kt, HEAD_DIM, 1)
                tile = jnp.where(lo_half, kt, jnp.where(bias_lanes(h), placed, 0.0))
                kaug_ref[h, rows, :] = tile.astype(BF16)
            vrow = lax.broadcasted_iota(jnp.int32, (LANES, t), 0)
            for h in range(B_HEADS):
                hp = h // 2
                vt = v_ref[0, rows, hp * LANES:(hp + 1) * LANES].astype(F32)
                if h % 2:
                    vt = pltpu.roll(vt, HEAD_DIM, 1)
                vt_ref[h, n] = jnp.where(vrow < HEAD_DIM, vt.T,
                                         jnp.where(vrow == SUM_ROW, 1.0, 0.0)).astype(BF16)
            return cum[t - 1:t, :]

        lax.fori_loop(0, n_chunks, build, jnp.zeros((1, LANES), F32))

    for hp in range(pairs):
        q = q_ref[0, :, hp * LANES:(hp + 1) * LANES].astype(F32) * (HEAD_DIM ** -0.5)
        qz_ref[(2 * hp) * t:(2 * hp + 1) * t, :] = jnp.where(
            lo_half, q, jnp.where(bias_lanes(2 * hp), 1.0, 0.0)).astype(BF16)
        qz_ref[(2 * hp + 1) * t:(2 * hp + 2) * t, :] = jnp.where(
            lo_half, pltpu.roll(q, HEAD_DIM, 1), jnp.where(bias_lanes(2 * hp + 1), 1.0, 0.0)).astype(BF16)
    causal = (lax.broadcasted_iota(jnp.int32, (t, t), 0)
              <= lax.broadcasted_iota(jnp.int32, (t, t), 1))
    m_ref[...] = jnp.full(m_ref.shape, NEG_BIG, F32)
    acc_ref[...] = jnp.zeros(acc_ref.shape, F32)

    def logits(c, slot):
        rows = pl.ds(pl.multiple_of(c * t, t), t)
        for h in range(B_HEADS):
            cols = slice(h * t, (h + 1) * t)
            s_ref[slot, :, cols] = _dot_nt(kaug_ref[h, rows, :], qz_ref[cols, :])

    def attend(c, slot, masked):
        if not masked:
            logits(c + 1, 1 - slot)
        for h in range(B_HEADS):
            cols = slice(h * t, (h + 1) * t)
            s = s_ref[slot, :, cols]
            if masked:
                s = jnp.where(causal, s, -jnp.inf)
            m_old = m_ref[:, cols]
            m_new = jnp.maximum(m_old, jnp.max(s, axis=0, keepdims=True))
            alpha = jnp.exp(m_old - m_new)
            p = jnp.exp(s - jnp.tile(m_new, (t // 8, 1)))
            m_ref[:, cols] = m_new
            alpha_ref[:, cols] = alpha
            p_ref[:, cols] = p.astype(BF16)
        for h in range(B_HEADS):
            cols = slice(h * t, (h + 1) * t)
            acc_ref[:, cols] = (jnp.tile(alpha_ref[:, cols], (LANES // 8, 1)) * acc_ref[:, cols]
                                + _dot(vt_ref[h, c], p_ref[:, cols]))

    def attend_pair(k, _):
        attend(2 * k, 0, masked=False)
        attend(2 * k + 1, 1, masked=False)
        return 0

    logits(0, 0)
    lax.fori_loop(0, i // 2, attend_pair, 0)

    @pl.when(i % 2 == 0)
    def _():
        attend(i, 0, masked=True)

    @pl.when(i % 2 == 1)
    def _():
        attend(i - 1, 0, masked=False)
        attend(i, 1, masked=True)

    value_rows = lax.broadcasted_iota(jnp.int32, (LANES, t), 0) < HEAD_DIM
    for hp in range(pairs):
        outs = []
        for h in (2 * hp, 2 * hp + 1):
            cols = slice(h * t, (h + 1) * t)
            out_t = jnp.where(value_rows,
                              acc_ref[:, cols] / acc_ref[SUM_ROW:SUM_ROW + 1, cols], 0.0)
            outs.append(out_t.T)
        o_ref[0, :, hp * LANES:(hp + 1) * LANES] = (
            outs[0] + pltpu.roll(outs[1], HEAD_DIM, 1)).astype(o_ref.dtype)


def _fox_t(b_grp, sc, bf_row):
    b, s, _ = b_grp.shape
    t = min(FOX_T, s)
    return pl.pallas_call(
        functools.partial(_fox_t_kernel, t=t),
        grid=(b, s // t),
        in_specs=[
            pl.BlockSpec((1, t, BRANCH), lambda bi, i: (bi, i, 0)),
            pl.BlockSpec((1, s, BRANCH), lambda bi, i: (bi, 0, 1), pipeline_mode=pl.Buffered(1)),
            pl.BlockSpec((1, s, BRANCH), lambda bi, i: (bi, 0, 2), pipeline_mode=pl.Buffered(1)),
            pl.BlockSpec((1, s, LANES), lambda bi, i: (bi, 0, 0)),
            _resident((1, LANES)),
        ],
        out_specs=pl.BlockSpec((1, t, BRANCH), lambda bi, i: (bi, i, 0)),
        out_shape=jax.ShapeDtypeStruct((b, s, BRANCH), BF16),
        scratch_shapes=[
            pltpu.VMEM((B_HEADS, s, LANES), BF16),
            pltpu.VMEM((B_HEADS, s // t, LANES, t), BF16),
            pltpu.VMEM((B_HEADS * t, LANES), BF16),
            pltpu.VMEM((8, B_HEADS * t), F32),
            pltpu.VMEM((8, B_HEADS * t), F32),
            pltpu.VMEM((LANES, B_HEADS * t), F32),
            pltpu.VMEM((2, t, B_HEADS * t), F32),
            pltpu.VMEM((t, B_HEADS * t), BF16),
        ],
        compiler_params=_cparams(("parallel", "arbitrary")),
        name="fox_attention",
    )(b_grp, b_grp, b_grp, sc, bf_row)


GDN_TS = 512
GDN_C = 128
TAIL = 8
NEUMANN_ROUNDS = 6


def _gdn_kernel(x_ref, z_ref, sc_ref, cw_ref, alog_ref, dtb_ref, dn_ref, o_ref,
                state_ref, tail_ref, ext_ref, act_ref, np_ref, rhs_ref,
                qk_ref, qg_ref, kdt_ref, dec_ref, *, ts):
    j = pl.program_id(1)
    width = x_ref.shape[2]
    n_chunks = ts // GDN_C
    chains = [(h, n) for n in range(n_chunks) for h in range(C_HEADS)]

    @pl.when(j == 0)
    def _():
        state_ref[...] = jnp.zeros_like(state_ref)
        tail_ref[...] = jnp.zeros_like(tail_ref)

    x = x_ref[0]
    ext_ref[0:TAIL, :] = tail_ref[...]
    ext_ref[TAIL:TAIL + ts, :] = x
    tail_ref[...] = x[ts - TAIL:ts, :]
    conv = jnp.zeros((ts, width), F32)
    for tap in range(CONV_WIDTH):
        start = TAIL - (CONV_WIDTH - 1) + tap
        conv = conv + cw_ref[tap:tap + 1, :] * ext_ref[start:start + ts, :]
    act_ref[...] = _silu(conv)

    sc = sc_ref[0]
    beta_all = jax.nn.sigmoid(sc)
    xg = sc + dtb_ref[...]
    softplus = jnp.maximum(xg, 0.0) + jnp.log1p(jnp.exp(-jnp.abs(xg)))
    g_all = -jnp.exp(alog_ref[...]) * softplus
    r = lax.broadcasted_iota(jnp.int32, (ts, ts), 0)
    c = lax.broadcasted_iota(jnp.int32, (ts, ts), 1)
    same_chunk_lower = jnp.where((c <= r) & (r // GDN_C == c // GDN_C), 1.0, 0.0)
    gc_all = _dot_hi(same_chunk_lower, g_all)
    gc_t = gc_all.T

    ri = lax.broadcasted_iota(jnp.int32, (GDN_C, GDN_C), 0)
    ci = lax.broadcasted_iota(jnp.int32, (GDN_C, GDN_C), 1)
    incl = ri >= ci
    strict = ri > ci

    def l2n(t):
        return t * lax.rsqrt(jnp.sum(t * t, axis=-1, keepdims=True) + L2_EPS)

    for idx, (h, n) in enumerate(chains):
        rows = slice(n * GDN_C, (n + 1) * GDN_C)
        q = l2n(act_ref[rows, h * C_DIM:(h + 1) * C_DIM]) * (C_DIM ** -0.5)
        k = l2n(act_ref[rows, BRANCH + h * C_DIM:BRANCH + (h + 1) * C_DIM])
        v = act_ref[rows, 2 * BRANCH + h * C_DIM:2 * BRANCH + (h + 1) * C_DIM]
        beta = beta_all[rows, SC_BETA + h:SC_BETA + h + 1]
        gcol = gc_all[rows, SC_DECAY + h:SC_DECAY + h + 1]
        grow = gc_t[SC_DECAY + h:SC_DECAY + h + 1, n * GDN_C:(n + 1) * GDN_C]
        g_last = gcol[GDN_C - 1:GDN_C, :]
        decay = jnp.exp(jnp.where(incl, gcol - grow, -jnp.inf))
        kb = k * beta
        k16 = k.astype(BF16)
        q16 = q.astype(BF16)
        a_mat = jnp.where(strict, _dot_nt(kb.astype(BF16), k16) * decay, 0.0)
        np_ref[idx, :, 0:GDN_C] = -a_mat
        np_ref[idx, :, GDN_C:2 * GDN_C] = _dot_solve(a_mat, a_mat)
        rhs_ref[idx, :, 0:C_DIM] = v * beta
        rhs_ref[idx, :, C_DIM:2 * C_DIM] = kb * jnp.exp(gcol)
        qk_ref[idx] = jnp.where(incl, _dot_nt(q16, k16) * decay, 0.0).astype(BF16)
        qg_ref[idx] = (q * jnp.exp(gcol)).astype(BF16)
        kdt_ref[idx] = (k * jnp.exp(g_last - gcol)).T.astype(BF16)
        dec_ref[idx] = jnp.broadcast_to(jnp.exp(g_last), (1, C_DIM))

    for rnd in range(1, NEUMANN_ROUNDS + 1):
        last = rnd == NEUMANN_ROUNDS
        for idx in range(len(chains)):
            n_old = np_ref[idx, :, 0:GDN_C]
            p_old = np_ref[idx, :, GDN_C:2 * GDN_C]
            if last:
                np_ref[idx, :, 0:GDN_C] = n_old + p_old + _dot_solve(p_old, n_old)
            else:
                prod = _dot_solve(p_old, np_ref[idx])
                np_ref[idx, :, 0:GDN_C] = n_old + p_old + prod[:, 0:GDN_C]
                np_ref[idx, :, GDN_C:2 * GDN_C] = prod[:, GDN_C:2 * GDN_C]

    for idx in range(len(chains)):
        rhs = rhs_ref[idx]
        rhs_ref[idx] = rhs + _dot_solve(np_ref[idx, :, 0:GDN_C], rhs)

    for idx, (h, n) in enumerate(chains):
        rows = slice(n * GDN_C, (n + 1) * GDN_C)
        state = state_ref[h]
        s16 = state.astype(BF16)
        v_new = rhs_ref[idx, :, 0:C_DIM] - _dot(rhs_ref[idx, :, C_DIM:2 * C_DIM].astype(BF16), s16)
        vn16 = v_new.astype(BF16)
        o = _dot(qg_ref[idx], s16) + _dot(qk_ref[idx], vn16)
        state_ref[h] = state * dec_ref[idx] + _dot(kdt_ref[idx], vn16)
        z = z_ref[0, rows, h * C_DIM:(h + 1) * C_DIM]
        y = _rms(o, dn_ref[...]) * _silu(z)
        o_ref[0, rows, h * C_DIM:(h + 1) * C_DIM] = y.astype(o_ref.dtype)


def _gdn(cqkv, cz, sc, conv_w, alog_row, dtb_row, dn_row):
    b, s, width = cqkv.shape
    ts = min(GDN_TS, s)
    nch = C_HEADS * (ts // GDN_C)
    blk = lambda bi, j: (bi, j, 0)
    return pl.pallas_call(
        functools.partial(_gdn_kernel, ts=ts),
        grid=(b, s // ts),
        in_specs=[
            pl.BlockSpec((1, ts, width), blk),
            pl.BlockSpec((1, ts, BRANCH), blk),
            pl.BlockSpec((1, ts, LANES), blk),
            _resident((CONV_WIDTH, width)),
            _resident((1, LANES)),
            _resident((1, LANES)),
            _resident((1, C_DIM)),
        ],
        out_specs=pl.BlockSpec((1, ts, BRANCH), blk),
        out_shape=jax.ShapeDtypeStruct((b, s, BRANCH), BF16),
        scratch_shapes=[
            pltpu.VMEM((C_HEADS, C_DIM, C_DIM), F32),
            pltpu.VMEM((TAIL, width), F32),
            pltpu.VMEM((TAIL + ts, width), F32),
            pltpu.VMEM((ts, width), F32),
            pltpu.VMEM((nch, GDN_C, 2 * GDN_C), F32),
            pltpu.VMEM((nch, GDN_C, 2 * C_DIM), F32),
            pltpu.VMEM((nch, GDN_C, GDN_C), BF16),
            pltpu.VMEM((nch, GDN_C, C_DIM), BF16),
            pltpu.VMEM((nch, C_DIM, GDN_C), BF16),
            pltpu.VMEM((nch, 1, C_DIM), F32),
        ],
        compiler_params=_cparams(("parallel", "arbitrary")),
        name="gated_deltanet",
    )(cqkv, cz, sc, conv_w, alog_row, dtb_row, dn_row)


MERGE_TM = 512


def _merge_kernel(x_ref, ya_ref, yb_ref, yc_ref, gl_ref, bg_ref,
                  wa_ref, wb_ref, wc_ref, wo_ref, o_ref):
    merged = None
    for n, (y_ref, w_ref) in enumerate(((ya_ref, wa_ref), (yb_ref, wb_ref), (yc_ref, wc_ref))):
        cols = slice(n * D_MODEL, (n + 1) * D_MODEL)
        gate = jax.nn.sigmoid(gl_ref[:, cols] + bg_ref[:, cols])
        term = gate * _dot(y_ref[...], w_ref[...])
        merged = term if merged is None else merged + term
    o_ref[...] = x_ref[...] + _dot(merged.astype(BF16), wo_ref[...])


def _merge(x, ya, yb, yc, gl, b_gate, wa, wb, wc, wo):
    n = x.shape[0]
    tm = min(MERGE_TM, n)
    row = lambda i: (i, 0)
    return pl.pallas_call(
        _merge_kernel,
        grid=(n // tm,),
        in_specs=[
            pl.BlockSpec((tm, D_MODEL), row),
            pl.BlockSpec((tm, BRANCH), row),
            pl.BlockSpec((tm, BRANCH), row),
            pl.BlockSpec((tm, BRANCH), row),
            pl.BlockSpec((tm, 3 * D_MODEL), row),
            _resident((1, 3 * D_MODEL)),
            _resident((BRANCH, D_MODEL)),
            _resident((BRANCH, D_MODEL)),
            _resident((BRANCH, D_MODEL)),
            _resident((D_MODEL, D_MODEL)),
        ],
        out_specs=pl.BlockSpec((tm, D_MODEL), row),
        out_shape=jax.ShapeDtypeStruct((n, D_MODEL), F32),
        compiler_params=_cparams(("parallel",)),
        name="gated_merge",
    )(x, ya, yb, yc, gl, b_gate, wa, wb, wc, wo)


def _w_in_pieces():
    offs = np.concatenate([[0], np.cumsum(IN_SIZES)])
    names = ("a_q", "a_k", "a_v", "i_q", "i_k", "i_w", "b_qkv", "b_f", "c_qkv", "c_z",
             "c_beta", "c_a", "gates")
    src = {n: (int(offs[i]), int(IN_SIZES[i])) for i, n in enumerate(names)}
    order = (("a_q", 0), ("i_q", A_Q_W), ("a_k", KEY_TILE * LANES),
             ("i_k", KEY_TILE * LANES + HEAD_DIM), ("a_v", VALUE_TILE * LANES),
             ("b_f", W_A + SC_BF), ("i_w", W_A + SC_IW), ("c_beta", W_A + SC_BETA),
             ("c_a", W_A + SC_DECAY), ("b_qkv", W_A + W_SC), ("c_qkv", W_A + W_SC + W_B),
             ("c_z", W_A + W_SC + W_B + W_CQKV), ("gates", W_A + W_SC + W_B + W_CQKV + W_CZ))
    pieces = []
    for name, dst in order:
        s0, width = src[name]
        done = 0
        while done < width:
            step = min(width - done, LANES - (dst + done) % LANES)
            pieces.append((dst + done, s0 + done, step))
            done += step
    return pieces


RELAYOUT_TM = 256


def _relayout_kernel(w_ref, tail_ref, o_ref, *, aligned_cols):
    by_tile = {}
    for dst, src, n in _w_in_pieces():
        by_tile.setdefault(dst // LANES, []).append((dst % LANES, src, n))
    row = lax.broadcasted_iota(jnp.int32, (2 * LANES, LANES), 0)
    col = lax.broadcasted_iota(jnp.int32, (2 * LANES, LANES), 1)

    def src_tile(start):
        if start + LANES <= aligned_cols:
            return w_ref[0, :, start:start + LANES]
        return tail_ref[0, :, start - aligned_cols:start - aligned_cols + LANES]

    for tile in range(W_TOTAL // LANES):
        acc = None
        for dst_lane, src, n in by_tile.get(tile, ()):
            base = (src // LANES) * LANES
            shift = src - base - dst_lane
            select = (row == col + shift) & (col >= dst_lane) & (col < dst_lane + n)
            window = jnp.concatenate([src_tile(base), src_tile(base + LANES)], axis=1)
            part = _dot(window.astype(BF16), jnp.where(select, 1.0, 0.0).astype(BF16))
            acc = part if acc is None else acc + part
        if acc is None:
            acc = jnp.zeros((w_ref.shape[1], LANES), F32)
        o_ref[0, :, tile * LANES:(tile + 1) * LANES] = acc.astype(BF16)


def _layout_w_in(w_in):
    depth, rows, cols = w_in.shape
    aligned_cols = (cols // LANES) * LANES
    tail = jnp.pad(w_in[:, :, aligned_cols:], ((0, 0), (0, 0), (0, 2 * LANES - (cols - aligned_cols))))
    tm = min(RELAYOUT_TM, rows)
    return pl.pallas_call(
        functools.partial(_relayout_kernel, aligned_cols=aligned_cols),
        grid=(depth, rows // tm),
        in_specs=[pl.BlockSpec((1, tm, cols), lambda l, i: (l, i, 0)),
                  pl.BlockSpec((1, tm, 2 * LANES), lambda l, i: (l, i, 0))],
        out_specs=pl.BlockSpec((1, tm, W_TOTAL), lambda l, i: (l, i, 0)),
        out_shape=jax.ShapeDtypeStruct((depth, rows, W_TOTAL), BF16),
        compiler_params=_cparams(("parallel", "parallel")),
        name="w_in_relayout",
    )(w_in, tail)


def _rotary_tables(seq):
    pos = jnp.arange(seq, dtype=F32)
    inv_freq = jnp.power(ROPE_THETA, -jnp.arange(0, ROT_DIM, 2, dtype=F32) / ROT_DIM)
    ang = pos[:, None] * inv_freq[None, :]
    cos, sin = jnp.cos(ang), jnp.sin(ang)
    half = ROT_DIM // 2
    ones = jnp.ones((seq, HEAD_DIM - ROT_DIM), F32)
    zeros_h = jnp.zeros((seq, half), F32)
    zeros_r = jnp.zeros((seq, HEAD_DIM - ROT_DIM), F32)
    c64 = jnp.concatenate([cos, cos, ones], axis=1)
    s1_64 = jnp.concatenate([zeros_h, sin, zeros_r], axis=1)
    s2_64 = jnp.concatenate([-sin, zeros_h, zeros_r], axis=1)
    twice = lambda t: jnp.concatenate([t, t], axis=1)
    return twice(c64), twice(s1_64), twice(s2_64)


def _lane_row(values, start):
    return jnp.zeros((1, LANES), F32).at[0, start:start + values.shape[0]].set(values.astype(F32))


def kernel(x, ffn1_norm, ffn1_w_in, ffn1_w_out, mix_norm, w_in, b_gate, b_forget, conv_w, a_log, dt_bias, delta_norm, w_branch_a, w_branch_b, w_branch_c, w_out, ffn2_norm, ffn2_w_in, ffn2_w_out, final_norm):
    batch, seq, _ = x.shape
    depth = w_in.shape[0]
    top_k = min(INDEX_TOPK, seq // 4)
    cos, s1, s2 = _rotary_tables(seq)
    w_mix = _layout_w_in(w_in)
    final_row = final_norm.reshape(1, D_MODEL)
    xt = x.reshape(batch * seq, D_MODEL)
    for l in range(depth):
        xt = _ffn(xt, ffn1_norm[l].reshape(1, D_MODEL), ffn1_w_in[l].astype(BF16),
                  ffn1_w_out[l].astype(BF16), final_row, False)
        a_grp, sc, b_grp, cqkv, cz, gl = _proj(
            xt, mix_norm[l].reshape(1, D_MODEL), w_mix[l], cos, s1, s2, seq)
        a_grp = a_grp.reshape(batch, seq, W_A)
        sc = sc.reshape(batch, seq, W_SC)
        y_a = _dsa_t(a_grp, sc, top_k)
        y_b = _fox_t(b_grp.reshape(batch, seq, W_B), sc, _lane_row(b_forget[l], SC_BF))
        y_c = _gdn(cqkv.reshape(batch, seq, W_CQKV), cz.reshape(batch, seq, W_CZ), sc,
                   conv_w[l], _lane_row(a_log[l], SC_DECAY), _lane_row(dt_bias[l], SC_DECAY),
                   delta_norm[l].reshape(1, C_DIM))
        flat = lambda t: t.reshape(batch * seq, BRANCH)
        xt = _merge(xt, flat(y_a), flat(y_b), flat(y_c), gl, b_gate[l].reshape(1, 3 * D_MODEL),
                    w_branch_a[l].astype(BF16), w_branch_b[l].astype(BF16),
                    w_branch_c[l].astype(BF16), w_out[l].astype(BF16))
        xt = _ffn(xt, ffn2_norm[l].reshape(1, D_MODEL), ffn2_w_in[l].astype(BF16),
                  ffn2_w_out[l].astype(BF16), final_row, l == depth - 1)
    return xt.reshape(batch, seq, D_MODEL)
```
